```python
import jax, jax.numpy as jnp
from jax import lax
import numpy as np

D_MODEL = 1024
BATCH = 16
SEQ = 4096
DEPTH = 2
DEC_BATCH = 32
DEC_SEQ = 2048
PAST_LEN = 128

D_MIX = D_MODEL
D_RG = D_MIX // 2
D_ML = D_MIX - D_RG
RG_BLOCKS = 8
RG_BW = D_RG // RG_BLOCKS
RG_C = 8.0
RG_CONV = 4
ML_HEADS = 4
ML_HD = D_ML // ML_HEADS
ML_CHUNK = 128
D_FF = 3 * D_MODEL
FFN_CONV = 3
N_GATE = 4 * ML_HEADS
N_IN = 2 * D_RG + 4 * D_ML + N_GATE
SPLITS = [D_RG, 2 * D_RG, 2 * D_RG + D_ML, 2 * D_RG + 2 * D_ML, 2 * D_RG + 3 * D_ML, 2 * D_RG + 4 * D_ML]
EPS = 1e-6

kernel_name = "hybrid_rglru_mlstm_bidir_encoder"


def rmsnorm(x, g):
    x32 = x.astype(jnp.float32)
    y = x32 * lax.rsqrt(jnp.mean(x32 * x32, axis=-1, keepdims=True) + EPS)
    return (y * g.astype(jnp.float32)).astype(x.dtype)


def dwconv(x, w, b, pad_l, pad_r):
    S = x.shape[1]
    xp = jnp.pad(x, ((0, 0), (pad_l, pad_r), (0, 0)))
    y = b + xp[:, 0:S] * w[0]
    for j in range(1, w.shape[0]):
        y = y + xp[:, j:j + S] * w[j]
    return y


def _lin_combine(e1, e2):
    a1, u1 = e1
    a2, u2 = e2
    return a1 * a2, a2 * u1 + u2


def rglru(xc, wa, ba, wx, bx, lam, reverse):
    B, S, _ = xc.shape
    xb = xc.reshape(B, S, RG_BLOCKS, RG_BW)
    r = jax.nn.sigmoid(jnp.einsum('bsnc,ncd->bsnd', xb, wa).reshape(B, S, D_RG) + ba)
    i = jax.nn.sigmoid(jnp.einsum('bsnc,ncd->bsnd', xb, wx).reshape(B, S, D_RG) + bx)
    log_a = -RG_C * jax.nn.softplus(-lam) * r
    a = jnp.exp(log_a)
    u = jnp.sqrt(-jnp.expm1(2.0 * log_a)) * (i * xc)
    _, h = lax.associative_scan(_lin_combine, (a, u), axis=1, reverse=reverse)
    return h


def mlstm_chunkwise(q, k, v, li, lf):
    B, S, H, dh = q.shape
    L = ML_CHUNK
    NC = S // L

    def vec_chunks(t):
        return t.reshape(B, NC, L, H, dh).transpose(1, 0, 3, 2, 4)

    def gate_chunks(t):
        return t.reshape(B, NC, L, H).transpose(1, 0, 3, 2)

    lower = jnp.tril(jnp.ones((L, L), dtype=bool))

    def step(carry, xs):
        C, n, m = carry
        qc, kc, vc, ic, fc = xs
        b = jnp.cumsum(fc, axis=-1)
        D = jnp.where(lower, b[..., :, None] - b[..., None, :] + ic[..., None, :], -jnp.inf)
        inter = b + m[..., None]
        m_t = jnp.maximum(inter, jnp.max(D, axis=-1))
        w_inter = jnp.exp(inter - m_t)
        s = jnp.einsum('bhtd,bhsd->bhts', qc, kc) * jnp.exp(D - m_t[..., None])
        num = jnp.einsum('bhts,bhsd->bhtd', s, vc) + w_inter[..., None] * jnp.einsum('bhtk,bhkv->bhtv', qc, C)
        den = jnp.sum(s, axis=-1) + w_inter * jnp.einsum('bhtk,bhk->bht', qc, n)
        h = num / jnp.maximum(jnp.abs(den), jnp.exp(-m_t))[..., None]
        b_last = b[..., -1]
        g = b_last[..., None] - b + ic
        m_new = jnp.maximum(b_last + m, jnp.max(g, axis=-1))
        decay = jnp.exp(b_last + m - m_new)
        wg = jnp.exp(g - m_new[..., None])
        C_new = decay[..., None, None] * C + jnp.einsum('bhs,bhsk,bhsv->bhkv', wg, kc, vc)
        n_new = decay[..., None] * n + jnp.einsum('bhs,bhsk->bhk', wg, kc)
        return (C_new, n_new, m_new), h

    init = (jnp.zeros((B, H, dh, dh), jnp.float32), jnp.zeros((B, H, dh), jnp.float32),
            jnp.zeros((B, H), jnp.float32))
    _, h = lax.scan(step, init, (vec_chunks(q), vec_chunks(k), vec_chunks(v), gate_chunks(li), gate_chunks(lf)))
    return h.transpose(1, 0, 3, 2, 4).reshape(B, S, H, dh)


def encoder_layer(x, norm1_g, w_in, b_gates, rg_conv_w, rg_conv_b, rg_wa, rg_ba, rg_wx, rg_bx,
                  rg_lambda, ml_norm_g, w_out, norm2_g, w_up, ffn_conv_w, ffn_conv_b, w_down):
    f32 = jnp.float32
    B, S, _ = x.shape
    h = rmsnorm(x, norm1_g)
    p = jnp.matmul(h, w_in)
    rg_x, rg_gate, q, k, v, o, gates = jnp.split(p, SPLITS, axis=-1)

    xc = dwconv(rg_x.astype(f32), rg_conv_w.astype(f32), rg_conv_b.astype(f32),
                RG_CONV // 2, RG_CONV - 1 - RG_CONV // 2)
    wa, ba, wx, bx, lam = (t.astype(f32) for t in (rg_wa, rg_ba, rg_wx, rg_bx, rg_lambda))
    h_rg = (rglru(xc, wa[0], ba[0], wx[0], bx[0], lam[0], False)
            + rglru(xc, wa[1], ba[1], wx[1], bx[1], lam[1], True))
    y_rg = h_rg * jax.nn.gelu(rg_gate.astype(f32))

    gates = gates.astype(f32).reshape(B, S, 4, ML_HEADS) + b_gates.astype(f32)
    qh = q.astype(f32).reshape(B, S, ML_HEADS, ML_HD) * (ML_HD ** -0.5)
    kh = k.astype(f32).reshape(B, S, ML_HEADS, ML_HD)
    vh = v.astype(f32).reshape(B, S, ML_HEADS, ML_HD)
    h_f = mlstm_chunkwise(qh, kh, vh, gates[:, :, 0], jax.nn.log_sigmoid(gates[:, :, 1]))
    flip = lambda t: jnp.flip(t, axis=1)
    h_b = flip(mlstm_chunkwise(flip(qh), flip(kh), flip(vh), flip(gates[:, :, 2]),
                               flip(jax.nn.log_sigmoid(gates[:, :, 3]))))
    h_ml = h_f + h_b
    h_ml = h_ml * lax.rsqrt(jnp.mean(h_ml * h_ml, axis=-1, keepdims=True) + EPS)
    y_ml = jax.nn.sigmoid(o.astype(f32)) * (h_ml.reshape(B, S, D_ML) * ml_norm_g.astype(f32))

    mix = jnp.concatenate([y_rg, y_ml], axis=-1).astype(x.dtype)
    x = x + jnp.matmul(mix, w_out)

    h = rmsnorm(x, norm2_g)
    uv = dwconv(jnp.matmul(h, w_up), ffn_conv_w, ffn_conv_b, FFN_CONV // 2, FFN_CONV // 2)
    gate, val = jnp.split(uv, 2, axis=-1)
    x = x + jnp.matmul(jax.nn.gelu(gate) * val, w_down)
    return x


def encoder(x, norm1_g, w_in, b_gates, rg_conv_w, rg_conv_b, rg_wa, rg_ba, rg_wx, rg_bx, rg_lambda,
            ml_norm_g, w_out, norm2_g, w_up, ffn_conv_w, ffn_conv_b, w_down, final_g):
    for l in range(DEPTH):
        x = encoder_layer(x, norm1_g[l], w_in[l], b_gates[l], rg_conv_w[l], rg_conv_b[l], rg_wa[l],
                          rg_ba[l], rg_wx[l], rg_bx[l], rg_lambda[l], ml_norm_g[l], w_out[l],
                          norm2_g[l], w_up[l], ffn_conv_w[l], ffn_conv_b[l], w_down[l])
    return rmsnorm(x, final_g)


def setup_inputs(seed: int = 0) -> dict:
    key = jax.random.key(seed)
    ks = jax.random.split(key, 24)
    nrm = jax.random.normal
    f_base = jnp.linspace(3.0, 6.0, ML_HEADS)
    u = jax.random.uniform(ks[11], (DEPTH, 2, D_RG), minval=0.9, maxval=0.999)
    pa = u ** (1.0 / RG_C)
    return {
        "x_prompt": nrm(ks[0], (BATCH, SEQ, D_MODEL), jnp.float32),
        "x_sample": nrm(ks[1], (DEC_BATCH, DEC_SEQ, D_MODEL), jnp.float32),
        "norm1_g": 1.0 + 0.02 * nrm(ks[2], (DEPTH, D_MODEL)),
        "w_in": nrm(ks[3], (DEPTH, D_MODEL, N_IN)) * D_MODEL ** -0.5,
        "b_gates": 0.1 * nrm(ks[4], (DEPTH, 4, ML_HEADS)) + jnp.array([0.0, 1.0, 0.0, 1.0])[:, None] * f_base[None, :],
        "rg_conv_w": nrm(ks[5], (DEPTH, RG_CONV, D_RG)) * RG_CONV ** -0.5,
        "rg_conv_b": 0.01 * nrm(ks[6], (DEPTH, D_RG)),
        "rg_wa": nrm(ks[7], (DEPTH, 2, RG_BLOCKS, RG_BW, RG_BW)) * RG_BW ** -0.5,
        "rg_ba": 0.01 * nrm(ks[8], (DEPTH, 2, D_RG)),
        "rg_wx": nrm(ks[9], (DEPTH, 2, RG_BLOCKS, RG_BW, RG_BW)) * RG_BW ** -0.5,
        "rg_bx": 0.01 * nrm(ks[10], (DEPTH, 2, D_RG)),
        "rg_lambda": jnp.log(pa) - jnp.log1p(-pa),
        "ml_norm_g": 1.0 + 0.02 * nrm(ks[12], (DEPTH, D_ML)),
        "w_out": nrm(ks[13], (DEPTH, D_MIX, D_MODEL)) * D_MIX ** -0.5,
        "norm2_g": 1.0 + 0.02 * nrm(ks[14], (DEPTH, D_MODEL)),
        "w_up": nrm(ks[15], (DEPTH, D_MODEL, 2 * D_FF)) * D_MODEL ** -0.5,
        "ffn_conv_w": nrm(ks[16], (DEPTH, FFN_CONV, 2 * D_FF)) * FFN_CONV ** -0.5,
        "ffn_conv_b": 0.01 * nrm(ks[17], (DEPTH, 2 * D_FF)),
        "w_down": nrm(ks[18], (DEPTH, D_FF, D_MODEL)) * D_FF ** -0.5,
        "final_g": 1.0 + 0.02 * nrm(ks[19], (D_MODEL,)),
    }


def reference(x_prompt, x_sample, norm1_g, w_in, b_gates, rg_conv_w, rg_conv_b, rg_wa, rg_ba, rg_wx,
              rg_bx, rg_lambda, ml_norm_g, w_out, norm2_g, w_up, ffn_conv_w, ffn_conv_b, w_down, final_g):
    y_prompt = encoder(x_prompt, norm1_g, w_in, b_gates, rg_conv_w, rg_conv_b, rg_wa, rg_ba, rg_wx, rg_bx,
                       rg_lambda, ml_norm_g, w_out, norm2_g, w_up, ffn_conv_w, ffn_conv_b, w_down, final_g)
    y_sample = encoder(x_sample, norm1_g, w_in, b_gates, rg_conv_w, rg_conv_b, rg_wa, rg_ba, rg_wx, rg_bx,
                       rg_lambda, ml_norm_g, w_out, norm2_g, w_up, ffn_conv_w, ffn_conv_b, w_down, final_g)
    return (y_prompt, y_sample)
```

```python
import functools

import jax
import jax.numpy as jnp
from jax import lax
from jax.experimental import pallas as pl
from jax.experimental.pallas import tpu as pltpu

F32 = jnp.float32
BF16 = jnp.bfloat16

D_MODEL = 1024
D_RG = 512
D_ML = 512
RG_BLOCKS = 8
RG_C = 8.0
ML_HEADS = 4
ML_HD = 128
ML_CHUNK = 128
D_FF = 3072
EPS = 1e-6
N_MAIN = 2 * D_RG + 4 * D_ML
N_GATE = 4 * ML_HEADS

SUBLANES = 8
LANES = 128
BF16_ROWS = 16
VMEM_LIMIT = 48 * 1024 * 1024

TM_PROJ = 512
TB_RG = 512
TB_FFN = 512
FF_CHUNK = 512


def _params(*sem):
    return pltpu.CompilerParams(dimension_semantics=sem, vmem_limit_bytes=VMEM_LIMIT)


def _softplus(z):
    return jnp.maximum(z, 0.0) + jnp.log1p(jnp.exp(-jnp.abs(z)))


def _gelu(x):
    return 0.5 * x * (1.0 + jnp.tanh(0.7978845608028654 * (x + 0.044715 * (x * x * x))))


def _rmsnorm(x, g):
    return x * lax.rsqrt(jnp.mean(x * x, axis=-1, keepdims=True) + EPS) * g


def _dot(a, b):
    return jnp.dot(a, b, preferred_element_type=F32)


def _dot_nt(a, b):
    return lax.dot_general(a, b, (((1,), (1,)), ((), ())), preferred_element_type=F32)


def _inproj_kernel(x_ref, g_ref, w_ref, wgt_ref, p_ref, gt_ref):
    h = _rmsnorm(x_ref[...], g_ref[...]).astype(BF16)
    p_ref[...] = _dot(h, w_ref[...])
    gt_ref[...] = _dot_nt(wgt_ref[...], h)


def _inproj(x2, g, w_main, w_gate_t):
    T = x2.shape[0]
    tm = TM_PROJ
    return pl.pallas_call(
        _inproj_kernel,
        grid=(T // tm,),
        in_specs=[
            pl.BlockSpec((tm, D_MODEL), lambda i: (i, 0)),
            pl.BlockSpec((1, D_MODEL), lambda i: (0, 0)),
            pl.BlockSpec((D_MODEL, N_MAIN), lambda i: (0, 0)),
            pl.BlockSpec((N_GATE, D_MODEL), lambda i: (0, 0)),
        ],
        out_specs=[
            pl.BlockSpec((tm, N_MAIN), lambda i: (i, 0)),
            pl.BlockSpec((N_GATE, tm), lambda i: (0, i)),
        ],
        out_shape=[
            jax.ShapeDtypeStruct((T, N_MAIN), F32),
            jax.ShapeDtypeStruct((N_GATE, T), F32),
        ],
        compiler_params=_params("parallel"),
        name="inproj",
    )(x2, g, w_main, w_gate_t)


def _rglru_kernel(x_ref, prev_ref, next_ref, cw_ref, cb_ref, wa_ref, wx_ref, ba_ref, bx_ref,
                  lam_ref, h_ref, xbuf, abuf, ubuf, carry, *, reverse, nblk, tb):
    j = pl.program_id(1)
    blk = (nblk - 1 - j) if reverse else j

    @pl.when(j == 0)
    def _():
        carry[...] = jnp.zeros_like(carry)

    xbuf[0:SUBLANES, :] = jnp.where(blk == 0, 0.0, prev_ref[...])
    xbuf[SUBLANES:SUBLANES + tb, :] = x_ref[...]
    xbuf[SUBLANES + tb:, :] = jnp.where(blk == nblk - 1, 0.0, next_ref[...])
    xc = cb_ref[...] + xbuf[pl.ds(SUBLANES - 2, tb), :] * cw_ref[0:1, :]
    xc = xc + xbuf[pl.ds(SUBLANES - 1, tb), :] * cw_ref[1:2, :]
    xc = xc + xbuf[pl.ds(SUBLANES, tb), :] * cw_ref[2:3, :]
    xc = xc + xbuf[pl.ds(SUBLANES + 1, tb), :] * cw_ref[3:4, :]

    xcb = xc.astype(BF16)
    r = jax.nn.sigmoid(_dot(xcb, wa_ref[...]) + ba_ref[...])
    i = jax.nn.sigmoid(_dot(xcb, wx_ref[...]) + bx_ref[...])
    log_a = (-RG_C * _softplus(-lam_ref[...])) * r
    a = jnp.exp(log_a)
    abuf[...] = a
    ubuf[...] = jnp.sqrt(-jnp.tanh(log_a) * (a * a + 1.0)) * (i * xc)

    row = lax.broadcasted_iota(jnp.int32, (SUBLANES, D_RG), 0)
    ngroups = tb // SUBLANES

    def group(gi, c):
        g = (ngroups - 1 - gi) if reverse else gi
        off = pl.multiple_of(g * SUBLANES, SUBLANES)
        A = abuf[pl.ds(off, SUBLANES), :]
        U = ubuf[pl.ds(off, SUBLANES), :]
        for s in (1, 2, 4):
            if reverse:
                keep = row < SUBLANES - s
                shift = SUBLANES - s
            else:
                keep = row >= s
                shift = s
            a_sh = jnp.where(keep, pltpu.roll(A, shift, 0), 1.0)
            u_sh = jnp.where(keep, pltpu.roll(U, shift, 0), 0.0)
            U = A * u_sh + U
            A = A * a_sh
        hg = U + A * c
        h_ref[pl.ds(off, SUBLANES), :] = hg
        edge = hg[0:1, :] if reverse else hg[SUBLANES - 1:SUBLANES, :]
        return jnp.broadcast_to(edge, (SUBLANES, D_RG))

    carry[...] = lax.fori_loop(0, ngroups, group, carry[...])


def _rglru(p, cw, cb, wa, wx, ba, bx, lam, *, B, S, reverse):
    T = B * S
    tb = TB_RG
    nblk = S // tb
    hb = tb // SUBLANES
    n_halo = T // SUBLANES

    def blk_of(j):
        return (nblk - 1 - j) if reverse else j

    def main_map(b, j):
        return (b * nblk + blk_of(j), 0)

    def prev_map(b, j):
        return (jnp.maximum((b * nblk + blk_of(j)) * hb - 1, 0), 0)

    def next_map(b, j):
        return (jnp.minimum((b * nblk + blk_of(j) + 1) * hb, n_halo - 1), 0)

    full = lambda shape: pl.BlockSpec(shape, lambda b, j: (0,) * len(shape))
    kern = functools.partial(_rglru_kernel, reverse=reverse, nblk=nblk, tb=tb)
    return pl.pallas_call(
        kern,
        grid=(B, nblk),
        in_specs=[
            pl.BlockSpec((tb, D_RG), main_map),
            pl.BlockSpec((SUBLANES, D_RG), prev_map),
            pl.BlockSpec((SUBLANES, D_RG), next_map),
            full((4, D_RG)), full((1, D_RG)),
            full((D_RG, D_RG)), full((D_RG, D_RG)),
            full((1, D_RG)), full((1, D_RG)), full((1, D_RG)),
        ],
        out_specs=pl.BlockSpec((tb, D_RG), main_map),
        out_shape=jax.ShapeDtypeStruct((T, D_RG), F32),
        scratch_shapes=[
            pltpu.VMEM((tb + 2 * SUBLANES, D_RG), F32),
            pltpu.VMEM((tb, D_RG), F32),
            pltpu.VMEM((tb, D_RG), F32),
            pltpu.VMEM((SUBLANES, D_RG), F32),
        ],
        compiler_params=_params("parallel", "arbitrary"),
        name="rglru_bwd" if reverse else "rglru_fwd",
    )(p, p, p, cw, cb, wa, wx, ba, bx, lam)


def _mlstm_kernel(q_ref, k_ref, v_ref, gt_ref, bias_ref, h_ref, c_st, n_st, m_st, *, reverse):
    L = ML_CHUNK

    @pl.when(pl.program_id(1) == 0)
    def _():
        c_st[...] = jnp.zeros_like(c_st)
        n_st[...] = jnp.zeros_like(n_st)
        m_st[...] = jnp.zeros_like(m_st)

    r0 = 2 * ML_HEADS if reverse else 0
    gates = gt_ref[r0:r0 + 2 * ML_HEADS, :] + bias_ref[r0:r0 + 2 * ML_HEADS, :]
    rowid = lax.broadcasted_iota(jnp.int32, (2 * ML_HEADS, L), 0)
    lane = lax.broadcasted_iota(jnp.int32, (2 * ML_HEADS, L), 1)
    cum = jnp.where(rowid >= ML_HEADS, -_softplus(-gates), 0.0)
    s = 1
    while s < L:
        if reverse:
            cum = cum + jnp.where(lane < L - s, pltpu.roll(cum, L - s, 1), 0.0)
        else:
            cum = cum + jnp.where(lane >= s, pltpu.roll(cum, s, 1), 0.0)
        s *= 2
    rows = jnp.where(rowid >= ML_HEADS, cum, gates)
    cols = jnp.concatenate([rows, jnp.zeros((L - 2 * ML_HEADS, L), F32)], axis=0).T

    t_id = lax.broadcasted_iota(jnp.int32, (L, L), 0)
    s_id = lax.broadcasted_iota(jnp.int32, (L, L), 1)
    causal = (s_id >= t_id) if reverse else (s_id <= t_id)
    last = 0 if reverse else L - 1

    for hd in range(ML_HEADS):
        cs = slice(hd * ML_HD, (hd + 1) * ML_HD)
        i_row = rows[hd:hd + 1, :]
        b_row = rows[ML_HEADS + hd:ML_HEADS + hd + 1, :]
        i_col = cols[:, hd:hd + 1]
        b_col = cols[:, ML_HEADS + hd:ML_HEADS + hd + 1]
        m_prev = m_st[hd:hd + 1, 0:1]
        n_prev = n_st[hd:hd + 1, :]

        qf = q_ref[:, cs] * (ML_HD ** -0.5)
        qb = qf.astype(BF16)
        kf = k_ref[:, cs]
        kb = kf.astype(BF16)
        vb = v_ref[:, cs].astype(BF16)

        dmat = jnp.where(causal, b_col - b_row + i_row, -jnp.inf)
        inter = b_col + m_prev
        m_t = jnp.maximum(inter, jnp.max(dmat, axis=-1, keepdims=True))
        w_inter = jnp.exp(inter - m_t)
        sc = _dot_nt(qb, kb) * jnp.exp(dmat - m_t)
        num = _dot(sc.astype(BF16), vb) + w_inter * _dot(qb, c_st[hd].astype(BF16))
        den = jnp.sum(sc, axis=-1, keepdims=True) + w_inter * jnp.sum(qf * n_prev, axis=-1, keepdims=True)
        h_ref[:, cs] = num / jnp.maximum(jnp.abs(den), jnp.exp(-m_t))

        b_last = b_row[:, last:last + 1]
        g_row = b_last - b_row + i_row
        g_col = b_last - b_col + i_col
        m_new = jnp.maximum(b_last + m_prev, jnp.max(g_row, axis=-1, keepdims=True))
        decay = jnp.exp(b_last + m_prev - m_new)
        kw = jnp.exp(g_col - m_new) * kf
        c_st[hd] = decay * c_st[hd] + _dot(kw.T.astype(BF16), vb)
        n_st[hd:hd + 1, :] = decay * n_prev + jnp.sum(kw, axis=0, keepdims=True)
        m_st[hd:hd + 1, :] = jnp.broadcast_to(m_new, (1, LANES))


def _mlstm(p, gt, bias, *, B, S, reverse):
    T = B * S
    L = ML_CHUNK
    nc = S // L

    def row(b, c):
        return b * nc + ((nc - 1 - c) if reverse else c)

    qkv = lambda col: pl.BlockSpec((L, D_ML), lambda b, c: (row(b, c), col))
    kern = functools.partial(_mlstm_kernel, reverse=reverse)
    return pl.pallas_call(
        kern,
        grid=(B, nc),
        in_specs=[
            qkv(2), qkv(3), qkv(4),
            pl.BlockSpec((N_GATE, L), lambda b, c: (0, row(b, c))),
            pl.BlockSpec((N_GATE, L), lambda b, c: (0, 0)),
        ],
        out_specs=pl.BlockSpec((L, D_ML), lambda b, c: (row(b, c), 0)),
        out_shape=jax.ShapeDtypeStruct((T, D_ML), F32),
        scratch_shapes=[
            pltpu.VMEM((ML_HEADS, ML_HD, ML_HD), F32),
            pltpu.VMEM((SUBLANES, ML_HD), F32),
            pltpu.VMEM((SUBLANES, LANES), F32),
        ],
        compiler_params=_params("parallel", "arbitrary"),
        name="mlstm_bwd" if reverse else "mlstm_fwd",
    )(p, p, p, gt, bias)


def _outproj_kernel(x_ref, rf_ref, rb_ref, gate_ref, mf_ref, mb_ref, o_ref, mg_ref, wr_ref, wm_ref,
                    y_ref):
    y_rg = (rf_ref[...] + rb_ref[...]) * _gelu(gate_ref[...])
    acc = _dot(y_rg.astype(BF16), wr_ref[...])
    h_ml = mf_ref[...] + mb_ref[...]
    parts = []
    for hd in range(ML_HEADS):
        hh = h_ml[:, hd * ML_HD:(hd + 1) * ML_HD]
        parts.append(hh * lax.rsqrt(jnp.mean(hh * hh, axis=-1, keepdims=True) + EPS))
    y_ml = jax.nn.sigmoid(o_ref[...]) * (jnp.concatenate(parts, axis=-1) * mg_ref[...])
    acc = acc + _dot(y_ml.astype(BF16), wm_ref[...])
    y_ref[...] = x_ref[...] + acc


def _outproj(x2, rf, rb, p, mf, mb, mg, w_rg, w_ml):
    T = x2.shape[0]
    tm = TM_PROJ
    tok = lambda width, col: pl.BlockSpec((tm, width), lambda i: (i, col))
    full = lambda shape: pl.BlockSpec(shape, lambda i: (0,) * len(shape))
    return pl.pallas_call(
        _outproj_kernel,
        grid=(T // tm,),
        in_specs=[
            tok(D_MODEL, 0), tok(D_RG, 0), tok(D_RG, 0), tok(D_RG, 1),
            tok(D_ML, 0), tok(D_ML, 0), tok(D_ML, 5),
            full((1, D_ML)), full((D_RG, D_MODEL)), full((D_ML, D_MODEL)),
        ],
        out_specs=tok(D_MODEL, 0),
        out_shape=jax.ShapeDtypeStruct((T, D_MODEL), F32),
        compiler_params=_params("parallel"),
        name="outproj",
    )(x2, rf, rb, p, mf, mb, p, mg, w_rg, w_ml)


def _ffn_kernel(x_ref, prev_ref, next_ref, g_ref, wug_ref, wuv_ref, cwg_ref, cwv_ref, cbg_ref,
                cbv_ref, wd_ref, fg_ref, y_ref, hbuf, uvg, uvv, acc, *, nblk, tb, final):
    i = pl.program_id(0)
    j = pl.program_id(1)
    blk = i % nblk
    H = BF16_ROWS

    @pl.when(j == 0)
    def _():
        g = g_ref[...]
        hbuf[0:H, :] = jnp.where(blk == 0, 0.0, _rmsnorm(prev_ref[...], g)).astype(BF16)
        hbuf[H:H + tb, :] = _rmsnorm(x_ref[...], g).astype(BF16)
        hbuf[H + tb:, :] = jnp.where(blk == nblk - 1, 0.0, _rmsnorm(next_ref[...], g)).astype(BF16)
        acc[...] = jnp.zeros_like(acc)

    hb = hbuf[...]
    uvg[...] = _dot(hb, wug_ref[...])
    uvv[...] = _dot(hb, wuv_ref[...])

    def conv(buf, cw_ref, cb_ref):
        c = cb_ref[...] + buf[pl.ds(H - 1, tb), :] * cw_ref[0:1, :]
        c = c + buf[pl.ds(H, tb), :] * cw_ref[1:2, :]
        return c + buf[pl.ds(H + 1, tb), :] * cw_ref[2:3, :]

    act = _gelu(conv(uvg, cwg_ref, cbg_ref)) * conv(uvv, cwv_ref, cbv_ref)
    acc[...] += _dot(act.astype(BF16), wd_ref[...])

    @pl.when(j == pl.num_programs(1) - 1)
    def _():
        y = x_ref[...] + acc[...]
        if final:
            y = _rmsnorm(y, fg_ref[...])
        y_ref[...] = y


def _ffn(x2, g, w_up, cw, cb, w_down, fg, *, S, final):
    T = x2.shape[0]
    tb = TB_FFN
    cc = FF_CHUNK
    nblk = S // tb
    nj = D_FF // cc
    hpb = tb // BF16_ROWS
    n_halo = T // BF16_ROWS
    kern = functools.partial(_ffn_kernel, nblk=nblk, tb=tb, final=final)
    return pl.pallas_call(
        kern,
        grid=(T // tb, nj),
        in_specs=[
            pl.BlockSpec((tb, D_MODEL), lambda i, j: (i, 0)),
            pl.BlockSpec((BF16_ROWS, D_MODEL), lambda i, j: (jnp.maximum(i * hpb - 1, 0), 0)),
            pl.BlockSpec((BF16_ROWS, D_MODEL), lambda i, j: (jnp.minimum((i + 1) * hpb, n_halo - 1), 0)),
            pl.BlockSpec((1, D_MODEL), lambda i, j: (0, 0)),
            pl.BlockSpec((D_MODEL, cc), lambda i, j: (0, j)),
            pl.BlockSpec((D_MODEL, cc), lambda i, j: (0, nj + j)),
            pl.BlockSpec((3, cc), lambda i, j: (0, j)),
            pl.BlockSpec((3, cc), lambda i, j: (0, nj + j)),
            pl.BlockSpec((1, cc), lambda i, j: (0, j)),
            pl.BlockSpec((1, cc), lambda i, j: (0, nj + j)),
            pl.BlockSpec((cc, D_MODEL), lambda i, j: (j, 0)),
            pl.BlockSpec((1, D_MODEL), lambda i, j: (0, 0)),
        ],
        out_specs=pl.BlockSpec((tb, D_MODEL), lambda i, j: (i, 0)),
        out_shape=jax.ShapeDtypeStruct((T, D_MODEL), F32),
        scratch_shapes=[
            pltpu.VMEM((tb + 2 * BF16_ROWS, D_MODEL), BF16),
            pltpu.VMEM((tb + 2 * BF16_ROWS, cc), F32),
            pltpu.VMEM((tb + 2 * BF16_ROWS, cc), F32),
            pltpu.VMEM((tb, D_MODEL), F32),
        ],
        compiler_params=_params("parallel", "arbitrary"),
        name="convffn",
    )(x2, x2, x2, g, w_up, w_up, cw, cw, cb, cb, w_down, fg)


def _block_diag(w):
    eye = jnp.eye(RG_BLOCKS, dtype=w.dtype)
    return jnp.einsum('ncd,nm->ncmd', w, eye).reshape(D_RG, D_RG)


def _encoder(x, norm1_g, w_in, b_gates, rg_conv_w, rg_conv_b, rg_wa, rg_ba, rg_wx, rg_bx, rg_lambda,
             ml_norm_g, w_out, norm2_g, w_up, ffn_conv_w, ffn_conv_b, w_down, final_g):
    B, S, _ = x.shape
    depth = w_in.shape[0]
    x2 = x.reshape(B * S, D_MODEL)
    row = lambda v: v.reshape(1, -1).astype(F32)
    for l in range(depth):
        w_main = w_in[l, :, :N_MAIN].astype(BF16)
        w_gate_t = w_in[l, :, N_MAIN:].T.astype(BF16)
        bias = jnp.broadcast_to(b_gates[l].astype(F32).reshape(N_GATE, 1), (N_GATE, ML_CHUNK))
        p, gt = _inproj(x2, row(norm1_g[l]), w_main, w_gate_t)
        r_dir, m_dir = [], []
        for d, reverse in enumerate((False, True)):
            r_dir.append(_rglru(
                p, rg_conv_w[l].astype(F32), row(rg_conv_b[l]),
                _block_diag(rg_wa[l, d]).astype(BF16), _block_diag(rg_wx[l, d]).astype(BF16),
                row(rg_ba[l, d]), row(rg_bx[l, d]), row(rg_lambda[l, d]), B=B, S=S, reverse=reverse))
            m_dir.append(_mlstm(p, gt, bias, B=B, S=S, reverse=reverse))
        wo = w_out[l].astype(BF16)
        x2 = _outproj(x2, r_dir[0], r_dir[1], p, m_dir[0], m_dir[1], row(ml_norm_g[l]),
                      wo[:D_RG], wo[D_RG:])
        x2 = _ffn(x2, row(norm2_g[l]), w_up[l].astype(BF16), ffn_conv_w[l].astype(F32),
                  row(ffn_conv_b[l]), w_down[l].astype(BF16), row(final_g), S=S,
                  final=(l == depth - 1))
    return x2.reshape(B, S, D_MODEL)


def kernel(x_prompt, x_sample, norm1_g, w_in, b_gates, rg_conv_w, rg_conv_b, rg_wa, rg_ba, rg_wx, rg_bx,
           rg_lambda, ml_norm_g, w_out, norm2_g, w_up, ffn_conv_w, ffn_conv_b, w_down, final_g):
    weights = (norm1_g, w_in, b_gates, rg_conv_w, rg_conv_b, rg_wa, rg_ba, rg_wx, rg_bx, rg_lambda,
               ml_norm_g, w_out, norm2_g, w_up, ffn_conv_w, ffn_conv_b, w_down, final_g)
    return (_encoder(x_prompt, *weights), _encoder(x_sample, *weights))
```

```python
import functools

import jax
import jax.numpy as jnp
from jax import lax
from jax.experimental import pallas as pl
from jax.experimental.pallas import tpu as pltpu

F32 = jnp.float32
BF16 = jnp.bfloat16

D_MODEL = 1024
D_RG = 512
D_ML = 512
RG_BLOCKS = 8
RG_C = 8.0
ML_HEADS = 4
ML_HD = 128
ML_CHUNK = 128
D_FF = 3072
EPS = 1e-6
N_GATE = 4 * ML_HEADS

SUBLANES = 8
LANES = 128
BF16_ROWS = 16
VMEM_LIMIT = 48 * 1024 * 1024

TM_PROJ = 512
TB_RG = 512
ML_SEQS = 4
TB_FFN = 512
FF_SUB = 256
FF_RING = 4
VMEM_LIMIT_FFN = 56 * 1024 * 1024

G_B, G_U, G_A, G_BL, G_GM = 0, 8, 16, 24, 32
G_ROWS = 40
UC_DIR = 24
UC_ONES = 2 * UC_DIR


def _params(*sem):
    return pltpu.CompilerParams(dimension_semantics=sem, vmem_limit_bytes=VMEM_LIMIT)


def _softplus(z):
    return jnp.maximum(z, 0.0) + jnp.log1p(jnp.exp(-jnp.abs(z)))


def _gelu(x):
    return 0.5 * x * (1.0 + jnp.tanh(0.7978845608028654 * (x + 0.044715 * (x * x * x))))


def _rmsnorm(x, g):
    return x * lax.rsqrt(jnp.mean(x * x, axis=-1, keepdims=True) + EPS) * g


def _dot(a, b):
    return jnp.dot(a, b, preferred_element_type=F32)


def _dot_nt(a, b):
    return lax.dot_general(a, b, (((1,), (1,)), ((), ())), preferred_element_type=F32)


def _split3(x):
    hi = x.astype(BF16).astype(F32)
    r1 = x - hi
    mid = r1.astype(BF16).astype(F32)
    return hi, mid, (r1 - mid).astype(BF16).astype(F32)


def _lane_scan(x, op, fill, reverse):
    n = x.shape[-1]
    lane = lax.broadcasted_iota(jnp.int32, x.shape, 1)
    s = 1
    while s < n:
        if reverse:
            x = op(x, jnp.where(lane < n - s, pltpu.roll(x, n - s, 1), fill))
        else:
            x = op(x, jnp.where(lane >= s, pltpu.roll(x, s, 1), fill))
        s *= 2
    return x


def _inproj_kernel(x_ref, g_ref, wn_ref, wt_ref, bias_ref, pa_ref, qk_ref, vt_ref, ot_ref, gr_ref,
                   uc_ref):
    L = ML_CHUNK
    NH = ML_HEADS
    h = _rmsnorm(x_ref[...], g_ref[...]).astype(BF16)
    nat = _dot(h, wn_ref[...])
    pa_ref[...] = nat[:, :2 * D_RG]
    qk_ref[...] = jnp.concatenate(
        [nat[:, 2 * D_RG:2 * D_RG + D_ML] * (ML_HD ** -0.5), nat[:, 2 * D_RG + D_ML:]], axis=-1).astype(BF16)
    tr = _dot_nt(wt_ref[...], h)
    vt_ref[...] = tr[:D_ML].astype(BF16)
    ot_ref[...] = tr[D_ML:2 * D_ML]

    rowid = lax.broadcasted_iota(jnp.int32, (2 * NH, L), 0)
    head_row = rowid < NH
    rep = lambda col: jnp.broadcast_to(col, (2 * NH, L))
    zeros8 = jnp.zeros((2 * NH, L), F32)
    for c in range(gr_ref.shape[0]):
        g16 = tr[2 * D_ML:, c * L:(c + 1) * L] + bias_ref[...]
        tiles = []
        for d, reverse in enumerate((False, True)):
            gates = g16[2 * NH * d:2 * NH * (d + 1)]
            lf = jnp.where(head_row, 0.0, -_softplus(-gates))
            bcum = pltpu.roll(_lane_scan(lf, jnp.add, 0.0, reverse), NH, 0)
            u = jnp.where(head_row, gates - bcum, 0.0)
            last = 0 if reverse else L - 1
            bl = rep(bcum[:, last:last + 1])
            base = G_ROWS * d
            gr_ref[c, base + G_B:base + G_B + 8, :] = bcum
            gr_ref[c, base + G_U:base + G_U + 8, :] = u
            gr_ref[c, base + G_A:base + G_A + 8, :] = bcum + _lane_scan(u, jnp.maximum, -jnp.inf, reverse)
            gr_ref[c, base + G_BL:base + G_BL + 8, :] = bl
            gr_ref[c, base + G_GM:base + G_GM + 8, :] = rep(jnp.max(bl + u, axis=-1, keepdims=True))
            tiles.extend(_split3(u))
        tiles.append(jnp.ones((2 * NH, L), F32))
        tiles.extend([zeros8] * (L // 8 - len(tiles)))
        uc_ref[c * L:(c + 1) * L, :] = jnp.concatenate(tiles, axis=0).T.astype(BF16)


def _inproj(x2, g, w_nat, w_tr, bias, *, B, S):
    T = B * S
    tm = TM_PROJ
    nb = S // tm
    nat_w = w_nat.shape[1]
    full = lambda shape: pl.BlockSpec(shape, lambda i: (0,) * len(shape))
    seq_t = pl.BlockSpec((None, D_ML, tm), lambda i: (i // nb, 0, i % nb))
    return pl.pallas_call(
        _inproj_kernel,
        grid=(T // tm,),
        in_specs=[
            pl.BlockSpec((tm, D_MODEL), lambda i: (i, 0)),
            full((1, D_MODEL)), full((D_MODEL, nat_w)), full(w_tr.shape), full((N_GATE, ML_CHUNK)),
        ],
        out_specs=[
            pl.BlockSpec((tm, 2 * D_RG), lambda i: (i, 0)),
            pl.BlockSpec((tm, 2 * D_ML), lambda i: (i, 0)),
            seq_t, seq_t,
            pl.BlockSpec((tm // ML_CHUNK, 2 * G_ROWS, ML_CHUNK), lambda i: (i, 0, 0)),
            pl.BlockSpec((tm, LANES), lambda i: (i, 0)),
        ],
        out_shape=[
            jax.ShapeDtypeStruct((T, 2 * D_RG), F32),
            jax.ShapeDtypeStruct((T, 2 * D_ML), BF16),
            jax.ShapeDtypeStruct((B, D_ML, S), BF16),
            jax.ShapeDtypeStruct((B, D_ML, S), F32),
            jax.ShapeDtypeStruct((T // ML_CHUNK, 2 * G_ROWS, ML_CHUNK), F32),
            jax.ShapeDtypeStruct((T, LANES), BF16),
        ],
        compiler_params=_params("parallel"),
        name="inproj",
    )(x2, g, w_nat, w_tr, bias)


def _rglru_kernel(x_ref, prev_ref, next_ref, cw_ref, cb_ref, wa_ref, wx_ref, ba_ref, bx_ref,
                  lam_ref, h_ref, xbuf, abuf, ubuf, carry, *, reverse, nblk, tb):
    j = pl.program_id(1)
    blk = (nblk - 1 - j) if reverse else j

    @pl.when(j == 0)
    def _():
        carry[...] = jnp.zeros_like(carry)

    xbuf[0:SUBLANES, :] = jnp.where(blk == 0, 0.0, prev_ref[...])
    xbuf[SUBLANES:SUBLANES + tb, :] = x_ref[...]
    xbuf[SUBLANES + tb:, :] = jnp.where(blk == nblk - 1, 0.0, next_ref[...])
    xc = cb_ref[...] + xbuf[pl.ds(SUBLANES - 2, tb), :] * cw_ref[0:1, :]
    xc = xc + xbuf[pl.ds(SUBLANES - 1, tb), :] * cw_ref[1:2, :]
    xc = xc + xbuf[pl.ds(SUBLANES, tb), :] * cw_ref[2:3, :]
    xc = xc + xbuf[pl.ds(SUBLANES + 1, tb), :] * cw_ref[3:4, :]

    xcb = xc.astype(BF16)
    r = jax.nn.sigmoid(_dot(xcb, wa_ref[...]) + ba_ref[...])
    i = jax.nn.sigmoid(_dot(xcb, wx_ref[...]) + bx_ref[...])
    log_a = (-RG_C * _softplus(-lam_ref[...])) * r
    a = jnp.exp(log_a)
    abuf[...] = a
    ubuf[...] = jnp.sqrt(-jnp.tanh(log_a) * (a * a + 1.0)) * (i * xc)

    row = lax.broadcasted_iota(jnp.int32, (SUBLANES, D_RG), 0)
    ngroups = tb // SUBLANES

    def group(gi, c):
        g = (ngroups - 1 - gi) if reverse else gi
        off = pl.multiple_of(g * SUBLANES, SUBLANES)
        A = abuf[pl.ds(off, SUBLANES), :]
        U = ubuf[pl.ds(off, SUBLANES), :]
        for s in (1, 2, 4):
            if reverse:
                keep = row < SUBLANES - s
                shift = SUBLANES - s
            else:
                keep = row >= s
                shift = s
            a_sh = jnp.where(keep, pltpu.roll(A, shift, 0), 1.0)
            u_sh = jnp.where(keep, pltpu.roll(U, shift, 0), 0.0)
            U = A * u_sh + U
            A = A * a_sh
        hg = U + A * c
        h_ref[pl.ds(off, SUBLANES), :] = hg
        edge = hg[0:1, :] if reverse else hg[SUBLANES - 1:SUBLANES, :]
        return jnp.broadcast_to(edge, (SUBLANES, D_RG))

    carry[...] = lax.fori_loop(0, ngroups, group, carry[...])


def _rglru(p, cw, cb, wa, wx, ba, bx, lam, *, B, S, reverse):
    T = B * S
    tb = TB_RG
    nblk = S // tb
    hb = tb // SUBLANES
    n_halo = T // SUBLANES

    def blk_of(j):
        return (nblk - 1 - j) if reverse else j

    def main_map(b, j):
        return (b * nblk + blk_of(j), 0)

    def prev_map(b, j):
        return (jnp.maximum((b * nblk + blk_of(j)) * hb - 1, 0), 0)

    def next_map(b, j):
        return (jnp.minimum((b * nblk + blk_of(j) + 1) * hb, n_halo - 1), 0)

    full = lambda shape: pl.BlockSpec(shape, lambda b, j: (0,) * len(shape))
    kern = functools.partial(_rglru_kernel, reverse=reverse, nblk=nblk, tb=tb)
    return pl.pallas_call(
        kern,
        grid=(B, nblk),
        in_specs=[
            pl.BlockSpec((tb, D_RG), main_map),
            pl.BlockSpec((SUBLANES, D_RG), prev_map),
            pl.BlockSpec((SUBLANES, D_RG), next_map),
            full((4, D_RG)), full((1, D_RG)),
            full((D_RG, D_RG)), full((D_RG, D_RG)),
            full((1, D_RG)), full((1, D_RG)), full((1, D_RG)),
        ],
        out_specs=pl.BlockSpec((tb, D_RG), main_map),
        out_shape=jax.ShapeDtypeStruct((T, D_RG), F32),
        scratch_shapes=[
            pltpu.VMEM((tb + 2 * SUBLANES, D_RG), F32),
            pltpu.VMEM((tb, D_RG), F32),
            pltpu.VMEM((tb, D_RG), F32),
            pltpu.VMEM((SUBLANES, D_RG), F32),
        ],
        compiler_params=_params("parallel", "arbitrary"),
        name="rglru_bwd" if reverse else "rglru_fwd",
    )(p, p, p, cw, cb, wa, wx, ba, bx, lam)


def _mlstm_kernel(q_ref, k_ref, vt_ref, gr_ref, uc_ref, h_ref, ct_st, n_st, m_st, *, reverse, bb):
    L = ML_CHUNK
    NH = ML_HEADS
    d = 1 if reverse else 0

    @pl.when(pl.program_id(1) == 0)
    def _():
        ct_st[...] = jnp.zeros_like(ct_st)
        n_st[...] = jnp.zeros_like(n_st)
        m_st[...] = jnp.zeros_like(m_st)

    row8 = lax.broadcasted_iota(jnp.int32, (SUBLANES, L), 0)
    s_id = lax.broadcasted_iota(jnp.int32, (L, L), 0)
    t_id = lax.broadcasted_iota(jnp.int32, (L, L), 1)
    valid = (s_id >= t_id) if reverse else (s_id <= t_id)
    zeros8 = jnp.zeros((SUBLANES, L), F32)
    tile = lambda rows, hd: jnp.broadcast_to(rows[hd:hd + 1, :], (L, L))
    row_of = lambda rows, hd: jnp.broadcast_to(rows[hd:hd + 1, :], (SUBLANES, L))
    pairs = [(b, hd) for b in range(bb) for hd in range(NH)]

    seq = []
    for b in range(bb):
        base = G_ROWS * d
        bcum = gr_ref[b, base + G_B:base + G_B + 8, :]
        u = gr_ref[b, base + G_U:base + G_U + 8, :]
        b_last = gr_ref[b, base + G_BL:base + G_BL + 8, :]
        m_prev = m_st[b]
        inter = bcum + m_prev
        m_t = jnp.maximum(inter, gr_ref[b, base + G_A:base + G_A + 8, :])
        m_new = jnp.maximum(b_last + m_prev, gr_ref[b, base + G_GM:base + G_GM + 8, :])
        n_prev = n_st[b]
        wg = jnp.exp(u + (b_last - m_new))
        seq.append(dict(
            w_int=jnp.exp(inter - m_t), e_neg=jnp.exp(-m_t), v3=_split3(bcum - m_t),
            decay=jnp.exp(b_last + m_prev - m_new), m_new=m_new, n_prev=n_prev, wg=wg,
            n_lhs=jnp.concatenate([n_prev, zeros8], axis=0).astype(BF16),
            wg_lhs=jnp.concatenate([wg, zeros8], axis=0).astype(BF16)))

    st, expo = {}, {}
    for b, hd in pairs:
        cs = slice(hd * ML_HD, (hd + 1) * ML_HD)
        st[b, hd] = _dot_nt(jnp.concatenate([k_ref[b, :, cs], seq[b]["n_lhs"]], axis=0), q_ref[b, :, cs])
        onehot = jnp.where(row8 == hd, 1.0, 0.0)
        v_hi, v_mid, v_lo = (row_of(x, hd) for x in seq[b]["v3"])
        v_rows = jnp.where(row8 == 0, v_hi, jnp.where(row8 == 1, v_mid, jnp.where(row8 == 2, v_lo, 0.0)))
        slabs = [zeros8] * (L // SUBLANES)
        for j in range(3):
            slabs[(UC_DIR * d) // SUBLANES + j] = onehot
        slabs[UC_ONES // SUBLANES] = v_rows
        expo[b, hd] = _dot(uc_ref[b], jnp.concatenate(slabs, axis=0).astype(BF16))

    for b, hd in pairs:
        cs = slice(hd * ML_HD, (hd + 1) * ML_HD)
        sq = seq[b]
        s_t = st[b, hd][0:L] * jnp.exp(jnp.where(valid, expo[b, hd], -jnp.inf))
        w_h = sq["w_int"][hd:hd + 1, :]
        den = jnp.sum(s_t, axis=0, keepdims=True) + w_h * st[b, hd][L + hd:L + hd + 1, :]
        r = 1.0 / jnp.maximum(jnp.abs(den), sq["e_neg"][hd:hd + 1, :])
        h_ref[b, cs, :] = (_dot(vt_ref[b, cs, :], (s_t * r).astype(BF16))
                           + _dot_nt(ct_st[b, hd].astype(BF16), q_ref[b, :, cs]) * (r * w_h))

    for b, hd in pairs:
        cs = slice(hd * ML_HD, (hd + 1) * ML_HD)
        sq = seq[b]
        vw = (vt_ref[b, cs, :].astype(F32) * sq["wg"][hd:hd + 1, :]).astype(BF16)
        upd = _dot(jnp.concatenate([vw, sq["wg_lhs"]], axis=0), k_ref[b, :, cs])
        ct_st[b, hd] = tile(sq["decay"], hd) * ct_st[b, hd] + upd[0:L]
        n_st[b, hd:hd + 1, :] = (sq["decay"][hd:hd + 1, :] * sq["n_prev"][hd:hd + 1, :]
                                 + upd[L + hd:L + hd + 1, :])
    for b in range(bb):
        m_st[b] = seq[b]["m_new"]


def _mlstm(qk3, vt, gr4, uc3, *, reverse):
    B, S, _ = qk3.shape
    L = ML_CHUNK
    nc = S // L
    bb = ML_SEQS

    def chunk(c):
        return (nc - 1 - c) if reverse else c

    kern = functools.partial(_mlstm_kernel, reverse=reverse, bb=bb)
    return pl.pallas_call(
        kern,
        grid=(B // bb, nc),
        in_specs=[
            pl.BlockSpec((bb, L, D_ML), lambda b, c: (b, chunk(c), 0)),
            pl.BlockSpec((bb, L, D_ML), lambda b, c: (b, chunk(c), 1)),
            pl.BlockSpec((bb, D_ML, L), lambda b, c: (b, 0, chunk(c))),
            pl.BlockSpec((bb, None, 2 * G_ROWS, L), lambda b, c: (b, chunk(c), 0, 0)),
            pl.BlockSpec((bb, L, LANES), lambda b, c: (b, chunk(c), 0)),
        ],
        out_specs=pl.BlockSpec((bb, D_ML, L), lambda b, c: (b, 0, chunk(c))),
        out_shape=jax.ShapeDtypeStruct((B, D_ML, S), F32),
        scratch_shapes=[
            pltpu.VMEM((bb, ML_HEADS, ML_HD, ML_HD), F32),
            pltpu.VMEM((bb, 2 * ML_HEADS, ML_HD), F32),
            pltpu.VMEM((bb, 2 * ML_HEADS, LANES), F32),
        ],
        compiler_params=_params("parallel", "arbitrary"),
        name="mlstm_bwd" if reverse else "mlstm_fwd",
    )(qk3, qk3, vt, gr4, uc3)


def _outproj_kernel(x_ref, rf_ref, rb_ref, gate_ref, mf_ref, mb_ref, ot_ref, mg_ref, wr_ref, wm_ref,
                    y_ref):
    y_rg = (rf_ref[...] + rb_ref[...]) * _gelu(gate_ref[...])
    acc = _dot(y_rg.astype(BF16), wr_ref[...])
    h_t = mf_ref[...] + mb_ref[...]
    parts = []
    for hd in range(ML_HEADS):
        hh = h_t[hd * ML_HD:(hd + 1) * ML_HD]
        parts.append(hh * lax.rsqrt(jnp.mean(hh * hh, axis=0, keepdims=True) + EPS))
    mg = jnp.tile(mg_ref[...], (1, h_t.shape[1] // LANES))
    y_t = jax.nn.sigmoid(ot_ref[...]) * (jnp.concatenate(parts, axis=0) * mg)
    acc = acc + _dot(y_t.T.astype(BF16), wm_ref[...])
    y_ref[...] = x_ref[...] + acc


def _outproj(x2, rf, rb, pa, mf_t, mb_t, o_t, mg_tile, w_rg, w_ml, *, S):
    T = x2.shape[0]
    tm = TM_PROJ
    nb = S // tm
    tok = lambda width, col: pl.BlockSpec((tm, width), lambda i: (i, col))
    seq_t = pl.BlockSpec((None, D_ML, tm), lambda i: (i // nb, 0, i % nb))
    full = lambda shape: pl.BlockSpec(shape, lambda i: (0,) * len(shape))
    return pl.pallas_call(
        _outproj_kernel,
        grid=(T // tm,),
        in_specs=[
            tok(D_MODEL, 0), tok(D_RG, 0), tok(D_RG, 0), tok(D_RG, 1),
            seq_t, seq_t, seq_t,
            full((D_ML, LANES)), full((D_RG, D_MODEL)), full((D_ML, D_MODEL)),
        ],
        out_specs=tok(D_MODEL, 0),
        out_shape=jax.ShapeDtypeStruct((T, D_MODEL), F32),
        compiler_params=_params("parallel"),
        name="outproj",
    )(x2, rf, rb, pa, mf_t, mb_t, o_t, mg_tile, w_rg, w_ml)


def _ffn_kernel(x_ref, prev_ref, next_ref, g_ref, wu_ref, cw_ref, cb_ref, wd_ref, fg_ref, y_ref,
                hbuf, uvbuf, acc, *, nblk, tb, final):
    blk = pl.program_id(0) % nblk
    H = BF16_ROWS
    sub = FF_SUB
    nsub = D_FF // sub
    nring = uvbuf.shape[0]

    g = g_ref[...]
    hbuf[0:H, :] = jnp.where(blk == 0, 0.0, _rmsnorm(prev_ref[...], g)).astype(BF16)
    hbuf[H:H + tb, :] = _rmsnorm(x_ref[...], g).astype(BF16)
    hbuf[H + tb:, :] = jnp.where(blk == nblk - 1, 0.0, _rmsnorm(next_ref[...], g)).astype(BF16)
    hb = hbuf[...]

    acts = []
    for sc in range(nsub):
        cols = slice(2 * sub * sc, 2 * sub * (sc + 1))
        slot = sc % nring
        uvbuf[slot] = _dot(hb, wu_ref[:, cols])
        c = cb_ref[:, cols] + uvbuf[slot, pl.ds(H - 1, tb), :] * cw_ref[0:1, cols]
        c = c + uvbuf[slot, pl.ds(H, tb), :] * cw_ref[1:2, cols]
        c = c + uvbuf[slot, pl.ds(H + 1, tb), :] * cw_ref[2:3, cols]
        acts.append((_gelu(c[:, :sub]) * c[:, sub:]).astype(BF16))
        if sc % 2 == 1:
            d = _dot(jnp.concatenate(acts[-2:], axis=1), wd_ref[sub * (sc - 1):sub * (sc + 1), :])
            if sc == 1:
                acc[...] = d
            else:
                acc[...] += d

    y = x_ref[...] + acc[...]
    if final:
        y = _rmsnorm(y, fg_ref[...])
    y_ref[...] = y


def _ffn(x2, g, w_up, cw, cb, w_down, fg, *, S, final):
    T = x2.shape[0]
    tb = TB_FFN
    nblk = S // tb
    hpb = tb // BF16_ROWS
    n_halo = T // BF16_ROWS
    full = lambda shape: pl.BlockSpec(shape, lambda i: (0,) * len(shape))
    kern = functools.partial(_ffn_kernel, nblk=nblk, tb=tb, final=final)
    return pl.pallas_call(
        kern,
        grid=(T // tb,),
        in_specs=[
            pl.BlockSpec((tb, D_MODEL), lambda i: (i, 0)),
            pl.BlockSpec((BF16_ROWS, D_MODEL), lambda i: (jnp.maximum(i * hpb - 1, 0), 0)),
            pl.BlockSpec((BF16_ROWS, D_MODEL), lambda i: (jnp.minimum((i + 1) * hpb, n_halo - 1), 0)),
            full((1, D_MODEL)), full((D_MODEL, 2 * D_FF)), full((3, 2 * D_FF)), full((1, 2 * D_FF)),
            full((D_FF, D_MODEL)), full((1, D_MODEL)),
        ],
        out_specs=pl.BlockSpec((tb, D_MODEL), lambda i: (i, 0)),
        out_shape=jax.ShapeDtypeStruct((T, D_MODEL), F32),
        scratch_shapes=[
            pltpu.VMEM((tb + 2 * BF16_ROWS, D_MODEL), BF16),
            pltpu.VMEM((FF_RING, tb + 2 * BF16_ROWS, 2 * FF_SUB), F32),
            pltpu.VMEM((tb, D_MODEL), F32),
        ],
        compiler_params=pltpu.CompilerParams(dimension_semantics=("parallel",),
                                             vmem_limit_bytes=VMEM_LIMIT_FFN),
        name="convffn",
    )(x2, x2, x2, g, w_up, cw, cb, w_down, fg)


def _pair_columns(w):
    lead = w.shape[:-1]
    n = D_FF // FF_SUB
    return jnp.swapaxes(w.reshape(*lead, 2, n, FF_SUB), -3, -2).reshape(*lead, 2 * D_FF)


def _block_diag(w):
    eye = jnp.eye(RG_BLOCKS, dtype=w.dtype)
    return jnp.einsum('ncd,nm->ncmd', w, eye).reshape(D_RG, D_RG)


def _encoder(x, norm1_g, w_in, b_gates, rg_conv_w, rg_conv_b, rg_wa, rg_ba, rg_wx, rg_bx, rg_lambda,
             ml_norm_g, w_out, norm2_g, w_up, ffn_conv_w, ffn_conv_b, w_down, final_g):
    B, S, _ = x.shape
    T = B * S
    depth = w_in.shape[0]
    x2 = x.reshape(T, D_MODEL)
    row = lambda v: v.reshape(1, -1).astype(F32)
    n_nat = 2 * D_RG + 2 * D_ML
    for l in range(depth):
        w_nat = w_in[l, :, :n_nat].astype(BF16)
        w_tr = w_in[l, :, n_nat:].T.astype(BF16)
        bias = jnp.broadcast_to(b_gates[l].astype(F32).reshape(N_GATE, 1), (N_GATE, ML_CHUNK))
        pa, qk, v_t, o_t, gr, uc = _inproj(x2, row(norm1_g[l]), w_nat, w_tr, bias, B=B, S=S)
        qk3 = qk.reshape(B, S, 2 * D_ML)
        gr4 = gr.reshape(B, S // ML_CHUNK, 2 * G_ROWS, ML_CHUNK)
        uc3 = uc.reshape(B, S, LANES)
        r_dir, m_dir = [], []
        for d, reverse in enumerate((False, True)):
            r_dir.append(_rglru(
                pa, rg_conv_w[l].astype(F32), row(rg_conv_b[l]),
                _block_diag(rg_wa[l, d]).astype(BF16), _block_diag(rg_wx[l, d]).astype(BF16),
                row(rg_ba[l, d]), row(rg_bx[l, d]), row(rg_lambda[l, d]), B=B, S=S, reverse=reverse))
            m_dir.append(_mlstm(qk3, v_t, gr4, uc3, reverse=reverse))
        wo = w_out[l].astype(BF16)
        mg_tile = jnp.broadcast_to(ml_norm_g[l].astype(F32).reshape(D_ML, 1), (D_ML, LANES))
        x2 = _outproj(x2, r_dir[0], r_dir[1], pa, m_dir[0], m_dir[1], o_t, mg_tile,
                      wo[:D_RG], wo[D_RG:], S=S)
        x2 = _ffn(x2, row(norm2_g[l]), _pair_columns(w_up[l]).astype(BF16),
                  _pair_columns(ffn_conv_w[l].astype(F32)), _pair_columns(row(ffn_conv_b[l])),
                  w_down[l].astype(BF16), row(final_g), S=S,
                  final=(l == depth - 1))
    return x2.reshape(B, S, D_MODEL)


def kernel(x_prompt, x_sample, norm1_g, w_in, b_gates, rg_conv_w, rg_conv_b, rg_wa, rg_ba, rg_wx, rg_bx,
           rg_lambda, ml_norm_g, w_out, norm2_g, w_up, ffn_conv_w, ffn_conv_b, w_down, final_g):
    weights = (norm1_g, w_in, b_gates, rg_conv_w, rg_conv_b, rg_wa, rg_ba, rg_wx, rg_bx, rg_lambda,
               ml_norm_g, w_out, norm2_g, w_up, ffn_conv_w, ffn_conv_b, w_down, final_g)
    return (_encoder(x_prompt, *weights), _encoder(x_sample, *weights))
```

```python
import functools

import jax
import jax.numpy as jnp
from jax import lax
from jax.experimental import pallas as pl
from jax.experimental.pallas import tpu as pltpu

F32 = jnp.float32
BF16 = jnp.bfloat16

D_MODEL = 1024
D_RG = 512
D_ML = 512
RG_BLOCKS = 8
RG_C = 8.0
ML_HEADS = 4
ML_HD = 128
ML_CHUNK = 128
D_FF = 3072
EPS = 1e-6
N_GATE = 4 * ML_HEADS

SUBLANES = 8
LANES = 128
BF16_ROWS = 16
VMEM_LIMIT = 48 * 1024 * 1024

TM_PROJ = 512
TB_RG = 512
ML_SEQS = 4
TB_FFN = 512
FF_SUB = 256
FF_RING = 4
VMEM_LIMIT_FFN = 56 * 1024 * 1024

G_B, G_U, G_A, G_BL, G_GM = 0, 8, 16, 24, 32
G_ROWS = 40
UC_DIR = 24
UC_ONES = 2 * UC_DIR


def _params(*sem):
    return pltpu.CompilerParams(dimension_semantics=sem, vmem_limit_bytes=VMEM_LIMIT)


def _softplus(z):
    return jnp.maximum(z, 0.0) + jnp.log1p(jnp.exp(-jnp.abs(z)))


def _gelu(x):
    return 0.5 * x * (1.0 + jnp.tanh(0.7978845608028654 * (x + 0.044715 * (x * x * x))))


def _rmsnorm(x, g):
    return x * lax.rsqrt(jnp.mean(x * x, axis=-1, keepdims=True) + EPS) * g


def _dot(a, b):
    return jnp.dot(a, b, preferred_element_type=F32)


def _dot_nt(a, b):
    return lax.dot_general(a, b, (((1,), (1,)), ((), ())), preferred_element_type=F32)


def _split3(x):
    hi = x.astype(BF16).astype(F32)
    r1 = x - hi
    mid = r1.astype(BF16).astype(F32)
    return hi, mid, (r1 - mid).astype(BF16).astype(F32)


def _lane_scan(x, op, fill, reverse):
    n = x.shape[-1]
    lane = lax.broadcasted_iota(jnp.int32, x.shape, 1)
    s = 1
    while s < n:
        if reverse:
            x = op(x, jnp.where(lane < n - s, pltpu.roll(x, n - s, 1), fill))
        else:
            x = op(x, jnp.where(lane >= s, pltpu.roll(x, s, 1), fill))
        s *= 2
    return x


def _inproj_kernel(x_ref, g_ref, wn_ref, wt_ref, bias_ref, pa_ref, qk_ref, vt_ref, ot_ref, gr_ref,
                   uc_ref):
    L = ML_CHUNK
    NH = ML_HEADS
    h = _rmsnorm(x_ref[...], g_ref[...]).astype(BF16)
    nat = _dot(h, wn_ref[...])
    pa_ref[...] = nat[:, :2 * D_RG]
    qk_ref[...] = jnp.concatenate(
        [nat[:, 2 * D_RG:2 * D_RG + D_ML] * (ML_HD ** -0.5), nat[:, 2 * D_RG + D_ML:]], axis=-1).astype(BF16)
    tr = _dot_nt(wt_ref[...], h)
    vt_ref[...] = tr[:D_ML].astype(BF16)
    ot_ref[...] = tr[D_ML:2 * D_ML]

    rowid = lax.broadcasted_iota(jnp.int32, (2 * NH, L), 0)
    head_row = rowid < NH
    rep = lambda col: jnp.broadcast_to(col, (2 * NH, L))
    zeros8 = jnp.zeros((2 * NH, L), F32)
    for c in range(gr_ref.shape[0]):
        g16 = tr[2 * D_ML:, c * L:(c + 1) * L] + bias_ref[...]
        tiles = []
        for d, reverse in enumerate((False, True)):
            gates = g16[2 * NH * d:2 * NH * (d + 1)]
            lf = jnp.where(head_row, 0.0, -_softplus(-gates))
            bcum = pltpu.roll(_lane_scan(lf, jnp.add, 0.0, reverse), NH, 0)
            u = jnp.where(head_row, gates - bcum, 0.0)
            last = 0 if reverse else L - 1
            bl = rep(bcum[:, last:last + 1])
            base = G_ROWS * d
            gr_ref[c, base + G_B:base + G_B + 8, :] = bcum
            gr_ref[c, base + G_U:base + G_U + 8, :] = u
            gr_ref[c, base + G_A:base + G_A + 8, :] = bcum + _lane_scan(u, jnp.maximum, -jnp.inf, reverse)
            gr_ref[c, base + G_BL:base + G_BL + 8, :] = bl
            gr_ref[c, base + G_GM:base + G_GM + 8, :] = rep(jnp.max(bl + u, axis=-1, keepdims=True))
            tiles.extend(_split3(u))
        tiles.append(jnp.ones((2 * NH, L), F32))
        tiles.extend([zeros8] * (L // 8 - len(tiles)))
        uc_ref[c * L:(c + 1) * L, :] = jnp.concatenate(tiles, axis=0).T.astype(BF16)


def _inproj(x2, g, w_nat, w_tr, bias, *, B, S):
    T = B * S
    tm = TM_PROJ
    nb = S // tm
    nat_w = w_nat.shape[1]
    full = lambda shape: pl.BlockSpec(shape, lambda i: (0,) * len(shape))
    seq_t = pl.BlockSpec((None, D_ML, tm), lambda i: (i // nb, 0, i % nb))
    return pl.pallas_call(
        _inproj_kernel,
        grid=(T // tm,),
        in_specs=[
            pl.BlockSpec((tm, D_MODEL), lambda i: (i, 0)),
            full((1, D_MODEL)), full((D_MODEL, nat_w)), full(w_tr.shape), full((N_GATE, ML_CHUNK)),
        ],
        out_specs=[
            pl.BlockSpec((tm, 2 * D_RG), lambda i: (i, 0)),
            pl.BlockSpec((tm, 2 * D_ML), lambda i: (i, 0)),
            seq_t, seq_t,
            pl.BlockSpec((tm // ML_CHUNK, 2 * G_ROWS, ML_CHUNK), lambda i: (i, 0, 0)),
            pl.BlockSpec((tm, LANES), lambda i: (i, 0)),
        ],
        out_shape=[
            jax.ShapeDtypeStruct((T, 2 * D_RG), F32),
            jax.ShapeDtypeStruct((T, 2 * D_ML), BF16),
            jax.ShapeDtypeStruct((B, D_ML, S), BF16),
            jax.ShapeDtypeStruct((B, D_ML, S), F32),
            jax.ShapeDtypeStruct((T // ML_CHUNK, 2 * G_ROWS, ML_CHUNK), F32),
            jax.ShapeDtypeStruct((T, LANES), BF16),
        ],
        compiler_params=_params("parallel"),
        name="inproj",
    )(x2, g, w_nat, w_tr, bias)


def _rglru_kernel(x_ref, prev_ref, next_ref, cw_ref, cb_ref, wa_ref, wx_ref, ba_ref, bx_ref,
                  lam_ref, h_ref, slabs, xbuf, pbuf, lbuf, carry, *, reverse, nblk, tb):
    step = pl.program_id(1)
    blk = (nblk - 1 - step) if reverse else step
    nslab = D_RG // LANES
    seg = tb // SUBLANES
    pitch = seg + SUBLANES
    X0 = 2 * SUBLANES

    @pl.when(step == 0)
    def _():
        carry[...] = jnp.zeros_like(carry)

    x = x_ref[...]
    for k in range(nslab):
        for s in range(SUBLANES):
            slabs[k, pitch * s:pitch * s + seg, :] = x[seg * s:seg * (s + 1), LANES * k:LANES * (k + 1)]
    seam = {}
    for j in range(seg):
        rows = jnp.concatenate(
            [slabs[k, pl.ds(j, SUBLANES, stride=pitch), :] for k in range(nslab)], axis=1)
        xbuf[X0 + SUBLANES * j:X0 + SUBLANES * (j + 1), :] = rows
        if j in (0, seg - 2, seg - 1):
            seam[j] = rows
    row = lax.broadcasted_iota(jnp.int32, (SUBLANES, D_RG), 0)
    tile_row = lambda v, i: jnp.broadcast_to(v[i:i + 1, :], (SUBLANES, D_RG))
    prev = jnp.where(blk == 0, 0.0, prev_ref[...])
    nxt = jnp.where(blk == nblk - 1, 0.0, next_ref[...])
    xbuf[0:SUBLANES, :] = jnp.where(row == 0, tile_row(prev, SUBLANES - 2), pltpu.roll(seam[seg - 2], 1, 0))
    xbuf[SUBLANES:X0, :] = jnp.where(row == 0, tile_row(prev, SUBLANES - 1), pltpu.roll(seam[seg - 1], 1, 0))
    xbuf[X0 + tb:, :] = jnp.where(row == SUBLANES - 1, tile_row(nxt, 0),
                                  pltpu.roll(seam[0], SUBLANES - 1, 0))
    xc = cb_ref[...] + xbuf[0:tb, :] * cw_ref[0:1, :]
    xc = xc + xbuf[SUBLANES:SUBLANES + tb, :] * cw_ref[1:2, :]
    xc = xc + xbuf[X0:X0 + tb, :] * cw_ref[2:3, :]
    xc = xc + xbuf[X0 + SUBLANES:X0 + SUBLANES + tb, :] * cw_ref[3:4, :]

    xcb = xc.astype(BF16)
    r = jax.nn.sigmoid(_dot(xcb, wa_ref[...]) + ba_ref[...])
    i = jax.nn.sigmoid(_dot(xcb, wx_ref[...]) + bx_ref[...])
    decay_rate = RG_C * _softplus(-lam_ref[...])
    a = jnp.exp2((decay_rate * -1.4426950408889634) * r)
    y = jnp.tanh(decay_rate * r) * (a * a + 1.0)
    u = jnp.where(y > 0.0, y * lax.rsqrt(y), 0.0) * (i * xc)

    order = range(seg - 1, -1, -1) if reverse else range(seg)
    P = L = None
    for j in order:
        rs = slice(SUBLANES * j, SUBLANES * (j + 1))
        if P is None:
            P, L = a[rs], u[rs]
        else:
            P, L = a[rs] * P, a[rs] * L + u[rs]
        pbuf[rs, :] = P
        lbuf[rs, :] = L

    A, U = P, L
    for s in (1, 2, 4):
        if reverse:
            keep = row < SUBLANES - s
            shift = SUBLANES - s
        else:
            keep = row >= s
            shift = s
        a_sh = jnp.where(keep, pltpu.roll(A, shift, 0), 1.0)
        u_sh = jnp.where(keep, pltpu.roll(U, shift, 0), 0.0)
        U = A * u_sh + U
        A = A * a_sh
    c_in = carry[...]
    e = U + A * c_in
    if reverse:
        c_seg = jnp.where(row == SUBLANES - 1, c_in, pltpu.roll(e, SUBLANES - 1, 0))
        carry[...] = tile_row(e, 0)
    else:
        c_seg = jnp.where(row == 0, c_in, pltpu.roll(e, 1, 0))
        carry[...] = tile_row(e, SUBLANES - 1)

    for j in range(seg):
        rs = slice(SUBLANES * j, SUBLANES * (j + 1))
        hj = lbuf[rs, :] + pbuf[rs, :] * c_seg
        for k in range(nslab):
            slabs[k, pl.ds(j, SUBLANES, stride=pitch), :] = hj[:, LANES * k:LANES * (k + 1)]
    for k in range(nslab):
        for s in range(SUBLANES):
            h_ref[seg * s:seg * (s + 1), LANES * k:LANES * (k + 1)] = slabs[k, pitch * s:pitch * s + seg, :]


def _rglru(p, cw, cb, wa, wx, ba, bx, lam, *, B, S, reverse):
    T = B * S
    tb = TB_RG
    nblk = S // tb
    hb = tb // SUBLANES
    n_halo = T // SUBLANES

    def blk_of(j):
        return (nblk - 1 - j) if reverse else j

    def main_map(b, j):
        return (b * nblk + blk_of(j), 0)

    def prev_map(b, j):
        return (jnp.maximum((b * nblk + blk_of(j)) * hb - 1, 0), 0)

    def next_map(b, j):
        return (jnp.minimum((b * nblk + blk_of(j) + 1) * hb, n_halo - 1), 0)

    full = lambda shape: pl.BlockSpec(shape, lambda b, j: (0,) * len(shape))
    kern = functools.partial(_rglru_kernel, reverse=reverse, nblk=nblk, tb=tb)
    return pl.pallas_call(
        kern,
        grid=(B, nblk),
        in_specs=[
            pl.BlockSpec((tb, D_RG), main_map),
            pl.BlockSpec((SUBLANES, D_RG), prev_map),
            pl.BlockSpec((SUBLANES, D_RG), next_map),
            full((4, D_RG)), full((1, D_RG)),
            full((D_RG, D_RG)), full((D_RG, D_RG)),
            full((1, D_RG)), full((1, D_RG)), full((1, D_RG)),
        ],
        out_specs=pl.BlockSpec((tb, D_RG), main_map),
        out_shape=jax.ShapeDtypeStruct((T, D_RG), F32),
        scratch_shapes=[
            pltpu.VMEM((D_RG // LANES, tb + SUBLANES * SUBLANES, LANES), F32),
            pltpu.VMEM((tb + 3 * SUBLANES, D_RG), F32),
            pltpu.VMEM((tb, D_RG), F32),
            pltpu.VMEM((tb, D_RG), F32),
            pltpu.VMEM((SUBLANES, D_RG), F32),
        ],
        compiler_params=_params("parallel", "arbitrary"),
        name="rglru_bwd" if reverse else "rglru_fwd",
    )(p, p, p, cw, cb, wa, wx, ba, bx, lam)


def _mlstm_kernel(q_ref, k_ref, vt_ref, gr_ref, uc_ref, h_ref, ct_st, n_st, m_st, *, reverse, bb):
    L = ML_CHUNK
    NH = ML_HEADS
    d = 1 if reverse else 0

    @pl.when(pl.program_id(1) == 0)
    def _():
        ct_st[...] = jnp.zeros_like(ct_st)
        n_st[...] = jnp.zeros_like(n_st)
        m_st[...] = jnp.zeros_like(m_st)

    row8 = lax.broadcasted_iota(jnp.int32, (SUBLANES, L), 0)
    s_id = lax.broadcasted_iota(jnp.int32, (L, L), 0)
    t_id = lax.broadcasted_iota(jnp.int32, (L, L), 1)
    valid = (s_id >= t_id) if reverse else (s_id <= t_id)
    zeros8 = jnp.zeros((SUBLANES, L), F32)
    tile = lambda rows, hd: jnp.broadcast_to(rows[hd:hd + 1, :], (L, L))
    row_of = lambda rows, hd: jnp.broadcast_to(rows[hd:hd + 1, :], (SUBLANES, L))
    pairs = [(b, hd) for b in range(bb) for hd in range(NH)]

    seq = []
    for b in range(bb):
        base = G_ROWS * d
        bcum = gr_ref[b, base + G_B:base + G_B + 8, :]
        u = gr_ref[b, base + G_U:base + G_U + 8, :]
        b_last = gr_ref[b, base + G_BL:base + G_BL + 8, :]
        m_prev = m_st[b]
        inter = bcum + m_prev
        m_t = jnp.maximum(inter, gr_ref[b, base + G_A:base + G_A + 8, :])
        m_new = jnp.maximum(b_last + m_prev, gr_ref[b, base + G_GM:base + G_GM + 8, :])
        n_prev = n_st[b]
        wg = jnp.exp(u + (b_last - m_new))
        seq.append(dict(
            w_int=jnp.exp(inter - m_t), e_neg=jnp.exp(-m_t), v3=_split3(bcum - m_t),
            decay=jnp.exp(b_last + m_prev - m_new), m_new=m_new, n_prev=n_prev, wg=wg,
            n_lhs=jnp.concatenate([n_prev, zeros8], axis=0).astype(BF16),
            wg_lhs=jnp.concatenate([wg, zeros8], axis=0).astype(BF16)))

    st, expo = {}, {}
    for b, hd in pairs:
        cs = slice(hd * ML_HD, (hd + 1) * ML_HD)
        st[b, hd] = _dot_nt(jnp.concatenate([k_ref[b, :, cs], seq[b]["n_lhs"]], axis=0), q_ref[b, :, cs])
        onehot = jnp.where(row8 == hd, 1.0, 0.0)
        v_hi, v_mid, v_lo = (row_of(x, hd) for x in seq[b]["v3"])
        v_rows = jnp.where(row8 == 0, v_hi, jnp.where(row8 == 1, v_mid, jnp.where(row8 == 2, v_lo, 0.0)))
        slabs = [zeros8] * (L // SUBLANES)
        for j in range(3):
            slabs[(UC_DIR * d) // SUBLANES + j] = onehot
        slabs[UC_ONES // SUBLANES] = v_rows
        expo[b, hd] = _dot(uc_ref[b], jnp.concatenate(slabs, axis=0).astype(BF16))

    for b, hd in pairs:
        cs = slice(hd * ML_HD, (hd + 1) * ML_HD)
        sq = seq[b]
        s_t = st[b, hd][0:L] * jnp.exp(jnp.where(valid, expo[b, hd], -jnp.inf))
        w_h = sq["w_int"][hd:hd + 1, :]
        den = jnp.sum(s_t, axis=0, keepdims=True) + w_h * st[b, hd][L + hd:L + hd + 1, :]
        r = 1.0 / jnp.maximum(jnp.abs(den), sq["e_neg"][hd:hd + 1, :])
        h_ref[b, cs, :] = (_dot(vt_ref[b, cs, :], (s_t * r).astype(BF16))
                           + _dot_nt(ct_st[b, hd].astype(BF16), q_ref[b, :, cs]) * (r * w_h))

    for b, hd in pairs:
        cs = slice(hd * ML_HD, (hd + 1) * ML_HD)
        sq = seq[b]
        vw = (vt_ref[b, cs, :].astype(F32) * sq["wg"][hd:hd + 1, :]).astype(BF16)
        upd = _dot(jnp.concatenate([vw, sq["wg_lhs"]], axis=0), k_ref[b, :, cs])
        ct_st[b, hd] = tile(sq["decay"], hd) * ct_st[b, hd] + upd[0:L]
        n_st[b, hd:hd + 1, :] = (sq["decay"][hd:hd + 1, :] * sq["n_prev"][hd:hd + 1, :]
                                 + upd[L + hd:L + hd + 1, :])
    for b in range(bb):
        m_st[b] = seq[b]["m_new"]


def _mlstm(qk3, vt, gr4, uc3, *, reverse):
    B, S, _ = qk3.shape
    L = ML_CHUNK
    nc = S // L
    bb = ML_SEQS

    def chunk(c):
        return (nc - 1 - c) if reverse else c

    kern = functools.partial(_mlstm_kernel, reverse=reverse, bb=bb)
    return pl.pallas_call(
        kern,
        grid=(B // bb, nc),
        in_specs=[
            pl.BlockSpec((bb, L, D_ML), lambda b, c: (b, chunk(c), 0)),
            pl.BlockSpec((bb, L, D_ML), lambda b, c: (b, chunk(c), 1)),
            pl.BlockSpec((bb, D_ML, L), lambda b, c: (b, 0, chunk(c))),
            pl.BlockSpec((bb, None, 2 * G_ROWS, L), lambda b, c: (b, chunk(c), 0, 0)),
            pl.BlockSpec((bb, L, LANES), lambda b, c: (b, chunk(c), 0)),
        ],
        out_specs=pl.BlockSpec((bb, D_ML, L), lambda b, c: (b, 0, chunk(c))),
        out_shape=jax.ShapeDtypeStruct((B, D_ML, S), F32),
        scratch_shapes=[
            pltpu.VMEM((bb, ML_HEADS, ML_HD, ML_HD), F32),
            pltpu.VMEM((bb, 2 * ML_HEADS, ML_HD), F32),
            pltpu.VMEM((bb, 2 * ML_HEADS, LANES), F32),
        ],
        compiler_params=_params("parallel", "arbitrary"),
        name="mlstm_bwd" if reverse else "mlstm_fwd",
    )(qk3, qk3, vt, gr4, uc3)


def _outproj_kernel(x_ref, rf_ref, rb_ref, gate_ref, mf_ref, mb_ref, ot_ref, mg_ref, wr_ref, wm_ref,
                    y_ref):
    y_rg = (rf_ref[...] + rb_ref[...]) * _gelu(gate_ref[...])
    acc = _dot(y_rg.astype(BF16), wr_ref[...])
    h_t = mf_ref[...] + mb_ref[...]
    parts = []
    for hd in range(ML_HEADS):
        hh = h_t[hd * ML_HD:(hd + 1) * ML_HD]
        parts.append(hh * lax.rsqrt(jnp.mean(hh * hh, axis=0, keepdims=True) + EPS))
    mg = jnp.tile(mg_ref[...], (1, h_t.shape[1] // LANES))
    y_t = jax.nn.sigmoid(ot_ref[...]) * (jnp.concatenate(parts, axis=0) * mg)
    acc = acc + _dot(y_t.T.astype(BF16), wm_ref[...])
    y_ref[...] = x_ref[...] + acc


def _outproj(x2, rf, rb, pa, mf_t, mb_t, o_t, mg_tile, w_rg, w_ml, *, S):
    T = x2.shape[0]
    tm = TM_PROJ
    nb = S // tm
    tok = lambda width, col: pl.BlockSpec((tm, width), lambda i: (i, col))
    seq_t = pl.BlockSpec((None, D_ML, tm), lambda i: (i // nb, 0, i % nb))
    full = lambda shape: pl.BlockSpec(shape, lambda i: (0,) * len(shape))
    return pl.pallas_call(
        _outproj_kernel,
        grid=(T // tm,),
        in_specs=[
            tok(D_MODEL, 0), tok(D_RG, 0), tok(D_RG, 0), tok(D_RG, 1),
            seq_t, seq_t, seq_t,
            full((D_ML, LANES)), full((D_RG, D_MODEL)), full((D_ML, D_MODEL)),
        ],
        out_specs=tok(D_MODEL, 0),
        out_shape=jax.ShapeDtypeStruct((T, D_MODEL), F32),
        compiler_params=_params("parallel"),
        name="outproj",
    )(x2, rf, rb, pa, mf_t, mb_t, o_t, mg_tile, w_rg, w_ml)


def _gelu_gate(gate, val):
    k0 = -2.0 * 0.7978845608028654 * 1.4426950408889634
    z = gate * (k0 + (k0 * 0.044715) * (gate * gate))
    return (gate * val) / (1.0 + jnp.exp2(z))


def _ffn_kernel(x_ref, prev_ref, next_ref, g_ref, wu_ref, cw_ref, cb_ref, wd_ref, fg_ref, y_ref,
                slabs, hbuf, uvbuf, acc, *, nblk, tb, final):
    blk = pl.program_id(0) % nblk
    sub = FF_SUB
    nsub = D_FF // sub
    nring = uvbuf.shape[0]
    nslab = D_MODEL // LANES
    seg = tb // SUBLANES
    pitch = seg + SUBLANES

    g = g_ref[...]
    hn = _rmsnorm(x_ref[...], g)
    for k in range(nslab):
        for s in range(SUBLANES):
            slabs[k, pitch * s:pitch * s + seg, :] = hn[seg * s:seg * (s + 1), LANES * k:LANES * (k + 1)]

    def perm_rows(j):
        return jnp.concatenate(
            [slabs[k, pl.ds(j, SUBLANES, stride=pitch), :] for k in range(nslab)], axis=1)

    for jj in range(seg // 2):
        hbuf[BF16_ROWS * jj:BF16_ROWS * (jj + 1), :] = jnp.concatenate(
            [perm_rows(2 * jj), perm_rows(2 * jj + 1)], axis=0).astype(BF16)
    row_x = lax.broadcasted_iota(jnp.int32, (SUBLANES, D_MODEL), 0)
    h_prev = jnp.where(blk == 0, 0.0, pltpu.roll(_rmsnorm(prev_ref[...], g), 1, 0))
    h_next = jnp.where(blk == nblk - 1, 0.0, pltpu.roll(_rmsnorm(next_ref[...], g), 1, 0))
    halo = jnp.where(row_x == 0, h_prev, jnp.where(row_x == 1, h_next, 0.0))
    hbuf[tb:, :] = jnp.concatenate([halo, jnp.zeros_like(halo)], axis=0).astype(BF16)

    row_u = lax.broadcasted_iota(jnp.int32, (SUBLANES, 2 * sub), 0)

    def up(sc):
        cols = slice(2 * sub * sc, 2 * sub * (sc + 1))
        slot = sc % nring
        res = _dot(hbuf[...], wu_ref[:, cols])
        uvbuf[slot, SUBLANES:SUBLANES + tb, :] = res[0:tb]
        uvbuf[slot, 0:SUBLANES, :] = jnp.where(
            row_u == 0, jnp.broadcast_to(res[tb:tb + 1], row_u.shape),
            pltpu.roll(res[tb - SUBLANES:tb], 1, 0))
        uvbuf[slot, SUBLANES + tb:, :] = jnp.where(
            row_u == SUBLANES - 1, jnp.broadcast_to(res[tb + 1:tb + 2], row_u.shape),
            pltpu.roll(res[0:SUBLANES], SUBLANES - 1, 0))

    def gate(sc):
        cols = slice(2 * sub * sc, 2 * sub * (sc + 1))
        slot = sc % nring
        c = cb_ref[:, cols] + uvbuf[slot, 0:tb, :] * cw_ref[0:1, cols]
        c = c + uvbuf[slot, SUBLANES:SUBLANES + tb, :] * cw_ref[1:2, cols]
        c = c + uvbuf[slot, 2 * SUBLANES:2 * SUBLANES + tb, :] * cw_ref[2:3, cols]
        return _gelu_gate(c[:, :sub], c[:, sub:]).astype(BF16)

    up(0)
    up(1)
    for p in range(nsub // 2):
        if 2 * p + 2 < nsub:
            up(2 * p + 2)
            up(2 * p + 3)
        d = _dot(jnp.concatenate([gate(2 * p), gate(2 * p + 1)], axis=1),
                 wd_ref[2 * sub * p:2 * sub * (p + 1), :])
        if p == 0:
            acc[...] = d
        else:
            acc[...] += d

    for j in range(seg):
        for k in range(nslab):
            slabs[k, pl.ds(j, SUBLANES, stride=pitch), :] = acc[SUBLANES * j:SUBLANES * (j + 1),
                                                                LANES * k:LANES * (k + 1)]
    ffn = jnp.concatenate(
        [jnp.concatenate([slabs[k, pitch * s:pitch * s + seg, :] for s in range(SUBLANES)], axis=0)
         for k in range(nslab)], axis=1)
    y = x_ref[...] + ffn
    if final:
        y = _rmsnorm(y, fg_ref[...])
    y_ref[...] = y


def _ffn(x2, g, w_up, cw, cb, w_down, fg, *, S, final):
    T = x2.shape[0]
    tb = TB_FFN
    nblk = S // tb
    hpb = tb // SUBLANES
    n_halo = T // SUBLANES
    full = lambda shape: pl.BlockSpec(shape, lambda i: (0,) * len(shape))
    kern = functools.partial(_ffn_kernel, nblk=nblk, tb=tb, final=final)
    return pl.pallas_call(
        kern,
        grid=(T // tb,),
        in_specs=[
            pl.BlockSpec((tb, D_MODEL), lambda i: (i, 0)),
            pl.BlockSpec((SUBLANES, D_MODEL), lambda i: (jnp.maximum(i * hpb - 1, 0), 0)),
            pl.BlockSpec((SUBLANES, D_MODEL), lambda i: (jnp.minimum((i + 1) * hpb, n_halo - 1), 0)),
            full((1, D_MODEL)), full((D_MODEL, 2 * D_FF)), full((3, 2 * D_FF)), full((1, 2 * D_FF)),
            full((D_FF, D_MODEL)), full((1, D_MODEL)),
        ],
        out_specs=pl.BlockSpec((tb, D_MODEL), lambda i: (i, 0)),
        out_shape=jax.ShapeDtypeStruct((T, D_MODEL), F32),
        scratch_shapes=[
            pltpu.VMEM((D_MODEL // LANES, tb + SUBLANES * SUBLANES, LANES), F32),
            pltpu.VMEM((tb + BF16_ROWS, D_MODEL), BF16),
            pltpu.VMEM((FF_RING, tb + 2 * SUBLANES, 2 * FF_SUB), F32),
            pltpu.VMEM((tb, D_MODEL), F32),
        ],
        compiler_params=pltpu.CompilerParams(dimension_semantics=("parallel",),
                                             vmem_limit_bytes=VMEM_LIMIT_FFN),
        name="convffn",
    )(x2, x2, x2, g, w_up, cw, cb, w_down, fg)


def _pair_columns(w):
    lead = w.shape[:-1]
    n = D_FF // FF_SUB
    return jnp.swapaxes(w.reshape(*lead, 2, n, FF_SUB), -3, -2).reshape(*lead, 2 * D_FF)


def _block_diag(w):
    eye = jnp.eye(RG_BLOCKS, dtype=w.dtype)
    return jnp.einsum('ncd,nm->ncmd', w, eye).reshape(D_RG, D_RG)


def _encoder(x, norm1_g, w_in, b_gates, rg_conv_w, rg_conv_b, rg_wa, rg_ba, rg_wx, rg_bx, rg_lambda,
             ml_norm_g, w_out, norm2_g, w_up, ffn_conv_w, ffn_conv_b, w_down, final_g):
    B, S, _ = x.shape
    T = B * S
    depth = w_in.shape[0]
    x2 = x.reshape(T, D_MODEL)
    row = lambda v: v.reshape(1, -1).astype(F32)
    n_nat = 2 * D_RG + 2 * D_ML
    for l in range(depth):
        w_nat = w_in[l, :, :n_nat].astype(BF16)
        w_tr = w_in[l, :, n_nat:].T.astype(BF16)
        bias = jnp.broadcast_to(b_gates[l].astype(F32).reshape(N_GATE, 1), (N_GATE, ML_CHUNK))
        pa, qk, v_t, o_t, gr, uc = _inproj(x2, row(norm1_g[l]), w_nat, w_tr, bias, B=B, S=S)
        qk3 = qk.reshape(B, S, 2 * D_ML)
        gr4 = gr.reshape(B, S // ML_CHUNK, 2 * G_ROWS, ML_CHUNK)
        uc3 = uc.reshape(B, S, LANES)
        r_dir, m_dir = [], []
        for d, reverse in enumerate((False, True)):
            r_dir.append(_rglru(
                pa, rg_conv_w[l].astype(F32), row(rg_conv_b[l]),
                _block_diag(rg_wa[l, d]).astype(BF16), _block_diag(rg_wx[l, d]).astype(BF16),
                row(rg_ba[l, d]), row(rg_bx[l, d]), row(rg_lambda[l, d]), B=B, S=S, reverse=reverse))
            m_dir.append(_mlstm(qk3, v_t, gr4, uc3, reverse=reverse))
        wo = w_out[l].astype(BF16)
        mg_tile = jnp.broadcast_to(ml_norm_g[l].astype(F32).reshape(D_ML, 1), (D_ML, LANES))
        x2 = _outproj(x2, r_dir[0], r_dir[1], pa, m_dir[0], m_dir[1], o_t, mg_tile,
                      wo[:D_RG], wo[D_RG:], S=S)
        x2 = _ffn(x2, row(norm2_g[l]), _pair_columns(w_up[l]).astype(BF16),
                  _pair_columns(ffn_conv_w[l].astype(F32)), _pair_columns(row(ffn_conv_b[l])),
                  w_down[l].astype(BF16), row(final_g), S=S,
                  final=(l == depth - 1))
    return x2.reshape(B, S, D_MODEL)


def kernel(x_prompt, x_sample, norm1_g, w_in, b_gates, rg_conv_w, rg_conv_b, rg_wa, rg_ba, rg_wx, rg_bx,
           rg_lambda, ml_norm_g, w_out, norm2_g, w_up, ffn_conv_w, ffn_conv_b, w_down, final_g):
    weights = (norm1_g, w_in, b_gates, rg_conv_w, rg_conv_b, rg_wa, rg_ba, rg_wx, rg_bx, rg_lambda,
               ml_norm_g, w_out, norm2_g, w_up, ffn_conv_w, ffn_conv_b, w_down, final_g)
    return (_encoder(x_prompt, *weights), _encoder(x_sample, *weights))
```

```python
import functools

import jax
import jax.numpy as jnp
from jax import lax
from jax.experimental import pallas as pl
from jax.experimental.pallas import tpu as pltpu

F32 = jnp.float32
BF16 = jnp.bfloat16

D_MODEL = 1024
D_RG = 512
D_ML = 512
RG_BLOCKS = 8
RG_C = 8.0
ML_HEADS = 4
ML_HD = 128
ML_CHUNK = 128
D_FF = 3072
EPS = 1e-6
N_GATE = 4 * ML_HEADS

SUBLANES = 8
LANES = 128
BF16_ROWS = 16
VMEM_LIMIT = 48 * 1024 * 1024

TM_PROJ = 512
TB_RG = 512
ML_SEQS = 8
TB_FFN = 512
FF_SUB = 256
FF_RING = 4
VMEM_LIMIT_FFN = 56 * 1024 * 1024

G_B, G_U, G_A, G_BL, G_GM = 0, 8, 16, 24, 32
G_ROWS = 40
UC_DIR = 24
UC_ONES = 2 * UC_DIR


def _params(*sem):
    return pltpu.CompilerParams(dimension_semantics=sem, vmem_limit_bytes=VMEM_LIMIT)


def _softplus(z):
    return jnp.maximum(z, 0.0) + jnp.log1p(jnp.exp(-jnp.abs(z)))


def _gelu(x):
    return 0.5 * x * (1.0 + jnp.tanh(0.7978845608028654 * (x + 0.044715 * (x * x * x))))


def _rmsnorm(x, g):
    return x * lax.rsqrt(jnp.mean(x * x, axis=-1, keepdims=True) + EPS) * g


def _dot(a, b):
    return jnp.dot(a, b, preferred_element_type=F32)


def _dot_nt(a, b):
    return lax.dot_general(a, b, (((1,), (1,)), ((), ())), preferred_element_type=F32)


def _split3(x):
    hi = x.astype(BF16).astype(F32)
    r1 = x - hi
    mid = r1.astype(BF16).astype(F32)
    return hi, mid, (r1 - mid).astype(BF16).astype(F32)


def _lane_scan(x, op, fill, reverse):
    n = x.shape[-1]
    lane = lax.broadcasted_iota(jnp.int32, x.shape, 1)
    s = 1
    while s < n:
        if reverse:
            x = op(x, jnp.where(lane < n - s, pltpu.roll(x, n - s, 1), fill))
        else:
            x = op(x, jnp.where(lane >= s, pltpu.roll(x, s, 1), fill))
        s *= 2
    return x


def _inproj_kernel(x_ref, g_ref, wn_ref, wt_ref, wg_ref, bias_ref, rx_ref, rg_ref, qk_ref, vt_ref,
                   ot_ref, gr_ref, uc_ref):
    L = ML_CHUNK
    NH = ML_HEADS
    h = _rmsnorm(x_ref[...], g_ref[...]).astype(BF16)

    gt = _dot_nt(wg_ref[...], h)
    rowid = lax.broadcasted_iota(jnp.int32, (2 * NH, L), 0)
    head_row = rowid < NH
    rep = lambda col: jnp.broadcast_to(col, (2 * NH, L))
    zeros8 = jnp.zeros((2 * NH, L), F32)
    for c in range(gr_ref.shape[0]):
        g16 = gt[:, c * L:(c + 1) * L] + bias_ref[...]
        tiles = []
        for d, reverse in enumerate((False, True)):
            gates = g16[2 * NH * d:2 * NH * (d + 1)]
            lf = jnp.where(head_row, 0.0, -_softplus(-gates))
            bcum = pltpu.roll(_lane_scan(lf, jnp.add, 0.0, reverse), NH, 0)
            u = jnp.where(head_row, gates - bcum, 0.0)
            last = 0 if reverse else L - 1
            bl = rep(bcum[:, last:last + 1])
            base = G_ROWS * d
            gr_ref[c, base + G_B:base + G_B + 8, :] = bcum
            gr_ref[c, base + G_U:base + G_U + 8, :] = u
            gr_ref[c, base + G_A:base + G_A + 8, :] = bcum + _lane_scan(u, jnp.maximum, -jnp.inf, reverse)
            gr_ref[c, base + G_BL:base + G_BL + 8, :] = bl
            gr_ref[c, base + G_GM:base + G_GM + 8, :] = rep(jnp.max(bl + u, axis=-1, keepdims=True))
            tiles.extend(_split3(u))
        tiles.append(jnp.ones((2 * NH, L), F32))
        tiles.extend([zeros8] * (L // 8 - len(tiles)))
        uc_ref[c * L:(c + 1) * L, :] = jnp.concatenate(tiles, axis=0).T.astype(BF16)

    nat = _dot(h, wn_ref[...])
    rx_ref[...] = nat[:, :D_RG]
    rg_ref[...] = nat[:, D_RG:2 * D_RG].astype(BF16)
    qk_ref[...] = jnp.concatenate(
        [nat[:, 2 * D_RG:2 * D_RG + D_ML] * (ML_HD ** -0.5), nat[:, 2 * D_RG + D_ML:]], axis=-1).astype(BF16)
    tr = _dot_nt(wt_ref[...], h)
    vt_ref[...] = tr[:D_ML].astype(BF16)
    ot_ref[...] = tr[D_ML:].astype(BF16)


def _inproj(x2, g, w_nat, w_tr, w_gate, bias, *, B, S):
    T = B * S
    tm = TM_PROJ
    nb = S // tm
    full = lambda shape: pl.BlockSpec(shape, lambda i: (0,) * len(shape))
    tok = lambda width: pl.BlockSpec((tm, width), lambda i: (i, 0))
    seq_t = pl.BlockSpec((None, D_ML, tm), lambda i: (i // nb, 0, i % nb))
    return pl.pallas_call(
        _inproj_kernel,
        grid=(T // tm,),
        in_specs=[
            tok(D_MODEL), full((1, D_MODEL)), full(w_nat.shape), full(w_tr.shape), full(w_gate.shape),
            full((N_GATE, ML_CHUNK)),
        ],
        out_specs=[
            tok(D_RG), tok(D_RG), tok(2 * D_ML), seq_t, seq_t,
            pl.BlockSpec((tm // ML_CHUNK, 2 * G_ROWS, ML_CHUNK), lambda i: (i, 0, 0)),
            tok(LANES),
        ],
        out_shape=[
            jax.ShapeDtypeStruct((T, D_RG), F32),
            jax.ShapeDtypeStruct((T, D_RG), BF16),
            jax.ShapeDtypeStruct((T, 2 * D_ML), BF16),
            jax.ShapeDtypeStruct((B, D_ML, S), BF16),
            jax.ShapeDtypeStruct((B, D_ML, S), BF16),
            jax.ShapeDtypeStruct((T // ML_CHUNK, 2 * G_ROWS, ML_CHUNK), F32),
            jax.ShapeDtypeStruct((T, LANES), BF16),
        ],
        compiler_params=_params("parallel"),
        name="inproj",
    )(x2, g, w_nat, w_tr, w_gate, bias)


def _rglru_kernel(x_ref, prev_ref, next_ref, cw_ref, cb_ref, wa_ref, wx_ref, ba_ref, bx_ref,
                  lam_ref, h_ref, slabs, xbuf, pbuf, lbuf, carry, *, reverse, nblk, tb):
    step = pl.program_id(1)
    blk = (nblk - 1 - step) if reverse else step
    nslab = D_RG // LANES
    seg = tb // SUBLANES
    pitch = seg + SUBLANES
    X0 = 2 * SUBLANES

    @pl.when(step == 0)
    def _():
        carry[...] = jnp.zeros_like(carry)

    x = x_ref[...]
    for k in range(nslab):
        for s in range(SUBLANES):
            slabs[k, pitch * s:pitch * s + seg, :] = x[seg * s:seg * (s + 1), LANES * k:LANES * (k + 1)]
    seam = {}
    for j in range(seg):
        rows = jnp.concatenate(
            [slabs[k, pl.ds(j, SUBLANES, stride=pitch), :] for k in range(nslab)], axis=1)
        xbuf[X0 + SUBLANES * j:X0 + SUBLANES * (j + 1), :] = rows
        if j in (0, seg - 2, seg - 1):
            seam[j] = rows
    row = lax.broadcasted_iota(jnp.int32, (SUBLANES, D_RG), 0)
    tile_row = lambda v, i: jnp.broadcast_to(v[i:i + 1, :], (SUBLANES, D_RG))
    prev = jnp.where(blk == 0, 0.0, prev_ref[...])
    nxt = jnp.where(blk == nblk - 1, 0.0, next_ref[...])
    xbuf[0:SUBLANES, :] = jnp.where(row == 0, tile_row(prev, SUBLANES - 2), pltpu.roll(seam[seg - 2], 1, 0))
    xbuf[SUBLANES:X0, :] = jnp.where(row == 0, tile_row(prev, SUBLANES - 1), pltpu.roll(seam[seg - 1], 1, 0))
    xbuf[X0 + tb:, :] = jnp.where(row == SUBLANES - 1, tile_row(nxt, 0),
                                  pltpu.roll(seam[0], SUBLANES - 1, 0))
    xc = cb_ref[...] + xbuf[0:tb, :] * cw_ref[0:1, :]
    xc = xc + xbuf[SUBLANES:SUBLANES + tb, :] * cw_ref[1:2, :]
    xc = xc + xbuf[X0:X0 + tb, :] * cw_ref[2:3, :]
    xc = xc + xbuf[X0 + SUBLANES:X0 + SUBLANES + tb, :] * cw_ref[3:4, :]

    xcb = xc.astype(BF16)
    r = jax.nn.sigmoid(_dot(xcb, wa_ref[...]) + ba_ref[...])
    i = jax.nn.sigmoid(_dot(xcb, wx_ref[...]) + bx_ref[...])
    decay_rate = RG_C * _softplus(-lam_ref[...])
    a = jnp.exp2((decay_rate * -1.4426950408889634) * r)
    y = jnp.tanh(decay_rate * r) * (a * a + 1.0)
    u = jnp.where(y > 0.0, y * lax.rsqrt(y), 0.0) * (i * xc)

    order = range(seg - 1, -1, -1) if reverse else range(seg)
    P = L = None
    for j in order:
        rs = slice(SUBLANES * j, SUBLANES * (j + 1))
        if P is None:
            P, L = a[rs], u[rs]
        else:
            P, L = a[rs] * P, a[rs] * L + u[rs]
        pbuf[rs, :] = P
        lbuf[rs, :] = L

    A, U = P, L
    for s in (1, 2, 4):
        if reverse:
            keep = row < SUBLANES - s
            shift = SUBLANES - s
        else:
            keep = row >= s
            shift = s
        a_sh = jnp.where(keep, pltpu.roll(A, shift, 0), 1.0)
        u_sh = jnp.where(keep, pltpu.roll(U, shift, 0), 0.0)
        U = A * u_sh + U
        A = A * a_sh
    c_in = carry[...]
    e = U + A * c_in
    if reverse:
        c_seg = jnp.where(row == SUBLANES - 1, c_in, pltpu.roll(e, SUBLANES - 1, 0))
        carry[...] = tile_row(e, 0)
    else:
        c_seg = jnp.where(row == 0, c_in, pltpu.roll(e, 1, 0))
        carry[...] = tile_row(e, SUBLANES - 1)

    for j in range(seg):
        rs = slice(SUBLANES * j, SUBLANES * (j + 1))
        hj = lbuf[rs, :] + pbuf[rs, :] * c_seg
        for k in range(nslab):
            slabs[k, pl.ds(j, SUBLANES, stride=pitch), :] = hj[:, LANES * k:LANES * (k + 1)]
    for k in range(nslab):
        for s in range(SUBLANES):
            h_ref[seg * s:seg * (s + 1), LANES * k:LANES * (k + 1)] = (
                slabs[k, pitch * s:pitch * s + seg, :].astype(h_ref.dtype))


def _rglru(p, cw, cb, wa, wx, ba, bx, lam, *, B, S, reverse):
    T = B * S
    tb = TB_RG
    nblk = S // tb
    hb = tb // SUBLANES
    n_halo = T // SUBLANES

    def blk_of(j):
        return (nblk - 1 - j) if reverse else j

    def main_map(b, j):
        return (b * nblk + blk_of(j), 0)

    def prev_map(b, j):
        return (jnp.maximum((b * nblk + blk_of(j)) * hb - 1, 0), 0)

    def next_map(b, j):
        return (jnp.minimum((b * nblk + blk_of(j) + 1) * hb, n_halo - 1), 0)

    full = lambda shape: pl.BlockSpec(shape, lambda b, j: (0,) * len(shape))
    kern = functools.partial(_rglru_kernel, reverse=reverse, nblk=nblk, tb=tb)
    return pl.pallas_call(
        kern,
        grid=(B, nblk),
        in_specs=[
            pl.BlockSpec((tb, D_RG), main_map),
            pl.BlockSpec((SUBLANES, D_RG), prev_map),
            pl.BlockSpec((SUBLANES, D_RG), next_map),
            full((4, D_RG)), full((1, D_RG)),
            full((D_RG, D_RG)), full((D_RG, D_RG)),
            full((1, D_RG)), full((1, D_RG)), full((1, D_RG)),
        ],
        out_specs=pl.BlockSpec((tb, D_RG), main_map),
        out_shape=jax.ShapeDtypeStruct((T, D_RG), BF16),
        scratch_shapes=[
            pltpu.VMEM((D_RG // LANES, tb + SUBLANES * SUBLANES, LANES), F32),
            pltpu.VMEM((tb + 3 * SUBLANES, D_RG), F32),
            pltpu.VMEM((tb, D_RG), F32),
            pltpu.VMEM((tb, D_RG), F32),
            pltpu.VMEM((SUBLANES, D_RG), F32),
        ],
        compiler_params=_params("parallel", "arbitrary"),
        name="rglru_bwd" if reverse else "rglru_fwd",
    )(p, p, p, cw, cb, wa, wx, ba, bx, lam)


def _mlstm_kernel(q_ref, k_ref, vt_ref, gr_ref, uc_ref, h_ref, ct_st, n_st, m_st, *, reverse, bb):
    L = ML_CHUNK
    NH = ML_HEADS
    d = 1 if reverse else 0

    @pl.when(pl.program_id(1) == 0)
    def _():
        ct_st[...] = jnp.zeros_like(ct_st)
        n_st[...] = jnp.zeros_like(n_st)
        m_st[...] = jnp.zeros_like(m_st)

    row8 = lax.broadcasted_iota(jnp.int32, (SUBLANES, L), 0)
    s_id = lax.broadcasted_iota(jnp.int32, (L, L), 0)
    t_id = lax.broadcasted_iota(jnp.int32, (L, L), 1)
    valid = (s_id >= t_id) if reverse else (s_id <= t_id)
    zeros8 = jnp.zeros((SUBLANES, L), F32)
    tile = lambda rows, hd: jnp.broadcast_to(rows[hd:hd + 1, :], (L, L))
    row_of = lambda rows, hd: jnp.broadcast_to(rows[hd:hd + 1, :], (SUBLANES, L))
    pairs = [(b, hd) for b in range(bb) for hd in range(NH)]

    seq = []
    for b in range(bb):
        base = G_ROWS * d
        bcum = gr_ref[b, base + G_B:base + G_B + 8, :]
        u = gr_ref[b, base + G_U:base + G_U + 8, :]
        b_last = gr_ref[b, base + G_BL:base + G_BL + 8, :]
        m_prev = m_st[b]
        inter = bcum + m_prev
        m_t = jnp.maximum(inter, gr_ref[b, base + G_A:base + G_A + 8, :])
        m_new = jnp.maximum(b_last + m_prev, gr_ref[b, base + G_GM:base + G_GM + 8, :])
        n_prev = n_st[b]
        wg = jnp.exp(u + (b_last - m_new))
        seq.append(dict(
            w_int=jnp.exp(inter - m_t), e_neg=jnp.exp(-m_t), v3=_split3(bcum - m_t),
            decay=jnp.exp(b_last + m_prev - m_new), m_new=m_new, n_prev=n_prev, wg=wg,
            n_lhs=jnp.concatenate([n_prev, zeros8], axis=0).astype(BF16),
            wg_lhs=jnp.concatenate([wg, zeros8], axis=0).astype(BF16)))

    st, expo = {}, {}
    for b, hd in pairs:
        cs = slice(hd * ML_HD, (hd + 1) * ML_HD)
        st[b, hd] = _dot_nt(jnp.concatenate([k_ref[b, :, cs], seq[b]["n_lhs"]], axis=0), q_ref[b, :, cs])
        onehot = jnp.where(row8 == hd, 1.0, 0.0)
        v_hi, v_mid, v_lo = (row_of(x, hd) for x in seq[b]["v3"])
        v_rows = jnp.where(row8 == 0, v_hi, jnp.where(row8 == 1, v_mid, jnp.where(row8 == 2, v_lo, 0.0)))
        slabs = [zeros8] * (L // SUBLANES)
        for j in range(3):
            slabs[(UC_DIR * d) // SUBLANES + j] = onehot
        slabs[UC_ONES // SUBLANES] = v_rows
        expo[b, hd] = _dot(uc_ref[b], jnp.concatenate(slabs, axis=0).astype(BF16))

    for b, hd in pairs:
        cs = slice(hd * ML_HD, (hd + 1) * ML_HD)
        sq = seq[b]
        s_t = st[b, hd][0:L] * jnp.exp(jnp.where(valid, expo[b, hd], -jnp.inf))
        w_h = sq["w_int"][hd:hd + 1, :]
        den = jnp.sum(s_t, axis=0, keepdims=True) + w_h * st[b, hd][L + hd:L + hd + 1, :]
        r = 1.0 / jnp.maximum(jnp.abs(den), sq["e_neg"][hd:hd + 1, :])
        h_ref[b, cs, :] = (_dot(vt_ref[b, cs, :], (s_t * r).astype(BF16))
                           + _dot_nt(ct_st[b, hd].astype(BF16), q_ref[b, :, cs]) * (r * w_h)
                           ).astype(h_ref.dtype)

    for b, hd in pairs:
        cs = slice(hd * ML_HD, (hd + 1) * ML_HD)
        sq = seq[b]
        vw = (vt_ref[b, cs, :].astype(F32) * sq["wg"][hd:hd + 1, :]).astype(BF16)
        upd = _dot(jnp.concatenate([vw, sq["wg_lhs"]], axis=0), k_ref[b, :, cs])
        ct_st[b, hd] = tile(sq["decay"], hd) * ct_st[b, hd] + upd[0:L]
        n_st[b, hd:hd + 1, :] = (sq["decay"][hd:hd + 1, :] * sq["n_prev"][hd:hd + 1, :]
                                 + upd[L + hd:L + hd + 1, :])
    for b in range(bb):
        m_st[b] = seq[b]["m_new"]


def _mlstm(qk3, vt, gr4, uc3, *, reverse):
    B, S, _ = qk3.shape
    L = ML_CHUNK
    nc = S // L
    bb = ML_SEQS

    def chunk(c):
        return (nc - 1 - c) if reverse else c

    kern = functools.partial(_mlstm_kernel, reverse=reverse, bb=bb)
    return pl.pallas_call(
        kern,
        grid=(B // bb, nc),
        in_specs=[
            pl.BlockSpec((bb, L, D_ML), lambda b, c: (b, chunk(c), 0)),
            pl.BlockSpec((bb, L, D_ML), lambda b, c: (b, chunk(c), 1)),
            pl.BlockSpec((bb, D_ML, L), lambda b, c: (b, 0, chunk(c))),
            pl.BlockSpec((bb, None, 2 * G_ROWS, L), lambda b, c: (b, chunk(c), 0, 0)),
            pl.BlockSpec((bb, L, LANES), lambda b, c: (b, chunk(c), 0)),
        ],
        out_specs=pl.BlockSpec((bb, D_ML, L), lambda b, c: (b, 0, chunk(c))),
        out_shape=jax.ShapeDtypeStruct((B, D_ML, S), BF16),
        scratch_shapes=[
            pltpu.VMEM((bb, ML_HEADS, ML_HD, ML_HD), F32),
            pltpu.VMEM((bb, 2 * ML_HEADS, ML_HD), F32),
            pltpu.VMEM((bb, 2 * ML_HEADS, LANES), F32),
        ],
        compiler_params=_params("parallel", "arbitrary"),
        name="mlstm_bwd" if reverse else "mlstm_fwd",
    )(qk3, qk3, vt, gr4, uc3)


def _outproj_kernel(x_ref, rf_ref, rb_ref, gate_ref, mf_ref, mb_ref, ot_ref, mg_ref, wr_ref, wm_ref,
                    y_ref):
    y_rg = (rf_ref[...].astype(F32) + rb_ref[...].astype(F32)) * _gelu(gate_ref[...].astype(F32))
    acc = _dot(y_rg.astype(BF16), wr_ref[...])
    h_t = mf_ref[...].astype(F32) + mb_ref[...].astype(F32)
    parts = []
    for hd in range(ML_HEADS):
        hh = h_t[hd * ML_HD:(hd + 1) * ML_HD]
        parts.append(hh * lax.rsqrt(jnp.mean(hh * hh, axis=0, keepdims=True) + EPS))
    mg = jnp.tile(mg_ref[...], (1, h_t.shape[1] // LANES))
    y_t = jax.nn.sigmoid(ot_ref[...].astype(F32)) * (jnp.concatenate(parts, axis=0) * mg)
    acc = acc + _dot(y_t.T.astype(BF16), wm_ref[...])
    y_ref[...] = x_ref[...] + acc


def _outproj(x2, rf, rb, pa, mf_t, mb_t, o_t, mg_tile, w_rg, w_ml, *, S):
    T = x2.shape[0]
    tm = TM_PROJ
    nb = S // tm
    tok = lambda width, col: pl.BlockSpec((tm, width), lambda i: (i, col))
    seq_t = pl.BlockSpec((None, D_ML, tm), lambda i: (i // nb, 0, i % nb))
    full = lambda shape: pl.BlockSpec(shape, lambda i: (0,) * len(shape))
    return pl.pallas_call(
        _outproj_kernel,
        grid=(T // tm,),
        in_specs=[
            tok(D_MODEL, 0), tok(D_RG, 0), tok(D_RG, 0), tok(D_RG, 0),
            seq_t, seq_t, seq_t,
            full((D_ML, LANES)), full((D_RG, D_MODEL)), full((D_ML, D_MODEL)),
        ],
        out_specs=tok(D_MODEL, 0),
        out_shape=jax.ShapeDtypeStruct((T, D_MODEL), F32),
        compiler_params=_params("parallel"),
        name="outproj",
    )(x2, rf, rb, pa, mf_t, mb_t, o_t, mg_tile, w_rg, w_ml)


def _gelu_gate(gate, val):
    k0 = -2.0 * 0.7978845608028654 * 1.4426950408889634
    z = gate * (k0 + (k0 * 0.044715) * (gate * gate))
    return (gate * val) / (1.0 + jnp.exp2(z))


def _ffn_kernel(x_ref, prev_ref, next_ref, g_ref, wu_ref, cw_ref, cb_ref, wd_ref, fg_ref, y_ref,
                slabs, hbuf, uvbuf, acc, *, nblk, tb, final):
    blk = pl.program_id(0) % nblk
    sub = FF_SUB
    nsub = D_FF // sub
    nring = uvbuf.shape[0]
    nslab = D_MODEL // LANES
    seg = tb // SUBLANES
    pitch = seg + SUBLANES

    g = g_ref[...]
    hn = _rmsnorm(x_ref[...], g)
    for k in range(nslab):
        for s in range(SUBLANES):
            slabs[k, pitch * s:pitch * s + seg, :] = hn[seg * s:seg * (s + 1), LANES * k:LANES * (k + 1)]

    def perm_rows(j):
        return jnp.concatenate(
            [slabs[k, pl.ds(j, SUBLANES, stride=pitch), :] for k in range(nslab)], axis=1)

    for jj in range(seg // 2):
        hbuf[BF16_ROWS * jj:BF16_ROWS * (jj + 1), :] = jnp.concatenate(
            [perm_rows(2 * jj), perm_rows(2 * jj + 1)], axis=0).astype(BF16)
    row_x = lax.broadcasted_iota(jnp.int32, (SUBLANES, D_MODEL), 0)
    h_prev = jnp.where(blk == 0, 0.0, pltpu.roll(_rmsnorm(prev_ref[...], g), 1, 0))
    h_next = jnp.where(blk == nblk - 1, 0.0, pltpu.roll(_rmsnorm(next_ref[...], g), 1, 0))
    halo = jnp.where(row_x == 0, h_prev, jnp.where(row_x == 1, h_next, 0.0))
    hbuf[tb:, :] = jnp.concatenate([halo, jnp.zeros_like(halo)], axis=0).astype(BF16)

    row_u = lax.broadcasted_iota(jnp.int32, (SUBLANES, 2 * sub), 0)

    def up(sc):
        cols = slice(2 * sub * sc, 2 * sub * (sc + 1))
        slot = sc % nring
        res = _dot(hbuf[...], wu_ref[:, cols])
        uvbuf[slot, SUBLANES:SUBLANES + tb, :] = res[0:tb]
        uvbuf[slot, 0:SUBLANES, :] = jnp.where(
            row_u == 0, jnp.broadcast_to(res[tb:tb + 1], row_u.shape),
            pltpu.roll(res[tb - SUBLANES:tb], 1, 0))
        uvbuf[slot, SUBLANES + tb:, :] = jnp.where(
            row_u == SUBLANES - 1, jnp.broadcast_to(res[tb + 1:tb + 2], row_u.shape),
            pltpu.roll(res[0:SUBLANES], SUBLANES - 1, 0))

    def gate(sc):
        cols = slice(2 * sub * sc, 2 * sub * (sc + 1))
        slot = sc % nring
        c = cb_ref[:, cols] + uvbuf[slot, 0:tb, :] * cw_ref[0:1, cols]
        c = c + uvbuf[slot, SUBLANES:SUBLANES + tb, :] * cw_ref[1:2, cols]
        c = c + uvbuf[slot, 2 * SUBLANES:2 * SUBLANES + tb, :] * cw_ref[2:3, cols]
        return _gelu_gate(c[:, :sub], c[:, sub:]).astype(BF16)

    up(0)
    up(1)
    for p in range(nsub // 2):
        if 2 * p + 2 < nsub:
            up(2 * p + 2)
            up(2 * p + 3)
        d = _dot(jnp.concatenate([gate(2 * p), gate(2 * p + 1)], axis=1),
                 wd_ref[2 * sub * p:2 * sub * (p + 1), :])
        if p == 0:
            acc[...] = d
        else:
            acc[...] += d

    for j in range(seg):
        for k in range(nslab):
            slabs[k, pl.ds(j, SUBLANES, stride=pitch), :] = acc[SUBLANES * j:SUBLANES * (j + 1),
                                                                LANES * k:LANES * (k + 1)]
    ffn = jnp.concatenate(
        [jnp.concatenate([slabs[k, pitch * s:pitch * s + seg, :] for s in range(SUBLANES)], axis=0)
         for k in range(nslab)], axis=1)
    y = x_ref[...] + ffn
    if final:
        y = _rmsnorm(y, fg_ref[...])
    y_ref[...] = y


def _ffn(x2, g, w_up, cw, cb, w_down, fg, *, S, final):
    T = x2.shape[0]
    tb = TB_FFN
    nblk = S // tb
    hpb = tb // SUBLANES
    n_halo = T // SUBLANES
    full = lambda shape: pl.BlockSpec(shape, lambda i: (0,) * len(shape))
    kern = functools.partial(_ffn_kernel, nblk=nblk, tb=tb, final=final)
    return pl.pallas_call(
        kern,
        grid=(T // tb,),
        in_specs=[
            pl.BlockSpec((tb, D_MODEL), lambda i: (i, 0)),
            pl.BlockSpec((SUBLANES, D_MODEL), lambda i: (jnp.maximum(i * hpb - 1, 0), 0)),
            pl.BlockSpec((SUBLANES, D_MODEL), lambda i: (jnp.minimum((i + 1) * hpb, n_halo - 1), 0)),
            full((1, D_MODEL)), full((D_MODEL, 2 * D_FF)), full((3, 2 * D_FF)), full((1, 2 * D_FF)),
            full((D_FF, D_MODEL)), full((1, D_MODEL)),
        ],
        out_specs=pl.BlockSpec((tb, D_MODEL), lambda i: (i, 0)),
        out_shape=jax.ShapeDtypeStruct((T, D_MODEL), F32),
        scratch_shapes=[
            pltpu.VMEM((D_MODEL // LANES, tb + SUBLANES * SUBLANES, LANES), F32),
            pltpu.VMEM((tb + BF16_ROWS, D_MODEL), BF16),
            pltpu.VMEM((FF_RING, tb + 2 * SUBLANES, 2 * FF_SUB), F32),
            pltpu.VMEM((tb, D_MODEL), F32),
        ],
        compiler_params=pltpu.CompilerParams(dimension_semantics=("parallel",),
                                             vmem_limit_bytes=VMEM_LIMIT_FFN),
        name="convffn",
    )(x2, x2, x2, g, w_up, cw, cb, w_down, fg)


def _pair_columns(w):
    lead = w.shape[:-1]
    n = D_FF // FF_SUB
    return jnp.swapaxes(w.reshape(*lead, 2, n, FF_SUB), -3, -2).reshape(*lead, 2 * D_FF)


def _block_diag(w):
    eye = jnp.eye(RG_BLOCKS, dtype=w.dtype)
    return jnp.einsum('ncd,nm->ncmd', w, eye).reshape(D_RG, D_RG)


def _encoder(x, norm1_g, w_in, b_gates, rg_conv_w, rg_conv_b, rg_wa, rg_ba, rg_wx, rg_bx, rg_lambda,
             ml_norm_g, w_out, norm2_g, w_up, ffn_conv_w, ffn_conv_b, w_down, final_g):
    B, S, _ = x.shape
    T = B * S
    depth = w_in.shape[0]
    x2 = x.reshape(T, D_MODEL)
    row = lambda v: v.reshape(1, -1).astype(F32)
    n_nat = 2 * D_RG + 2 * D_ML
    for l in range(depth):
        w_nat = w_in[l, :, :n_nat].astype(BF16)
        w_tr = w_in[l, :, n_nat:n_nat + 2 * D_ML].T.astype(BF16)
        w_gate = w_in[l, :, n_nat + 2 * D_ML:].T.astype(BF16)
        bias = jnp.broadcast_to(b_gates[l].astype(F32).reshape(N_GATE, 1), (N_GATE, ML_CHUNK))
        rx, rgate, qk, v_t, o_t, gr, uc = _inproj(x2, row(norm1_g[l]), w_nat, w_tr, w_gate, bias,
                                                  B=B, S=S)
        qk3 = qk.reshape(B, S, 2 * D_ML)
        gr4 = gr.reshape(B, S // ML_CHUNK, 2 * G_ROWS, ML_CHUNK)
        uc3 = uc.reshape(B, S, LANES)
        r_dir, m_dir = [], []
        for d, reverse in enumerate((False, True)):
            r_dir.append(_rglru(
                rx, rg_conv_w[l].astype(F32), row(rg_conv_b[l]),
                _block_diag(rg_wa[l, d]).astype(BF16), _block_diag(rg_wx[l, d]).astype(BF16),
                row(rg_ba[l, d]), row(rg_bx[l, d]), row(rg_lambda[l, d]), B=B, S=S, reverse=reverse))
            m_dir.append(_mlstm(qk3, v_t, gr4, uc3, reverse=reverse))
        wo = w_out[l].astype(BF16)
        mg_tile = jnp.broadcast_to(ml_norm_g[l].astype(F32).reshape(D_ML, 1), (D_ML, LANES))
        x2 = _outproj(x2, r_dir[0], r_dir[1], rgate, m_dir[0], m_dir[1], o_t, mg_tile,
                      wo[:D_RG], wo[D_RG:], S=S)
        x2 = _ffn(x2, row(norm2_g[l]), _pair_columns(w_up[l]).astype(BF16),
                  _pair_columns(ffn_conv_w[l].astype(F32)), _pair_columns(row(ffn_conv_b[l])),
                  w_down[l].astype(BF16), row(final_g), S=S,
                  final=(l == depth - 1))
    return x2.reshape(B, S, D_MODEL)


def kernel(x_prompt, x_sample, norm1_g, w_in, b_gates, rg_conv_w, rg_conv_b, rg_wa, rg_ba, rg_wx, rg_bx,
           rg_lambda, ml_norm_g, w_out, norm2_g, w_up, ffn_conv_w, ffn_conv_b, w_down, final_g):
    weights = (norm1_g, w_in, b_gates, rg_conv_w, rg_conv_b, rg_wa, rg_ba, rg_wx, rg_bx, rg_lambda,
               ml_norm_g, w_out, norm2_g, w_up, ffn_conv_w, ffn_conv_b, w_down, final_g)
    return (_encoder(x_prompt, *weights), _encoder(x_sample, *weights))
```

```python
import functools

import jax
import jax.numpy as jnp
from jax import lax
from jax.experimental import pallas as pl
from jax.experimental.pallas import tpu as pltpu

F32 = jnp.float32
BF16 = jnp.bfloat16

D_MODEL = 1024
D_RG = 512
D_ML = 512
RG_BLOCKS = 8
RG_C = 8.0
ML_HEADS = 4
ML_HD = 128
ML_CHUNK = 128
D_FF = 3072
EPS = 1e-6
N_GATE = 4 * ML_HEADS

SUBLANES = 8
LANES = 128
BF16_ROWS = 16
VMEM_LIMIT = 48 * 1024 * 1024

TM_PROJ = 512
TB_RG = 512
ML_SEQS = 8
TB_FFN = 512
FF_SUB = 256
FF_RING = 4
VMEM_LIMIT_FFN = 56 * 1024 * 1024

G_B, G_U, G_A, G_BL, G_GM = 0, 8, 16, 24, 32
G_ROWS = 40
UC_DIR = 24
UC_ONES = 2 * UC_DIR


def _params(*sem):
    return pltpu.CompilerParams(dimension_semantics=sem, vmem_limit_bytes=VMEM_LIMIT)


def _softplus(z):
    return jnp.maximum(z, 0.0) + jnp.log1p(jnp.exp(-jnp.abs(z)))


def _gelu(x):
    return 0.5 * x * (1.0 + jnp.tanh(0.7978845608028654 * (x + 0.044715 * (x * x * x))))


def _rmsnorm(x, g):
    return x * lax.rsqrt(jnp.mean(x * x, axis=-1, keepdims=True) + EPS) * g


def _dot(a, b):
    return jnp.dot(a, b, preferred_element_type=F32)


def _dot_nt(a, b):
    return lax.dot_general(a, b, (((1,), (1,)), ((), ())), preferred_element_type=F32)


def _split3(x):
    hi = x.astype(BF16).astype(F32)
    r1 = x - hi
    mid = r1.astype(BF16).astype(F32)
    return hi, mid, (r1 - mid).astype(BF16).astype(F32)


def _lane_scan(x, op, fill, reverse):
    n = x.shape[-1]
    lane = lax.broadcasted_iota(jnp.int32, x.shape, 1)
    s = 1
    while s < n:
        if reverse:
            x = op(x, jnp.where(lane < n - s, pltpu.roll(x, n - s, 1), fill))
        else:
            x = op(x, jnp.where(lane >= s, pltpu.roll(x, s, 1), fill))
        s *= 2
    return x


def _inproj_kernel(x_ref, g_ref, wn_ref, wt_ref, wg_ref, bias_ref, rx_ref, rg_ref, qk_ref, vt_ref,
                   ot_ref, gr_ref, uc_ref):
    L = ML_CHUNK
    NH = ML_HEADS
    h = _rmsnorm(x_ref[...], g_ref[...]).astype(BF16)

    gt = _dot_nt(wg_ref[...], h)
    rowid = lax.broadcasted_iota(jnp.int32, (2 * NH, L), 0)
    head_row = rowid < NH
    rep = lambda col: jnp.broadcast_to(col, (2 * NH, L))
    zeros8 = jnp.zeros((2 * NH, L), F32)
    for c in range(gr_ref.shape[0]):
        g16 = gt[:, c * L:(c + 1) * L] + bias_ref[...]
        tiles = []
        for d, reverse in enumerate((False, True)):
            gates = g16[2 * NH * d:2 * NH * (d + 1)]
            lf = jnp.where(head_row, 0.0, -_softplus(-gates))
            bcum = pltpu.roll(_lane_scan(lf, jnp.add, 0.0, reverse), NH, 0)
            u = jnp.where(head_row, gates - bcum, 0.0)
            last = 0 if reverse else L - 1
            bl = rep(bcum[:, last:last + 1])
            base = G_ROWS * d
            gr_ref[c, base + G_B:base + G_B + 8, :] = bcum
            gr_ref[c, base + G_U:base + G_U + 8, :] = u
            gr_ref[c, base + G_A:base + G_A + 8, :] = bcum + _lane_scan(u, jnp.maximum, -jnp.inf, reverse)
            gr_ref[c, base + G_BL:base + G_BL + 8, :] = bl
            gr_ref[c, base + G_GM:base + G_GM + 8, :] = rep(jnp.max(bl + u, axis=-1, keepdims=True))
            tiles.extend(_split3(u))
        tiles.append(jnp.ones((2 * NH, L), F32))
        tiles.extend([zeros8] * (L // 8 - len(tiles)))
        uc_ref[c * L:(c + 1) * L, :] = jnp.concatenate(tiles, axis=0).T.astype(BF16)

    nat = _dot(h, wn_ref[...])
    rx_ref[...] = nat[:, :D_RG]
    rg_ref[...] = nat[:, D_RG:2 * D_RG].astype(BF16)
    qk_ref[...] = jnp.concatenate(
        [nat[:, 2 * D_RG:2 * D_RG + D_ML] * (ML_HD ** -0.5), nat[:, 2 * D_RG + D_ML:]], axis=-1).astype(BF16)
    tr = _dot_nt(wt_ref[...], h)
    vt_ref[...] = tr[:D_ML].astype(BF16)
    ot_ref[...] = tr[D_ML:].astype(BF16)


def _inproj(x2, g, w_nat, w_tr, w_gate, bias, *, B, S):
    T = B * S
    tm = TM_PROJ
    nb = S // tm
    full = lambda shape: pl.BlockSpec(shape, lambda i: (0,) * len(shape))
    tok = lambda width: pl.BlockSpec((tm, width), lambda i: (i, 0))
    seq_t = pl.BlockSpec((None, D_ML, tm), lambda i: (i // nb, 0, i % nb))
    return pl.pallas_call(
        _inproj_kernel,
        grid=(T // tm,),
        in_specs=[
            tok(D_MODEL), full((1, D_MODEL)), full(w_nat.shape), full(w_tr.shape), full(w_gate.shape),
            full((N_GATE, ML_CHUNK)),
        ],
        out_specs=[
            tok(D_RG), tok(D_RG), tok(2 * D_ML), seq_t, seq_t,
            pl.BlockSpec((tm // ML_CHUNK, 2 * G_ROWS, ML_CHUNK), lambda i: (i, 0, 0)),
            tok(LANES),
        ],
        out_shape=[
            jax.ShapeDtypeStruct((T, D_RG), F32),
            jax.ShapeDtypeStruct((T, D_RG), BF16),
            jax.ShapeDtypeStruct((T, 2 * D_ML), BF16),
            jax.ShapeDtypeStruct((B, D_ML, S), BF16),
            jax.ShapeDtypeStruct((B, D_ML, S), BF16),
            jax.ShapeDtypeStruct((T // ML_CHUNK, 2 * G_ROWS, ML_CHUNK), F32),
            jax.ShapeDtypeStruct((T, LANES), BF16),
        ],
        compiler_params=_params("parallel"),
        name="inproj",
    )(x2, g, w_nat, w_tr, w_gate, bias)


def _rglru_kernel(x_ref, prev_ref, next_ref, cw_ref, cb_ref, wa_ref, wx_ref, ba_ref, bx_ref,
                  lam_ref, h_ref, slabs, xbuf, pbuf, lbuf, carry, *, reverse, nblk, tb):
    step = pl.program_id(1)
    blk = (nblk - 1 - step) if reverse else step
    nslab = D_RG // LANES
    seg = tb // SUBLANES
    pitch = seg + SUBLANES
    X0 = 2 * SUBLANES

    @pl.when(step == 0)
    def _():
        carry[...] = jnp.zeros_like(carry)

    x = x_ref[...]
    for k in range(nslab):
        for s in range(SUBLANES):
            slabs[k, pitch * s:pitch * s + seg, :] = x[seg * s:seg * (s + 1), LANES * k:LANES * (k + 1)]
    seam = {}
    for j in range(seg):
        rows = jnp.concatenate(
            [slabs[k, pl.ds(j, SUBLANES, stride=pitch), :] for k in range(nslab)], axis=1)
        xbuf[X0 + SUBLANES * j:X0 + SUBLANES * (j + 1), :] = rows
        if j in (0, seg - 2, seg - 1):
            seam[j] = rows
    row = lax.broadcasted_iota(jnp.int32, (SUBLANES, D_RG), 0)
    tile_row = lambda v, i: jnp.broadcast_to(v[i:i + 1, :], (SUBLANES, D_RG))
    prev = jnp.where(blk == 0, 0.0, prev_ref[...])
    nxt = jnp.where(blk == nblk - 1, 0.0, next_ref[...])
    xbuf[0:SUBLANES, :] = jnp.where(row == 0, tile_row(prev, SUBLANES - 2), pltpu.roll(seam[seg - 2], 1, 0))
    xbuf[SUBLANES:X0, :] = jnp.where(row == 0, tile_row(prev, SUBLANES - 1), pltpu.roll(seam[seg - 1], 1, 0))
    xbuf[X0 + tb:, :] = jnp.where(row == SUBLANES - 1, tile_row(nxt, 0),
                                  pltpu.roll(seam[0], SUBLANES - 1, 0))
    xc = cb_ref[...] + xbuf[0:tb, :] * cw_ref[0:1, :]
    xc = xc + xbuf[SUBLANES:SUBLANES + tb, :] * cw_ref[1:2, :]
    xc = xc + xbuf[X0:X0 + tb, :] * cw_ref[2:3, :]
    xc = xc + xbuf[X0 + SUBLANES:X0 + SUBLANES + tb, :] * cw_ref[3:4, :]

    xcb = xc.astype(BF16)
    r = jax.nn.sigmoid(_dot(xcb, wa_ref[...]) + ba_ref[...])
    i = jax.nn.sigmoid(_dot(xcb, wx_ref[...]) + bx_ref[...])
    decay_rate = RG_C * _softplus(-lam_ref[...])
    a = jnp.exp2((decay_rate * -1.4426950408889634) * r)
    y = jnp.tanh(decay_rate * r) * (a * a + 1.0)
    u = jnp.where(y > 0.0, y * lax.rsqrt(y), 0.0) * (i * xc)

    order = range(seg - 1, -1, -1) if reverse else range(seg)
    P = L = None
    for j in order:
        rs = slice(SUBLANES * j, SUBLANES * (j + 1))
        if P is None:
            P, L = a[rs], u[rs]
        else:
            P, L = a[rs] * P, a[rs] * L + u[rs]
        pbuf[rs, :] = P
        lbuf[rs, :] = L

    A, U = P, L
    for s in (1, 2, 4):
        if reverse:
            keep = row < SUBLANES - s
            shift = SUBLANES - s
        else:
            keep = row >= s
            shift = s
        a_sh = jnp.where(keep, pltpu.roll(A, shift, 0), 1.0)
        u_sh = jnp.where(keep, pltpu.roll(U, shift, 0), 0.0)
        U = A * u_sh + U
        A = A * a_sh
    c_in = carry[...]
    e = U + A * c_in
    if reverse:
        c_seg = jnp.where(row == SUBLANES - 1, c_in, pltpu.roll(e, SUBLANES - 1, 0))
        carry[...] = tile_row(e, 0)
    else:
        c_seg = jnp.where(row == 0, c_in, pltpu.roll(e, 1, 0))
        carry[...] = tile_row(e, SUBLANES - 1)

    for j in range(seg):
        rs = slice(SUBLANES * j, SUBLANES * (j + 1))
        hj = lbuf[rs, :] + pbuf[rs, :] * c_seg
        for k in range(nslab):
            slabs[k, pl.ds(j, SUBLANES, stride=pitch), :] = hj[:, LANES * k:LANES * (k + 1)]
    for k in range(nslab):
        for s in range(SUBLANES):
            h_ref[seg * s:seg * (s + 1), LANES * k:LANES * (k + 1)] = (
                slabs[k, pitch * s:pitch * s + seg, :].astype(h_ref.dtype))


def _rglru(p, cw, cb, wa, wx, ba, bx, lam, *, B, S, reverse):
    T = B * S
    tb = TB_RG
    nblk = S // tb
    hb = tb // SUBLANES
    n_halo = T // SUBLANES

    def blk_of(j):
        return (nblk - 1 - j) if reverse else j

    def main_map(b, j):
        return (b * nblk + blk_of(j), 0)

    def prev_map(b, j):
        return (jnp.maximum((b * nblk + blk_of(j)) * hb - 1, 0), 0)

    def next_map(b, j):
        return (jnp.minimum((b * nblk + blk_of(j) + 1) * hb, n_halo - 1), 0)

    full = lambda shape: pl.BlockSpec(shape, lambda b, j: (0,) * len(shape))
    kern = functools.partial(_rglru_kernel, reverse=reverse, nblk=nblk, tb=tb)
    return pl.pallas_call(
        kern,
        grid=(B, nblk),
        in_specs=[
            pl.BlockSpec((tb, D_RG), main_map),
            pl.BlockSpec((SUBLANES, D_RG), prev_map),
            pl.BlockSpec((SUBLANES, D_RG), next_map),
            full((4, D_RG)), full((1, D_RG)),
            full((D_RG, D_RG)), full((D_RG, D_RG)),
            full((1, D_RG)), full((1, D_RG)), full((1, D_RG)),
        ],
        out_specs=pl.BlockSpec((tb, D_RG), main_map),
        out_shape=jax.ShapeDtypeStruct((T, D_RG), BF16),
        scratch_shapes=[
            pltpu.VMEM((D_RG // LANES, tb + SUBLANES * SUBLANES, LANES), F32),
            pltpu.VMEM((tb + 3 * SUBLANES, D_RG), F32),
            pltpu.VMEM((tb, D_RG), F32),
            pltpu.VMEM((tb, D_RG), F32),
            pltpu.VMEM((SUBLANES, D_RG), F32),
        ],
        compiler_params=_params("parallel", "arbitrary"),
        name="rglru_bwd" if reverse else "rglru_fwd",
    )(p, p, p, cw, cb, wa, wx, ba, bx, lam)


def _mlstm_kernel(q_ref, k_ref, vt_ref, gr_ref, uc_ref, h_ref, ct_st, n_st, m_st, *, reverse, bb):
    L = ML_CHUNK
    NH = ML_HEADS
    d = 1 if reverse else 0

    @pl.when(pl.program_id(1) == 0)
    def _():
        ct_st[...] = jnp.zeros_like(ct_st)
        n_st[...] = jnp.zeros_like(n_st)
        m_st[...] = jnp.zeros_like(m_st)

    row8 = lax.broadcasted_iota(jnp.int32, (SUBLANES, L), 0)
    s_id = lax.broadcasted_iota(jnp.int32, (L, L), 0)
    t_id = lax.broadcasted_iota(jnp.int32, (L, L), 1)
    valid = (s_id >= t_id) if reverse else (s_id <= t_id)
    zeros8 = jnp.zeros((SUBLANES, L), F32)
    tile = lambda rows, hd: jnp.broadcast_to(rows[hd:hd + 1, :], (L, L))
    row_of = lambda rows, hd: jnp.broadcast_to(rows[hd:hd + 1, :], (SUBLANES, L))
    pairs = [(b, hd) for b in range(bb) for hd in range(NH)]

    seq = []
    for b in range(bb):
        base = G_ROWS * d
        bcum = gr_ref[b, base + G_B:base + G_B + 8, :]
        u = gr_ref[b, base + G_U:base + G_U + 8, :]
        b_last = gr_ref[b, base + G_BL:base + G_BL + 8, :]
        m_prev = m_st[b]
        inter = bcum + m_prev
        m_t = jnp.maximum(inter, gr_ref[b, base + G_A:base + G_A + 8, :])
        m_new = jnp.maximum(b_last + m_prev, gr_ref[b, base + G_GM:base + G_GM + 8, :])
        n_prev = n_st[b]
        wg = jnp.exp(u + (b_last - m_new))
        seq.append(dict(
            w_int=jnp.exp(inter - m_t), e_neg=jnp.exp(-m_t), v3=_split3(bcum - m_t),
            decay=jnp.exp(b_last + m_prev - m_new), m_new=m_new, n_prev=n_prev, wg=wg,
            n_lhs=jnp.concatenate([n_prev, zeros8], axis=0).astype(BF16),
            wg_lhs=jnp.concatenate([wg, zeros8], axis=0).astype(BF16)))

    st, expo = {}, {}
    for b, hd in pairs:
        cs = slice(hd * ML_HD, (hd + 1) * ML_HD)
        st[b, hd] = _dot_nt(jnp.concatenate([k_ref[b, :, cs], seq[b]["n_lhs"]], axis=0), q_ref[b, :, cs])
        onehot = jnp.where(row8 == hd, 1.0, 0.0)
        v_hi, v_mid, v_lo = (row_of(x, hd) for x in seq[b]["v3"])
        v_rows = jnp.where(row8 == 0, v_hi, jnp.where(row8 == 1, v_mid, jnp.where(row8 == 2, v_lo, 0.0)))
        slabs = [zeros8] * (L // SUBLANES)
        for j in range(3):
            slabs[(UC_DIR * d) // SUBLANES + j] = onehot
        slabs[UC_ONES // SUBLANES] = v_rows
        expo[b, hd] = _dot(uc_ref[b], jnp.concatenate(slabs, axis=0).astype(BF16))

    for b, hd in pairs:
        cs = slice(hd * ML_HD, (hd + 1) * ML_HD)
        sq = seq[b]
        s_t = st[b, hd][0:L] * jnp.exp(jnp.where(valid, expo[b, hd], -jnp.inf))
        w_h = sq["w_int"][hd:hd + 1, :]
        den = jnp.sum(s_t, axis=0, keepdims=True) + w_h * st[b, hd][L + hd:L + hd + 1, :]
        r = 1.0 / jnp.maximum(jnp.abs(den), sq["e_neg"][hd:hd + 1, :])
        h_ref[b, cs, :] = (_dot(vt_ref[b, cs, :], (s_t * r).astype(BF16))
                           + _dot_nt(ct_st[b, hd].astype(BF16), q_ref[b, :, cs]) * (r * w_h)
                           ).astype(h_ref.dtype)

    for b, hd in pairs:
        cs = slice(hd * ML_HD, (hd + 1) * ML_HD)
        sq = seq[b]
        vw = (vt_ref[b, cs, :].astype(F32) * sq["wg"][hd:hd + 1, :]).astype(BF16)
        upd = _dot(jnp.concatenate([vw, sq["wg_lhs"]], axis=0), k_ref[b, :, cs])
        ct_st[b, hd] = tile(sq["decay"], hd) * ct_st[b, hd] + upd[0:L]
        n_st[b, hd:hd + 1, :] = (sq["decay"][hd:hd + 1, :] * sq["n_prev"][hd:hd + 1, :]
                                 + upd[L + hd:L + hd + 1, :])
    for b in range(bb):
        m_st[b] = seq[b]["m_new"]


def _mlstm(qk3, vt, gr4, uc3, *, reverse):
    B, S, _ = qk3.shape
    L = ML_CHUNK
    nc = S // L
    bb = ML_SEQS

    def chunk(c):
        return (nc - 1 - c) if reverse else c

    kern = functools.partial(_mlstm_kernel, reverse=reverse, bb=bb)
    return pl.pallas_call(
        kern,
        grid=(B // bb, nc),
        in_specs=[
            pl.BlockSpec((bb, L, D_ML), lambda b, c: (b, chunk(c), 0)),
            pl.BlockSpec((bb, L, D_ML), lambda b, c: (b, chunk(c), 1)),
            pl.BlockSpec((bb, D_ML, L), lambda b, c: (b, 0, chunk(c))),
            pl.BlockSpec((bb, None, 2 * G_ROWS, L), lambda b, c: (b, chunk(c), 0, 0)),
            pl.BlockSpec((bb, L, LANES), lambda b, c: (b, chunk(c), 0)),
        ],
        out_specs=pl.BlockSpec((bb, D_ML, L), lambda b, c: (b, 0, chunk(c))),
        out_shape=jax.ShapeDtypeStruct((B, D_ML, S), BF16),
        scratch_shapes=[
            pltpu.VMEM((bb, ML_HEADS, ML_HD, ML_HD), F32),
            pltpu.VMEM((bb, 2 * ML_HEADS, ML_HD), F32),
            pltpu.VMEM((bb, 2 * ML_HEADS, LANES), F32),
        ],
        compiler_params=_params("parallel", "arbitrary"),
        name="mlstm_bwd" if reverse else "mlstm_fwd",
    )(qk3, qk3, vt, gr4, uc3)


def _outproj_kernel(rf_ref, rb_ref, gate_ref, mf_ref, mb_ref, ot_ref, mg_ref, wr_ref, wm_ref, y_ref):
    y_rg = (rf_ref[...].astype(F32) + rb_ref[...].astype(F32)) * _gelu(gate_ref[...].astype(F32))
    acc = _dot(y_rg.astype(BF16), wr_ref[...])
    h_t = mf_ref[...].astype(F32) + mb_ref[...].astype(F32)
    parts = []
    for hd in range(ML_HEADS):
        hh = h_t[hd * ML_HD:(hd + 1) * ML_HD]
        parts.append(hh * lax.rsqrt(jnp.mean(hh * hh, axis=0, keepdims=True) + EPS))
    mg = jnp.tile(mg_ref[...], (1, h_t.shape[1] // LANES))
    y_t = jax.nn.sigmoid(ot_ref[...].astype(F32)) * (jnp.concatenate(parts, axis=0) * mg)
    y_ref[...] = acc + _dot(y_t.T.astype(BF16), wm_ref[...])


def _outproj(rf, rb, pa, mf_t, mb_t, o_t, mg_tile, w_rg, w_ml, *, S):
    T = rf.shape[0]
    tm = TM_PROJ
    nb = S // tm
    tok = lambda width, col: pl.BlockSpec((tm, width), lambda i: (i, col))
    seq_t = pl.BlockSpec((None, D_ML, tm), lambda i: (i // nb, 0, i % nb))
    full = lambda shape: pl.BlockSpec(shape, lambda i: (0,) * len(shape))
    return pl.pallas_call(
        _outproj_kernel,
        grid=(T // tm,),
        in_specs=[
            tok(D_RG, 0), tok(D_RG, 0), tok(D_RG, 0),
            seq_t, seq_t, seq_t,
            full((D_ML, LANES)), full((D_RG, D_MODEL)), full((D_ML, D_MODEL)),
        ],
        out_specs=tok(D_MODEL, 0),
        out_shape=jax.ShapeDtypeStruct((T, D_MODEL), F32),
        compiler_params=_params("parallel"),
        name="outproj",
    )(rf, rb, pa, mf_t, mb_t, o_t, mg_tile, w_rg, w_ml)


def _gelu_gate(gate, val):
    k0 = -2.0 * 0.7978845608028654 * 1.4426950408889634
    z = gate * (k0 + (k0 * 0.044715) * (gate * gate))
    return (gate * val) / (1.0 + jnp.exp2(z))


def _ffn_kernel(x_ref, prev_ref, next_ref, d_ref, dprev_ref, dnext_ref, g_ref, wu_ref, cw_ref, cb_ref,
                wd_ref, fg_ref, y_ref, slabs, hbuf, uvbuf, acts, acc, *, nblk, tb, final):
    blk = pl.program_id(0) % nblk
    sub = FF_SUB
    nsub = D_FF // sub
    nring = uvbuf.shape[0]
    nslab = D_MODEL // LANES
    seg = tb // SUBLANES
    pitch = seg + SUBLANES

    g = g_ref[...]
    hn = _rmsnorm(x_ref[...] + d_ref[...], g)
    for k in range(nslab):
        for s in range(SUBLANES):
            slabs[k, pitch * s:pitch * s + seg, :] = hn[seg * s:seg * (s + 1), LANES * k:LANES * (k + 1)]

    def perm_rows(j):
        return jnp.concatenate(
            [slabs[k, pl.ds(j, SUBLANES, stride=pitch), :] for k in range(nslab)], axis=1)

    for jj in range(seg // 2):
        hbuf[BF16_ROWS * jj:BF16_ROWS * (jj + 1), :] = jnp.concatenate(
            [perm_rows(2 * jj), perm_rows(2 * jj + 1)], axis=0).astype(BF16)
    row_x = lax.broadcasted_iota(jnp.int32, (SUBLANES, D_MODEL), 0)
    h_prev = jnp.where(blk == 0, 0.0, pltpu.roll(_rmsnorm(prev_ref[...] + dprev_ref[...], g), 1, 0))
    h_next = jnp.where(blk == nblk - 1, 0.0,
                       pltpu.roll(_rmsnorm(next_ref[...] + dnext_ref[...], g), 1, 0))
    halo = jnp.where(row_x == 0, h_prev, jnp.where(row_x == 1, h_next, 0.0))
    hbuf[tb:, :] = jnp.concatenate([halo, jnp.zeros_like(halo)], axis=0).astype(BF16)

    row_u = lax.broadcasted_iota(jnp.int32, (SUBLANES, 2 * sub), 0)

    def up(sc):
        cols = slice(2 * sub * sc, 2 * sub * (sc + 1))
        slot = sc % nring
        res = _dot(hbuf[...], wu_ref[:, cols])
        uvbuf[slot, SUBLANES:SUBLANES + tb, :] = res[0:tb]
        uvbuf[slot, 0:SUBLANES, :] = jnp.where(
            row_u == 0, jnp.broadcast_to(res[tb:tb + 1], row_u.shape),
            pltpu.roll(res[tb - SUBLANES:tb], 1, 0))
        uvbuf[slot, SUBLANES + tb:, :] = jnp.where(
            row_u == SUBLANES - 1, jnp.broadcast_to(res[tb + 1:tb + 2], row_u.shape),
            pltpu.roll(res[0:SUBLANES], SUBLANES - 1, 0))

    def gate(sc):
        cols = slice(2 * sub * sc, 2 * sub * (sc + 1))
        slot = sc % nring
        c = cb_ref[:, cols] + uvbuf[slot, 0:tb, :] * cw_ref[0:1, cols]
        c = c + uvbuf[slot, SUBLANES:SUBLANES + tb, :] * cw_ref[1:2, cols]
        c = c + uvbuf[slot, 2 * SUBLANES:2 * SUBLANES + tb, :] * cw_ref[2:3, cols]
        return _gelu_gate(c[:, :sub], c[:, sub:]).astype(BF16)

    up(0)
    up(1)
    for sc in range(nsub):
        if sc + 2 < nsub:
            up(sc + 2)
        acts[:, sub * sc:sub * (sc + 1)] = gate(sc)
    acc[...] = _dot(acts[...], wd_ref[...])

    for j in range(seg):
        for k in range(nslab):
            slabs[k, pl.ds(j, SUBLANES, stride=pitch), :] = acc[SUBLANES * j:SUBLANES * (j + 1),
                                                                LANES * k:LANES * (k + 1)]
    ffn = jnp.concatenate(
        [jnp.concatenate([slabs[k, pitch * s:pitch * s + seg, :] for s in range(SUBLANES)], axis=0)
         for k in range(nslab)], axis=1)
    y = (x_ref[...] + d_ref[...]) + ffn
    if final:
        y = _rmsnorm(y, fg_ref[...])
    y_ref[...] = y


def _ffn(x2, d2, g, w_up, cw, cb, w_down, fg, *, S, final):
    T = x2.shape[0]
    tb = TB_FFN
    nblk = S // tb
    hpb = tb // SUBLANES
    n_halo = T // SUBLANES
    full = lambda shape: pl.BlockSpec(shape, lambda i: (0,) * len(shape))
    tile = pl.BlockSpec((tb, D_MODEL), lambda i: (i, 0))
    halo_prev = pl.BlockSpec((SUBLANES, D_MODEL), lambda i: (jnp.maximum(i * hpb - 1, 0), 0))
    halo_next = pl.BlockSpec((SUBLANES, D_MODEL), lambda i: (jnp.minimum((i + 1) * hpb, n_halo - 1), 0))
    kern = functools.partial(_ffn_kernel, nblk=nblk, tb=tb, final=final)
    return pl.pallas_call(
        kern,
        grid=(T // tb,),
        in_specs=[
            tile, halo_prev, halo_next, tile, halo_prev, halo_next,
            full((1, D_MODEL)), full((D_MODEL, 2 * D_FF)), full((3, 2 * D_FF)), full((1, 2 * D_FF)),
            full((D_FF, D_MODEL)), full((1, D_MODEL)),
        ],
        out_specs=pl.BlockSpec((tb, D_MODEL), lambda i: (i, 0)),
        out_shape=jax.ShapeDtypeStruct((T, D_MODEL), F32),
        scratch_shapes=[
            pltpu.VMEM((D_MODEL // LANES, tb + SUBLANES * SUBLANES, LANES), F32),
            pltpu.VMEM((tb + BF16_ROWS, D_MODEL), BF16),
            pltpu.VMEM((FF_RING, tb + 2 * SUBLANES, 2 * FF_SUB), F32),
            pltpu.VMEM((tb, D_FF), BF16),
            pltpu.VMEM((tb, D_MODEL), F32),
        ],
        compiler_params=pltpu.CompilerParams(dimension_semantics=("parallel",),
                                             vmem_limit_bytes=VMEM_LIMIT_FFN),
        name="convffn",
    )(x2, x2, x2, d2, d2, d2, g, w_up, cw, cb, w_down, fg)


def _pair_columns(w):
    lead = w.shape[:-1]
    n = D_FF // FF_SUB
    return jnp.swapaxes(w.reshape(*lead, 2, n, FF_SUB), -3, -2).reshape(*lead, 2 * D_FF)


def _block_diag(w):
    eye = jnp.eye(RG_BLOCKS, dtype=w.dtype)
    return jnp.einsum('ncd,nm->ncmd', w, eye).reshape(D_RG, D_RG)


def _encoder(x, norm1_g, w_in, b_gates, rg_conv_w, rg_conv_b, rg_wa, rg_ba, rg_wx, rg_bx, rg_lambda,
             ml_norm_g, w_out, norm2_g, w_up, ffn_conv_w, ffn_conv_b, w_down, final_g):
    B, S, _ = x.shape
    T = B * S
    depth = w_in.shape[0]
    x2 = x.reshape(T, D_MODEL)
    row = lambda v: v.reshape(1, -1).astype(F32)
    n_nat = 2 * D_RG + 2 * D_ML
    for l in range(depth):
        w_nat = w_in[l, :, :n_nat].astype(BF16)
        w_tr = w_in[l, :, n_nat:n_nat + 2 * D_ML].T.astype(BF16)
        w_gate = w_in[l, :, n_nat + 2 * D_ML:].T.astype(BF16)
        bias = jnp.broadcast_to(b_gates[l].astype(F32).reshape(N_GATE, 1), (N_GATE, ML_CHUNK))
        rx, rgate, qk, v_t, o_t, gr, uc = _inproj(x2, row(norm1_g[l]), w_nat, w_tr, w_gate, bias,
                                                  B=B, S=S)
        qk3 = qk.reshape(B, S, 2 * D_ML)
        gr4 = gr.reshape(B, S // ML_CHUNK, 2 * G_ROWS, ML_CHUNK)
        uc3 = uc.reshape(B, S, LANES)
        r_dir, m_dir = [], []
        for d, reverse in enumerate((False, True)):
            r_dir.append(_rglru(
                rx, rg_conv_w[l].astype(F32), row(rg_conv_b[l]),
                _block_diag(rg_wa[l, d]).astype(BF16), _block_diag(rg_wx[l, d]).astype(BF16),
                row(rg_ba[l, d]), row(rg_bx[l, d]), row(rg_lambda[l, d]), B=B, S=S, reverse=reverse))
            m_dir.append(_mlstm(qk3, v_t, gr4, uc3, reverse=reverse))
        wo = w_out[l].astype(BF16)
        mg_tile = jnp.broadcast_to(ml_norm_g[l].astype(F32).reshape(D_ML, 1), (D_ML, LANES))
        mixed = _outproj(r_dir[0], r_dir[1], rgate, m_dir[0], m_dir[1], o_t, mg_tile,
                         wo[:D_RG], wo[D_RG:], S=S)
        x2 = _ffn(x2, mixed, row(norm2_g[l]), _pair_columns(w_up[l].astype(BF16)),
                  _pair_columns(ffn_conv_w[l].astype(F32)), _pair_columns(row(ffn_conv_b[l])),
                  w_down[l].astype(BF16), row(final_g), S=S,
                  final=(l == depth - 1))
    return x2.reshape(B, S, D_MODEL)


def kernel(x_prompt, x_sample, norm1_g, w_in, b_gates, rg_conv_w, rg_conv_b, rg_wa, rg_ba, rg_wx, rg_bx,
           rg_lambda, ml_norm_g, w_out, norm2_g, w_up, ffn_conv_w, ffn_conv_b, w_down, final_g):
    weights = (norm1_g, w_in, b_gates, rg_conv_w, rg_conv_b, rg_wa, rg_ba, rg_wx, rg_bx, rg_lambda,
               ml_norm_g, w_out, norm2_g, w_up, ffn_conv_w, ffn_conv_b, w_down, final_g)
    return (_encoder(x_prompt, *weights), _encoder(x_sample, *weights))
```

```python
import functools

import jax
import jax.numpy as jnp
from jax import lax
from jax.experimental import pallas as pl
from jax.experimental.pallas import tpu as pltpu

F32 = jnp.float32
BF16 = jnp.bfloat16

D_MODEL = 1024
D_RG = 512
D_ML = 512
RG_BLOCKS = 8
RG_C = 8.0
ML_HEADS = 4
ML_HD = 128
ML_CHUNK = 128
D_FF = 3072
EPS = 1e-6
N_GATE = 4 * ML_HEADS

SUBLANES = 8
LANES = 128
BF16_ROWS = 16
VMEM_LIMIT = 48 * 1024 * 1024

TM_PROJ = 512
TB_RG = 512
ML_SEQS = 8
TB_FFN = 512
FF_SUB = 256
FF_RING = 4
VMEM_LIMIT_FFN = 56 * 1024 * 1024

G_B, G_U, G_A, G_BL, G_GM = 0, 8, 16, 24, 32
G_ROWS = 40
UC_DIR = 24
UC_ONES = 2 * UC_DIR


def _params(*sem):
    return pltpu.CompilerParams(dimension_semantics=sem, vmem_limit_bytes=VMEM_LIMIT)


def _softplus(z):
    return jnp.maximum(z, 0.0) + jnp.log1p(jnp.exp(-jnp.abs(z)))


def _gelu_gate(gate, val):
    k0 = -2.0 * 0.7978845608028654 * 1.4426950408889634
    z = gate * (k0 + (k0 * 0.044715) * (gate * gate))
    return (gate * val) / (1.0 + jnp.exp2(z))


def _rmsnorm(x, g):
    return x * lax.rsqrt(jnp.mean(x * x, axis=-1, keepdims=True) + EPS) * g


def _dot(a, b):
    return jnp.dot(a, b, preferred_element_type=F32)


def _dot_nt(a, b):
    return lax.dot_general(a, b, (((1,), (1,)), ((), ())), preferred_element_type=F32)


def _split3(x):
    hi = x.astype(BF16).astype(F32)
    r1 = x - hi
    mid = r1.astype(BF16).astype(F32)
    return hi, mid, (r1 - mid).astype(BF16).astype(F32)


def _lane_scan(x, op, fill, reverse):
    n = x.shape[-1]
    lane = lax.broadcasted_iota(jnp.int32, x.shape, 1)
    s = 1
    while s < n:
        if reverse:
            x = op(x, jnp.where(lane < n - s, pltpu.roll(x, n - s, 1), fill))
        else:
            x = op(x, jnp.where(lane >= s, pltpu.roll(x, s, 1), fill))
        s *= 2
    return x


def _inproj_kernel(x_ref, g_ref, wn_ref, wt_ref, wg_ref, bias_ref, rx_ref, rg_ref, qk_ref, vt_ref,
                   ot_ref, gr_ref, uc_ref):
    L = ML_CHUNK
    NH = ML_HEADS
    h = _rmsnorm(x_ref[...], g_ref[...]).astype(BF16)

    gt = _dot_nt(wg_ref[...], h)
    rowid = lax.broadcasted_iota(jnp.int32, (2 * NH, L), 0)
    head_row = rowid < NH
    rep = lambda col: jnp.broadcast_to(col, (2 * NH, L))
    zeros8 = jnp.zeros((2 * NH, L), F32)
    for c in range(gr_ref.shape[0]):
        g16 = gt[:, c * L:(c + 1) * L] + bias_ref[...]
        tiles = []
        for d, reverse in enumerate((False, True)):
            gates = g16[2 * NH * d:2 * NH * (d + 1)]
            lf = jnp.where(head_row, 0.0, -_softplus(-gates))
            bcum = pltpu.roll(_lane_scan(lf, jnp.add, 0.0, reverse), NH, 0)
            u = jnp.where(head_row, gates - bcum, 0.0)
            last = 0 if reverse else L - 1
            bl = rep(bcum[:, last:last + 1])
            base = G_ROWS * d
            gr_ref[c, base + G_B:base + G_B + 8, :] = bcum
            gr_ref[c, base + G_U:base + G_U + 8, :] = u
            gr_ref[c, base + G_A:base + G_A + 8, :] = bcum + _lane_scan(u, jnp.maximum, -jnp.inf, reverse)
            gr_ref[c, base + G_BL:base + G_BL + 8, :] = bl
            gr_ref[c, base + G_GM:base + G_GM + 8, :] = rep(jnp.max(bl + u, axis=-1, keepdims=True))
            tiles.extend(_split3(u))
        tiles.append(jnp.ones((2 * NH, L), F32))
        tiles.extend([zeros8] * (L // 8 - len(tiles)))
        uc_ref[c * L:(c + 1) * L, :] = jnp.concatenate(tiles, axis=0).T.astype(BF16)

    nat = _dot(h, wn_ref[...])
    rx_ref[...] = nat[:, :D_RG]
    rg_ref[...] = nat[:, D_RG:2 * D_RG].astype(BF16)
    qk_ref[...] = jnp.concatenate(
        [nat[:, 2 * D_RG:2 * D_RG + D_ML] * (ML_HD ** -0.5), nat[:, 2 * D_RG + D_ML:]], axis=-1).astype(BF16)
    tr = _dot_nt(wt_ref[...], h)
    vt_ref[...] = tr[:D_ML].astype(BF16)
    ot_ref[...] = tr[D_ML:].astype(BF16)


def _inproj(x2, g, w_nat, w_tr, w_gate, bias, *, B, S):
    T = B * S
    tm = TM_PROJ
    nb = S // tm
    full = lambda shape: pl.BlockSpec(shape, lambda i: (0,) * len(shape))
    tok = lambda width: pl.BlockSpec((tm, width), lambda i: (i, 0))
    seq_t = pl.BlockSpec((None, D_ML, tm), lambda i: (i // nb, 0, i % nb))
    return pl.pallas_call(
        _inproj_kernel,
        grid=(T // tm,),
        in_specs=[
            tok(D_MODEL), full((1, D_MODEL)), full(w_nat.shape), full(w_tr.shape), full(w_gate.shape),
            full((N_GATE, ML_CHUNK)),
        ],
        out_specs=[
            tok(D_RG), tok(D_RG), tok(2 * D_ML), seq_t, seq_t,
            pl.BlockSpec((tm // ML_CHUNK, 2 * G_ROWS, ML_CHUNK), lambda i: (i, 0, 0)),
            tok(LANES),
        ],
        out_shape=[
            jax.ShapeDtypeStruct((T, D_RG), F32),
            jax.ShapeDtypeStruct((T, D_RG), BF16),
            jax.ShapeDtypeStruct((T, 2 * D_ML), BF16),
            jax.ShapeDtypeStruct((B, D_ML, S), BF16),
            jax.ShapeDtypeStruct((B, D_ML, S), BF16),
            jax.ShapeDtypeStruct((T // ML_CHUNK, 2 * G_ROWS, ML_CHUNK), F32),
            jax.ShapeDtypeStruct((T, LANES), BF16),
        ],
        compiler_params=_params("parallel"),
        name="inproj",
    )(x2, g, w_nat, w_tr, w_gate, bias)


def _rglru_fwd_kernel(x_ref, prev_ref, next_ref, cw_ref, cb_ref, wa_ref, wx_ref, ba_ref, bx_ref,
                      lam_ref, h_ref, xc_ref, slabs, xbuf, pbuf, lbuf, carry, *, nblk, tb):
    @pl.when(pl.program_id(1) == 0)
    def _():
        carry[...] = jnp.zeros_like(carry)

    xc = _rglru_conv(x_ref, prev_ref, next_ref, cw_ref, cb_ref, slabs, xbuf,
                     blk=pl.program_id(1), nblk=nblk, tb=tb)
    xc_ref[...] = xc
    _rglru_scan(xc, wa_ref, wx_ref, ba_ref, bx_ref, lam_ref, h_ref, slabs, pbuf, lbuf, carry,
                reverse=False, tb=tb)


def _rglru_bwd_kernel(xc_ref, wa_ref, wx_ref, ba_ref, bx_ref, lam_ref, h_ref, slabs, pbuf, lbuf,
                      carry, *, tb):
    @pl.when(pl.program_id(1) == 0)
    def _():
        carry[...] = jnp.zeros_like(carry)

    _rglru_scan(xc_ref[...], wa_ref, wx_ref, ba_ref, bx_ref, lam_ref, h_ref, slabs, pbuf, lbuf, carry,
                reverse=True, tb=tb)


def _rglru_conv(x_ref, prev_ref, next_ref, cw_ref, cb_ref, slabs, xbuf, *, blk, nblk, tb):
    nslab = D_RG // LANES
    seg = tb // SUBLANES
    pitch = seg + SUBLANES
    X0 = 2 * SUBLANES

    x = x_ref[...]
    for k in range(nslab):
        for s in range(SUBLANES):
            slabs[k, pitch * s:pitch * s + seg, :] = x[seg * s:seg * (s + 1), LANES * k:LANES * (k + 1)]
    seam = {}
    for j in range(seg):
        rows = jnp.concatenate(
            [slabs[k, pl.ds(j, SUBLANES, stride=pitch), :] for k in range(nslab)], axis=1)
        xbuf[X0 + SUBLANES * j:X0 + SUBLANES * (j + 1), :] = rows
        if j in (0, seg - 2, seg - 1):
            seam[j] = rows
    row = lax.broadcasted_iota(jnp.int32, (SUBLANES, D_RG), 0)
    tile_row = lambda v, i: jnp.broadcast_to(v[i:i + 1, :], (SUBLANES, D_RG))
    prev = jnp.where(blk == 0, 0.0, prev_ref[...])
    nxt = jnp.where(blk == nblk - 1, 0.0, next_ref[...])
    xbuf[0:SUBLANES, :] = jnp.where(row == 0, tile_row(prev, SUBLANES - 2), pltpu.roll(seam[seg - 2], 1, 0))
    xbuf[SUBLANES:X0, :] = jnp.where(row == 0, tile_row(prev, SUBLANES - 1), pltpu.roll(seam[seg - 1], 1, 0))
    xbuf[X0 + tb:, :] = jnp.where(row == SUBLANES - 1, tile_row(nxt, 0),
                                  pltpu.roll(seam[0], SUBLANES - 1, 0))
    xc = cb_ref[...] + xbuf[0:tb, :] * cw_ref[0:1, :]
    xc = xc + xbuf[SUBLANES:SUBLANES + tb, :] * cw_ref[1:2, :]
    xc = xc + xbuf[X0:X0 + tb, :] * cw_ref[2:3, :]
    return xc + xbuf[X0 + SUBLANES:X0 + SUBLANES + tb, :] * cw_ref[3:4, :]


def _rglru_scan(xc, wa_ref, wx_ref, ba_ref, bx_ref, lam_ref, h_ref, slabs, pbuf, lbuf, carry, *,
                reverse, tb):
    nslab = D_RG // LANES
    seg = tb // SUBLANES
    pitch = seg + SUBLANES
    row = lax.broadcasted_iota(jnp.int32, (SUBLANES, D_RG), 0)
    tile_row = lambda v, i: jnp.broadcast_to(v[i:i + 1, :], (SUBLANES, D_RG))

    xcb = xc.astype(BF16)
    r = jax.nn.sigmoid(_dot(xcb, wa_ref[...]) + ba_ref[...])
    i = jax.nn.sigmoid(_dot(xcb, wx_ref[...]) + bx_ref[...])
    decay_rate = RG_C * _softplus(-lam_ref[...])
    a = jnp.exp2((decay_rate * -1.4426950408889634) * r)
    y = jnp.tanh(decay_rate * r) * (a * a + 1.0)
    u = jnp.where(y > 0.0, y * lax.rsqrt(y), 0.0) * (i * xc)

    order = range(seg - 1, -1, -1) if reverse else range(seg)
    P = L = None
    for j in order:
        rs = slice(SUBLANES * j, SUBLANES * (j + 1))
        if P is None:
            P, L = a[rs], u[rs]
        else:
            P, L = a[rs] * P, a[rs] * L + u[rs]
        pbuf[rs, :] = P
        lbuf[rs, :] = L

    A, U = P, L
    for s in (1, 2, 4):
        if reverse:
            keep = row < SUBLANES - s
            shift = SUBLANES - s
        else:
            keep = row >= s
            shift = s
        a_sh = jnp.where(keep, pltpu.roll(A, shift, 0), 1.0)
        u_sh = jnp.where(keep, pltpu.roll(U, shift, 0), 0.0)
        U = A * u_sh + U
        A = A * a_sh
    c_in = carry[...]
    e = U + A * c_in
    if reverse:
        c_seg = jnp.where(row == SUBLANES - 1, c_in, pltpu.roll(e, SUBLANES - 1, 0))
        carry[...] = tile_row(e, 0)
    else:
        c_seg = jnp.where(row == 0, c_in, pltpu.roll(e, 1, 0))
        carry[...] = tile_row(e, SUBLANES - 1)

    for j in range(seg):
        rs = slice(SUBLANES * j, SUBLANES * (j + 1))
        hj = lbuf[rs, :] + pbuf[rs, :] * c_seg
        for k in range(nslab):
            slabs[k, pl.ds(j, SUBLANES, stride=pitch), :] = hj[:, LANES * k:LANES * (k + 1)]
    for k in range(nslab):
        for s in range(SUBLANES):
            h_ref[seg * s:seg * (s + 1), LANES * k:LANES * (k + 1)] = (
                slabs[k, pitch * s:pitch * s + seg, :].astype(h_ref.dtype))


def _rglru(x, cw, cb, gates_fwd, gates_bwd, *, B, S):
    T = B * S
    tb = TB_RG
    nblk = S // tb
    hb = tb // SUBLANES
    n_halo = T // SUBLANES
    full = lambda shape: pl.BlockSpec(shape, lambda b, j: (0,) * len(shape))
    gate_specs = [full((D_RG, D_RG)), full((D_RG, D_RG)), full((1, D_RG)), full((1, D_RG)), full((1, D_RG))]
    slabs = pltpu.VMEM((D_RG // LANES, tb + SUBLANES * SUBLANES, LANES), F32)
    scan_scratch = [
        pltpu.VMEM((tb, D_RG), F32),
        pltpu.VMEM((tb, D_RG), F32),
        pltpu.VMEM((SUBLANES, D_RG), F32),
    ]
    tile_fwd = pl.BlockSpec((tb, D_RG), lambda b, j: (b * nblk + j, 0))
    tile_bwd = pl.BlockSpec((tb, D_RG), lambda b, j: (b * nblk + nblk - 1 - j, 0))

    h_fwd, xc = pl.pallas_call(
        functools.partial(_rglru_fwd_kernel, nblk=nblk, tb=tb),
        grid=(B, nblk),
        in_specs=[
            tile_fwd,
            pl.BlockSpec((SUBLANES, D_RG), lambda b, j: (jnp.maximum((b * nblk + j) * hb - 1, 0), 0)),
            pl.BlockSpec((SUBLANES, D_RG), lambda b, j: (jnp.minimum((b * nblk + j + 1) * hb, n_halo - 1), 0)),
            full((4, D_RG)), full((1, D_RG)), *gate_specs,
        ],
        out_specs=[tile_fwd, tile_fwd],
        out_shape=[jax.ShapeDtypeStruct((T, D_RG), BF16),
                   jax.ShapeDtypeStruct((T, D_RG), F32)],
        scratch_shapes=[slabs, pltpu.VMEM((tb + 3 * SUBLANES, D_RG), F32), *scan_scratch],
        compiler_params=_params("parallel", "arbitrary"),
        name="rglru_fwd",
    )(x, x, x, cw, cb, *gates_fwd)
    h_bwd = pl.pallas_call(
        functools.partial(_rglru_bwd_kernel, tb=tb),
        grid=(B, nblk),
        in_specs=[tile_bwd, *gate_specs],
        out_specs=tile_bwd,
        out_shape=jax.ShapeDtypeStruct((T, D_RG), BF16),
        scratch_shapes=[slabs, *scan_scratch],
        compiler_params=_params("parallel", "arbitrary"),
        name="rglru_bwd",
    )(xc, *gates_bwd)
    return h_fwd, h_bwd


def _mlstm_kernel(q_ref, k_ref, vt_ref, gr_ref, uc_ref, h_ref, ct_st, n_st, m_st, *, reverse, bb):
    L = ML_CHUNK
    NH = ML_HEADS
    d = 1 if reverse else 0

    @pl.when(pl.program_id(1) == 0)
    def _():
        ct_st[...] = jnp.zeros_like(ct_st)
        n_st[...] = jnp.zeros_like(n_st)
        m_st[...] = jnp.zeros_like(m_st)

    row8 = lax.broadcasted_iota(jnp.int32, (SUBLANES, L), 0)
    s_id = lax.broadcasted_iota(jnp.int32, (L, L), 0)
    t_id = lax.broadcasted_iota(jnp.int32, (L, L), 1)
    valid = (s_id >= t_id) if reverse else (s_id <= t_id)
    zeros8 = jnp.zeros((SUBLANES, L), F32)
    tile = lambda rows, hd: jnp.broadcast_to(rows[hd:hd + 1, :], (L, L))
    row_of = lambda rows, hd: jnp.broadcast_to(rows[hd:hd + 1, :], (SUBLANES, L))
    pairs = [(b, hp) for b in range(bb) for hp in range(NH // 2)]

    seq = []
    for b in range(bb):
        base = G_ROWS * d
        bcum = gr_ref[b, base + G_B:base + G_B + 8, :]
        u = gr_ref[b, base + G_U:base + G_U + 8, :]
        b_last = gr_ref[b, base + G_BL:base + G_BL + 8, :]
        m_prev = m_st[b]
        inter = bcum + m_prev
        m_t = jnp.maximum(inter, gr_ref[b, base + G_A:base + G_A + 8, :])
        m_new = jnp.maximum(b_last + m_prev, gr_ref[b, base + G_GM:base + G_GM + 8, :])
        n_prev = n_st[b]
        wg = jnp.exp(u + (b_last - m_new))
        seq.append(dict(
            w_int=jnp.exp(inter - m_t), e_neg=jnp.exp(-m_t), v3=_split3(bcum - m_t),
            decay=jnp.exp(b_last + m_prev - m_new), m_new=m_new, n_prev=n_prev, wg=wg,
            n_lhs=jnp.concatenate([n_prev, zeros8], axis=0).astype(BF16),
            wg_lhs=jnp.concatenate([wg, zeros8], axis=0).astype(BF16)))

    cs = lambda hd: slice(hd * ML_HD, (hd + 1) * ML_HD)
    side = lambda i: slice(i * L, (i + 1) * L)
    twice = lambda x: jnp.concatenate([x, x], axis=1)

    def blockdiag(a0, a1):
        z = jnp.zeros_like(a0)
        return jnp.concatenate([jnp.concatenate([a0, z], axis=1), jnp.concatenate([z, a1], axis=1)], axis=0)

    def expo_rhs(b, hd):
        onehot = jnp.where(row8 == hd, 1.0, 0.0)
        v_hi, v_mid, v_lo = (row_of(x, hd) for x in seq[b]["v3"])
        v_rows = jnp.where(row8 == 0, v_hi, jnp.where(row8 == 1, v_mid, jnp.where(row8 == 2, v_lo, 0.0)))
        slabs = [zeros8] * (L // SUBLANES)
        for j in range(3):
            slabs[(UC_DIR * d) // SUBLANES + j] = onehot
        slabs[UC_ONES // SUBLANES] = v_rows
        return jnp.concatenate(slabs, axis=0).astype(BF16)

    st, expo, q_diag = {}, {}, {}
    for b, hp in pairs:
        h0, h1 = 2 * hp, 2 * hp + 1
        q_diag[b, hp] = blockdiag(q_ref[b, :, cs(h0)], q_ref[b, :, cs(h1)])
        st[b, hp] = _dot_nt(
            jnp.concatenate([k_ref[b, :, h0 * ML_HD:(h1 + 1) * ML_HD], twice(seq[b]["n_lhs"])], axis=0),
            q_diag[b, hp])
        expo[b, hp] = _dot(uc_ref[b], jnp.concatenate([expo_rhs(b, h0), expo_rhs(b, h1)], axis=1))

    for b, hp in pairs:
        sq = seq[b]
        heads = (2 * hp, 2 * hp + 1)
        p_t, rw = [], []
        for i, hd in enumerate(heads):
            s_t = st[b, hp][0:L, side(i)] * jnp.exp(jnp.where(valid, expo[b, hp][:, side(i)], -jnp.inf))
            w_h = sq["w_int"][hd:hd + 1, :]
            den = jnp.sum(s_t, axis=0, keepdims=True) + w_h * st[b, hp][L + hd:L + hd + 1, side(i)]
            r = 1.0 / jnp.maximum(jnp.abs(den), sq["e_neg"][hd:hd + 1, :])
            p_t.append((s_t * r).astype(BF16))
            rw.append(r * w_h)
        intra = _dot(jnp.concatenate([vt_ref[b, cs(hd), :] for hd in heads], axis=1), blockdiag(*p_t))
        inter = _dot_nt(jnp.concatenate([ct_st[b, hd].astype(BF16) for hd in heads], axis=1), q_diag[b, hp])
        out = (intra + inter * jnp.concatenate(rw, axis=1)).astype(h_ref.dtype)
        for i, hd in enumerate(heads):
            h_ref[b, cs(hd), :] = out[:, side(i)]

    for b, hp in pairs:
        sq = seq[b]
        heads = (2 * hp, 2 * hp + 1)
        vw = jnp.concatenate([(vt_ref[b, cs(hd), :].astype(F32) * sq["wg"][hd:hd + 1, :]).astype(BF16)
                              for hd in heads], axis=1)
        upd = _dot(jnp.concatenate([vw, twice(sq["wg_lhs"])], axis=0),
                   blockdiag(k_ref[b, :, cs(heads[0])], k_ref[b, :, cs(heads[1])]))
        for i, hd in enumerate(heads):
            ct_st[b, hd] = tile(sq["decay"], hd) * ct_st[b, hd] + upd[0:L, side(i)]
            n_st[b, hd:hd + 1, :] = (sq["decay"][hd:hd + 1, :] * sq["n_prev"][hd:hd + 1, :]
                                     + upd[L + hd:L + hd + 1, side(i)])
    for b in range(bb):
        m_st[b] = seq[b]["m_new"]


def _mlstm(qk3, vt, gr4, uc3, *, reverse):
    B, S, _ = qk3.shape
    L = ML_CHUNK
    nc = S // L
    bb = ML_SEQS

    def chunk(c):
        return (nc - 1 - c) if reverse else c

    kern = functools.partial(_mlstm_kernel, reverse=reverse, bb=bb)
    return pl.pallas_call(
        kern,
        grid=(B // bb, nc),
        in_specs=[
            pl.BlockSpec((bb, L, D_ML), lambda b, c: (b, chunk(c), 0)),
            pl.BlockSpec((bb, L, D_ML), lambda b, c: (b, chunk(c), 1)),
            pl.BlockSpec((bb, D_ML, L), lambda b, c: (b, 0, chunk(c))),
            pl.BlockSpec((bb, None, 2 * G_ROWS, L), lambda b, c: (b, chunk(c), 0, 0)),
            pl.BlockSpec((bb, L, LANES), lambda b, c: (b, chunk(c), 0)),
        ],
        out_specs=pl.BlockSpec((bb, D_ML, L), lambda b, c: (b, 0, chunk(c))),
        out_shape=jax.ShapeDtypeStruct((B, D_ML, S), BF16),
        scratch_shapes=[
            pltpu.VMEM((bb, ML_HEADS, ML_HD, ML_HD), F32),
            pltpu.VMEM((bb, 2 * ML_HEADS, ML_HD), F32),
            pltpu.VMEM((bb, 2 * ML_HEADS, LANES), F32),
        ],
        compiler_params=_params("parallel", "arbitrary"),
        name="mlstm_bwd" if reverse else "mlstm_fwd",
    )(qk3, qk3, vt, gr4, uc3)


def _outproj_kernel(rf_ref, rb_ref, gate_ref, mf_ref, mb_ref, ot_ref, mg_ref, wr_ref, wm_ref, y_ref):
    y_rg = _gelu_gate(gate_ref[...].astype(F32), rf_ref[...].astype(F32) + rb_ref[...].astype(F32))
    acc = _dot(y_rg.astype(BF16), wr_ref[...])
    h_t = mf_ref[...].astype(F32) + mb_ref[...].astype(F32)
    parts = []
    for hd in range(ML_HEADS):
        hh = h_t[hd * ML_HD:(hd + 1) * ML_HD]
        parts.append(hh * lax.rsqrt(jnp.mean(hh * hh, axis=0, keepdims=True) + EPS))
    mg = jnp.tile(mg_ref[...], (1, h_t.shape[1] // LANES))
    y_t = jax.nn.sigmoid(ot_ref[...].astype(F32)) * (jnp.concatenate(parts, axis=0) * mg)
    y_ref[...] = acc + _dot(y_t.T.astype(BF16), wm_ref[...])


def _outproj(rf, rb, pa, mf_t, mb_t, o_t, mg_tile, w_rg, w_ml, *, S):
    T = rf.shape[0]
    tm = TM_PROJ
    nb = S // tm
    tok = lambda width, col: pl.BlockSpec((tm, width), lambda i: (i, col))
    seq_t = pl.BlockSpec((None, D_ML, tm), lambda i: (i // nb, 0, i % nb))
    full = lambda shape: pl.BlockSpec(shape, lambda i: (0,) * len(shape))
    return pl.pallas_call(
        _outproj_kernel,
        grid=(T // tm,),
        in_specs=[
            tok(D_RG, 0), tok(D_RG, 0), tok(D_RG, 0),
            seq_t, seq_t, seq_t,
            full((D_ML, LANES)), full((D_RG, D_MODEL)), full((D_ML, D_MODEL)),
        ],
        out_specs=tok(D_MODEL, 0),
        out_shape=jax.ShapeDtypeStruct((T, D_MODEL), F32),
        compiler_params=_params("parallel"),
        name="outproj",
    )(rf, rb, pa, mf_t, mb_t, o_t, mg_tile, w_rg, w_ml)


def _ffn_kernel(x_ref, prev_ref, next_ref, d_ref, dprev_ref, dnext_ref, g_ref, wu_ref, cw_ref, cb_ref,
                wd_ref, fg_ref, y_ref, slabs, hbuf, uvbuf, acts, *, nblk, tb, final):
    blk = pl.program_id(0) % nblk
    sub = FF_SUB
    nsub = D_FF // sub
    nring = uvbuf.shape[0]
    nslab = D_MODEL // LANES
    seg = tb // SUBLANES
    pitch = seg + SUBLANES

    g = g_ref[...]
    hn = _rmsnorm(x_ref[...] + d_ref[...], g)
    for k in range(nslab):
        for s in range(SUBLANES):
            slabs[k, pitch * s:pitch * s + seg, :] = hn[seg * s:seg * (s + 1), LANES * k:LANES * (k + 1)]

    def perm_rows(j):
        return jnp.concatenate(
            [slabs[k, pl.ds(j, SUBLANES, stride=pitch), :] for k in range(nslab)], axis=1)

    for jj in range(seg // 2):
        hbuf[BF16_ROWS * jj:BF16_ROWS * (jj + 1), :] = jnp.concatenate(
            [perm_rows(2 * jj), perm_rows(2 * jj + 1)], axis=0).astype(BF16)
    row_x = lax.broadcasted_iota(jnp.int32, (SUBLANES, D_MODEL), 0)
    h_prev = jnp.where(blk == 0, 0.0, pltpu.roll(_rmsnorm(prev_ref[...] + dprev_ref[...], g), 1, 0))
    h_next = jnp.where(blk == nblk - 1, 0.0,
                       pltpu.roll(_rmsnorm(next_ref[...] + dnext_ref[...], g), 1, 0))
    halo = jnp.where(row_x == 0, h_prev, jnp.where(row_x == 1, h_next, 0.0))
    hbuf[tb:, :] = jnp.concatenate([halo, jnp.zeros_like(halo)], axis=0).astype(BF16)

    row_u = lax.broadcasted_iota(jnp.int32, (SUBLANES, 2 * sub), 0)

    def pair(ref, sc):
        return jnp.concatenate([ref[:, sub * sc:sub * (sc + 1)],
                                ref[:, D_FF + sub * sc:D_FF + sub * (sc + 1)]], axis=1)

    def up(sc):
        slot = sc % nring
        res = jnp.concatenate([_dot(hbuf[...], wu_ref[:, sub * sc:sub * (sc + 1)]),
                               _dot(hbuf[...], wu_ref[:, D_FF + sub * sc:D_FF + sub * (sc + 1)])], axis=1)
        uvbuf[slot, SUBLANES:SUBLANES + tb, :] = res[0:tb]
        uvbuf[slot, 0:SUBLANES, :] = jnp.where(
            row_u == 0, jnp.broadcast_to(res[tb:tb + 1], row_u.shape),
            pltpu.roll(res[tb - SUBLANES:tb], 1, 0))
        uvbuf[slot, SUBLANES + tb:, :] = jnp.where(
            row_u == SUBLANES - 1, jnp.broadcast_to(res[tb + 1:tb + 2], row_u.shape),
            pltpu.roll(res[0:SUBLANES], SUBLANES - 1, 0))

    def gate(sc):
        slot = sc % nring
        cw = pair(cw_ref, sc)
        c = pair(cb_ref, sc) + uvbuf[slot, 0:tb, :] * cw[0:1]
        c = c + uvbuf[slot, SUBLANES:SUBLANES + tb, :] * cw[1:2]
        c = c + uvbuf[slot, 2 * SUBLANES:2 * SUBLANES + tb, :] * cw[2:3]
        return _gelu_gate(c[:, :sub], c[:, sub:]).astype(BF16)

    up(0)
    up(1)
    for sc in range(nsub):
        if sc + 2 < nsub:
            up(sc + 2)
        acts[:, sub * sc:sub * (sc + 1)] = gate(sc)

    out_w = 2 * LANES
    outs = []
    for n in range(D_MODEL // out_w):
        cols = slice(out_w * n, out_w * (n + 1))
        dn = _dot(acts[...], wd_ref[:, cols])
        ks = range(2 * n, 2 * n + 2)
        for j in range(seg):
            for k in ks:
                slabs[k, pl.ds(j, SUBLANES, stride=pitch), :] = dn[SUBLANES * j:SUBLANES * (j + 1),
                                                                   LANES * (k - 2 * n):LANES * (k - 2 * n + 1)]
        ffn = jnp.concatenate(
            [jnp.concatenate([slabs[k, pitch * s:pitch * s + seg, :] for s in range(SUBLANES)], axis=0)
             for k in ks], axis=1)
        yn = (x_ref[:, cols] + d_ref[:, cols]) + ffn
        if final:
            outs.append(yn)
        else:
            y_ref[:, cols] = yn
    if final:
        y_ref[...] = _rmsnorm(jnp.concatenate(outs, axis=1), fg_ref[...])


def _ffn(x2, d2, g, w_up, cw, cb, w_down, fg, *, S, final):
    T = x2.shape[0]
    tb = TB_FFN
    nblk = S // tb
    hpb = tb // SUBLANES
    n_halo = T // SUBLANES
    full = lambda shape: pl.BlockSpec(shape, lambda i: (0,) * len(shape))
    tile = pl.BlockSpec((tb, D_MODEL), lambda i: (i, 0))
    halo_prev = pl.BlockSpec((SUBLANES, D_MODEL), lambda i: (jnp.maximum(i * hpb - 1, 0), 0))
    halo_next = pl.BlockSpec((SUBLANES, D_MODEL), lambda i: (jnp.minimum((i + 1) * hpb, n_halo - 1), 0))
    kern = functools.partial(_ffn_kernel, nblk=nblk, tb=tb, final=final)
    return pl.pallas_call(
        kern,
        grid=(T // tb,),
        in_specs=[
            tile, halo_prev, halo_next, tile, halo_prev, halo_next,
            full((1, D_MODEL)), full((D_MODEL, 2 * D_FF)), full((3, 2 * D_FF)), full((1, 2 * D_FF)),
            full((D_FF, D_MODEL)), full((1, D_MODEL)),
        ],
        out_specs=pl.BlockSpec((tb, D_MODEL), lambda i: (i, 0)),
        out_shape=jax.ShapeDtypeStruct((T, D_MODEL), F32),
        scratch_shapes=[
            pltpu.VMEM((D_MODEL // LANES, tb + SUBLANES * SUBLANES, LANES), F32),
            pltpu.VMEM((tb + BF16_ROWS, D_MODEL), BF16),
            pltpu.VMEM((FF_RING, tb + 2 * SUBLANES, 2 * FF_SUB), F32),
            pltpu.VMEM((tb, D_FF), BF16),
        ],
        compiler_params=pltpu.CompilerParams(dimension_semantics=("parallel",),
                                             vmem_limit_bytes=VMEM_LIMIT_FFN),
        name="convffn",
    )(x2, x2, x2, d2, d2, d2, g, w_up, cw, cb, w_down, fg)


def _block_diag(w):
    eye = jnp.eye(RG_BLOCKS, dtype=w.dtype)
    return jnp.einsum('ncd,nm->ncmd', w, eye).reshape(D_RG, D_RG)


def _encoder(x, norm1_g, w_in, b_gates, rg_conv_w, rg_conv_b, rg_wa, rg_ba, rg_wx, rg_bx, rg_lambda,
             ml_norm_g, w_out, norm2_g, w_up, ffn_conv_w, ffn_conv_b, w_down, final_g):
    B, S, _ = x.shape
    T = B * S
    depth = w_in.shape[0]
    x2 = x.reshape(T, D_MODEL)
    row = lambda v: v.reshape(1, -1).astype(F32)
    n_nat = 2 * D_RG + 2 * D_ML
    for l in range(depth):
        w_nat = w_in[l, :, :n_nat].astype(BF16)
        w_tr = w_in[l, :, n_nat:n_nat + 2 * D_ML].T.astype(BF16)
        w_gate = w_in[l, :, n_nat + 2 * D_ML:].T.astype(BF16)
        bias = jnp.broadcast_to(b_gates[l].astype(F32).reshape(N_GATE, 1), (N_GATE, ML_CHUNK))
        rx, rgate, qk, v_t, o_t, gr, uc = _inproj(x2, row(norm1_g[l]), w_nat, w_tr, w_gate, bias,
                                                  B=B, S=S)
        qk3 = qk.reshape(B, S, 2 * D_ML)
        gr4 = gr.reshape(B, S // ML_CHUNK, 2 * G_ROWS, ML_CHUNK)
        uc3 = uc.reshape(B, S, LANES)
        rg_gates = [(_block_diag(rg_wa[l, d]).astype(BF16), _block_diag(rg_wx[l, d]).astype(BF16),
                     row(rg_ba[l, d]), row(rg_bx[l, d]), row(rg_lambda[l, d])) for d in range(2)]
        r_dir = _rglru(rx, rg_conv_w[l].astype(F32), row(rg_conv_b[l]), *rg_gates, B=B, S=S)
        m_dir = [_mlstm(qk3, v_t, gr4, uc3, reverse=reverse) for reverse in (False, True)]
        wo = w_out[l].astype(BF16)
        mg_tile = jnp.broadcast_to(ml_norm_g[l].astype(F32).reshape(D_ML, 1), (D_ML, LANES))
        mixed = _outproj(r_dir[0], r_dir[1], rgate, m_dir[0], m_dir[1], o_t, mg_tile,
                         wo[:D_RG], wo[D_RG:], S=S)
        x2 = _ffn(x2, mixed, row(norm2_g[l]), w_up[l].astype(BF16),
                  ffn_conv_w[l].astype(F32), row(ffn_conv_b[l]),
                  w_down[l].astype(BF16), row(final_g), S=S,
                  final=(l == depth - 1))
    return x2.reshape(B, S, D_MODEL)


def kernel(x_prompt, x_sample, norm1_g, w_in, b_gates, rg_conv_w, rg_conv_b, rg_wa, rg_ba, rg_wx, rg_bx,
           rg_lambda, ml_norm_g, w_out, norm2_g, w_up, ffn_conv_w, ffn_conv_b, w_down, final_g):
    weights = (norm1_g, w_in, b_gates, rg_conv_w, rg_conv_b, rg_wa, rg_ba, rg_wx, rg_bx, rg_lambda,
               ml_norm_g, w_out, norm2_g, w_up, ffn_conv_w, ffn_conv_b, w_down, final_g)
    return (_encoder(x_prompt, *weights), _encoder(x_sample, *weights))
```

```python
import functools

import jax
import jax.numpy as jnp
from jax import lax
from jax.experimental import pallas as pl
from jax.experimental.pallas import tpu as pltpu

F32 = jnp.float32
BF16 = jnp.bfloat16

D_MODEL = 1024
D_RG = 512
D_ML = 512
RG_BLOCKS = 8
RG_C = 8.0
ML_HEADS = 4
ML_HD = 128
ML_CHUNK = 128
D_FF = 3072
EPS = 1e-6
N_GATE = 4 * ML_HEADS

SUBLANES = 8
LANES = 128
BF16_ROWS = 16
VMEM_LIMIT = 48 * 1024 * 1024

TM_PROJ = 1024
TM_SUB = 512
TB_RG = 512
ML_SEQS = 8
TB_FFN = 512
FF_SUB = 256
FF_RING = 4
VMEM_LIMIT_FFN = 56 * 1024 * 1024

G_B, G_U, G_A, G_BL, G_GM = 0, 8, 16, 24, 32
G_ROWS = 40
UC_DIR = 24
UC_ONES = 2 * UC_DIR


def _params(*sem):
    return pltpu.CompilerParams(dimension_semantics=sem, vmem_limit_bytes=VMEM_LIMIT)


def _softplus(z):
    return jnp.maximum(z, 0.0) + jnp.log1p(jnp.exp(-jnp.abs(z)))


def _gelu_gate(gate, val):
    k0 = -2.0 * 0.7978845608028654 * 1.4426950408889634
    z = gate * (k0 + (k0 * 0.044715) * (gate * gate))
    return (gate * val) / (1.0 + jnp.exp2(z))


def _rmsnorm(x, g):
    return x * lax.rsqrt(jnp.mean(x * x, axis=-1, keepdims=True) + EPS) * g


def _dot(a, b):
    return jnp.dot(a, b, preferred_element_type=F32)


def _dot_nt(a, b):
    return lax.dot_general(a, b, (((1,), (1,)), ((), ())), preferred_element_type=F32)


def _split3(x):
    hi = x.astype(BF16).astype(F32)
    r1 = x - hi
    mid = r1.astype(BF16).astype(F32)
    return hi, mid, (r1 - mid).astype(BF16).astype(F32)


def _lane_scan(x, op, fill, reverse):
    n = x.shape[-1]
    lane = lax.broadcasted_iota(jnp.int32, x.shape, 1)
    s = 1
    while s < n:
        if reverse:
            x = op(x, jnp.where(lane < n - s, pltpu.roll(x, n - s, 1), fill))
        else:
            x = op(x, jnp.where(lane >= s, pltpu.roll(x, s, 1), fill))
        s *= 2
    return x


def _inproj_kernel(x_ref, g_ref, wn_ref, wt_ref, wg_ref, bias_ref, rx_ref, rg_ref, qk_ref, vt_ref,
                   ot_ref, gr_ref, uc_ref):
    for r0 in range(0, x_ref.shape[0], TM_SUB):
        _inproj_piece(x_ref, g_ref, wn_ref, wt_ref, wg_ref, bias_ref, rx_ref, rg_ref, qk_ref, vt_ref,
                      ot_ref, gr_ref, uc_ref, r0)


def _inproj_piece(x_ref, g_ref, wn_ref, wt_ref, wg_ref, bias_ref, rx_ref, rg_ref, qk_ref, vt_ref,
                  ot_ref, gr_ref, uc_ref, r0):
    L = ML_CHUNK
    NH = ML_HEADS
    rows = slice(r0, r0 + TM_SUB)
    h = _rmsnorm(x_ref[rows, :], g_ref[...]).astype(BF16)

    gt = _dot_nt(wg_ref[...], h)
    rowid = lax.broadcasted_iota(jnp.int32, (2 * NH, L), 0)
    head_row = rowid < NH
    rep = lambda col: jnp.broadcast_to(col, (2 * NH, L))
    zeros8 = jnp.zeros((2 * NH, L), F32)
    for cl in range(TM_SUB // L):
        c = r0 // L + cl
        g16 = gt[:, cl * L:(cl + 1) * L] + bias_ref[...]
        tiles = []
        for d, reverse in enumerate((False, True)):
            gates = g16[2 * NH * d:2 * NH * (d + 1)]
            lf = jnp.where(head_row, 0.0, -_softplus(-gates))
            bcum = pltpu.roll(_lane_scan(lf, jnp.add, 0.0, reverse), NH, 0)
            u = jnp.where(head_row, gates - bcum, 0.0)
            last = 0 if reverse else L - 1
            bl = rep(bcum[:, last:last + 1])
            base = G_ROWS * d
            gr_ref[c, base + G_B:base + G_B + 8, :] = bcum
            gr_ref[c, base + G_U:base + G_U + 8, :] = u
            gr_ref[c, base + G_A:base + G_A + 8, :] = bcum + _lane_scan(u, jnp.maximum, -jnp.inf, reverse)
            gr_ref[c, base + G_BL:base + G_BL + 8, :] = bl
            gr_ref[c, base + G_GM:base + G_GM + 8, :] = rep(jnp.max(bl + u, axis=-1, keepdims=True))
            tiles.extend(_split3(u))
        tiles.append(jnp.ones((2 * NH, L), F32))
        tiles.extend([zeros8] * (L // 8 - len(tiles)))
        uc_ref[c * L:(c + 1) * L, :] = jnp.concatenate(tiles, axis=0).T.astype(BF16)

    nat = _dot(h, wn_ref[...])
    rx_ref[rows, :] = nat[:, :D_RG]
    rg_ref[rows, :] = nat[:, D_RG:2 * D_RG].astype(BF16)
    qk_ref[rows, :] = jnp.concatenate(
        [nat[:, 2 * D_RG:2 * D_RG + D_ML] * (ML_HD ** -0.5), nat[:, 2 * D_RG + D_ML:]], axis=-1).astype(BF16)
    tr = _dot_nt(wt_ref[...], h)
    vt_ref[:, rows] = tr[:D_ML].astype(BF16)
    ot_ref[:, rows] = tr[D_ML:].astype(BF16)


def _inproj(x2, g, w_nat, w_tr, w_gate, bias, *, B, S):
    T = B * S
    tm = TM_PROJ
    nb = S // tm
    full = lambda shape: pl.BlockSpec(shape, lambda i: (0,) * len(shape))
    tok = lambda width: pl.BlockSpec((tm, width), lambda i: (i, 0))
    seq_t = pl.BlockSpec((None, D_ML, tm), lambda i: (i // nb, 0, i % nb))
    return pl.pallas_call(
        _inproj_kernel,
        grid=(T // tm,),
        in_specs=[
            tok(D_MODEL), full((1, D_MODEL)), full(w_nat.shape), full(w_tr.shape), full(w_gate.shape),
            full((N_GATE, ML_CHUNK)),
        ],
        out_specs=[
            tok(D_RG), tok(D_RG), tok(2 * D_ML), seq_t, seq_t,
            pl.BlockSpec((tm // ML_CHUNK, 2 * G_ROWS, ML_CHUNK), lambda i: (i, 0, 0)),
            tok(LANES),
        ],
        out_shape=[
            jax.ShapeDtypeStruct((T, D_RG), F32),
            jax.ShapeDtypeStruct((T, D_RG), BF16),
            jax.ShapeDtypeStruct((T, 2 * D_ML), BF16),
            jax.ShapeDtypeStruct((B, D_ML, S), BF16),
            jax.ShapeDtypeStruct((B, D_ML, S), BF16),
            jax.ShapeDtypeStruct((T // ML_CHUNK, 2 * G_ROWS, ML_CHUNK), F32),
            jax.ShapeDtypeStruct((T, LANES), BF16),
        ],
        compiler_params=_params("parallel"),
        name="inproj",
    )(x2, g, w_nat, w_tr, w_gate, bias)


def _rglru_fwd_kernel(x_ref, prev_ref, next_ref, cw_ref, cb_ref, wa_ref, wx_ref, ba_ref, bx_ref,
                      lam_ref, h_ref, xc_ref, slabs, xbuf, pbuf, lbuf, carry, *, nblk, tb):
    @pl.when(pl.program_id(1) == 0)
    def _():
        carry[...] = jnp.zeros_like(carry)

    xc = _rglru_conv(x_ref, prev_ref, next_ref, cw_ref, cb_ref, slabs, xbuf,
                     blk=pl.program_id(1), nblk=nblk, tb=tb)
    xc_ref[...] = xc
    _rglru_scan(xc, wa_ref, wx_ref, ba_ref, bx_ref, lam_ref, h_ref, slabs, pbuf, lbuf, carry,
                reverse=False, tb=tb)


def _rglru_bwd_kernel(xc_ref, wa_ref, wx_ref, ba_ref, bx_ref, lam_ref, h_ref, slabs, pbuf, lbuf,
                      carry, *, tb):
    @pl.when(pl.program_id(1) == 0)
    def _():
        carry[...] = jnp.zeros_like(carry)

    _rglru_scan(xc_ref[...], wa_ref, wx_ref, ba_ref, bx_ref, lam_ref, h_ref, slabs, pbuf, lbuf, carry,
                reverse=True, tb=tb)


def _rglru_conv(x_ref, prev_ref, next_ref, cw_ref, cb_ref, slabs, xbuf, *, blk, nblk, tb):
    nslab = D_RG // LANES
    seg = tb // SUBLANES
    pitch = seg + SUBLANES
    X0 = 2 * SUBLANES

    x = x_ref[...]
    for k in range(nslab):
        for s in range(SUBLANES):
            slabs[k, pitch * s:pitch * s + seg, :] = x[seg * s:seg * (s + 1), LANES * k:LANES * (k + 1)]
    seam = {}
    for j in range(seg):
        rows = jnp.concatenate(
            [slabs[k, pl.ds(j, SUBLANES, stride=pitch), :] for k in range(nslab)], axis=1)
        xbuf[X0 + SUBLANES * j:X0 + SUBLANES * (j + 1), :] = rows
        if j in (0, seg - 2, seg - 1):
            seam[j] = rows
    row = lax.broadcasted_iota(jnp.int32, (SUBLANES, D_RG), 0)
    tile_row = lambda v, i: jnp.broadcast_to(v[i:i + 1, :], (SUBLANES, D_RG))
    prev = jnp.where(blk == 0, 0.0, prev_ref[...])
    nxt = jnp.where(blk == nblk - 1, 0.0, next_ref[...])
    xbuf[0:SUBLANES, :] = jnp.where(row == 0, tile_row(prev, SUBLANES - 2), pltpu.roll(seam[seg - 2], 1, 0))
    xbuf[SUBLANES:X0, :] = jnp.where(row == 0, tile_row(prev, SUBLANES - 1), pltpu.roll(seam[seg - 1], 1, 0))
    xbuf[X0 + tb:, :] = jnp.where(row == SUBLANES - 1, tile_row(nxt, 0),
                                  pltpu.roll(seam[0], SUBLANES - 1, 0))
    xc = cb_ref[...] + xbuf[0:tb, :] * cw_ref[0:1, :]
    xc = xc + xbuf[SUBLANES:SUBLANES + tb, :] * cw_ref[1:2, :]
    xc = xc + xbuf[X0:X0 + tb, :] * cw_ref[2:3, :]
    return xc + xbuf[X0 + SUBLANES:X0 + SUBLANES + tb, :] * cw_ref[3:4, :]


def _rglru_scan(xc, wa_ref, wx_ref, ba_ref, bx_ref, lam_ref, h_ref, slabs, pbuf, lbuf, carry, *,
                reverse, tb):
    nslab = D_RG // LANES
    seg = tb // SUBLANES
    pitch = seg + SUBLANES
    row = lax.broadcasted_iota(jnp.int32, (SUBLANES, D_RG), 0)
    tile_row = lambda v, i: jnp.broadcast_to(v[i:i + 1, :], (SUBLANES, D_RG))

    xcb = xc.astype(BF16)
    r = jax.nn.sigmoid(_dot(xcb, wa_ref[...]) + ba_ref[...])
    i = jax.nn.sigmoid(_dot(xcb, wx_ref[...]) + bx_ref[...])
    decay_rate = RG_C * _softplus(-lam_ref[...])
    a = jnp.exp2((decay_rate * -1.4426950408889634) * r)
    y = jnp.tanh(decay_rate * r) * (a * a + 1.0)
    u = jnp.where(y > 0.0, y * lax.rsqrt(y), 0.0) * (i * xc)

    order = range(seg - 1, -1, -1) if reverse else range(seg)
    P = L = None
    for j in order:
        rs = slice(SUBLANES * j, SUBLANES * (j + 1))
        if P is None:
            P, L = a[rs], u[rs]
        else:
            P, L = a[rs] * P, a[rs] * L + u[rs]
        pbuf[rs, :] = P
        lbuf[rs, :] = L

    A, U = P, L
    for s in (1, 2, 4):
        if reverse:
            keep = row < SUBLANES - s
            shift = SUBLANES - s
        else:
            keep = row >= s
            shift = s
        a_sh = jnp.where(keep, pltpu.roll(A, shift, 0), 1.0)
        u_sh = jnp.where(keep, pltpu.roll(U, shift, 0), 0.0)
        U = A * u_sh + U
        A = A * a_sh
    c_in = carry[...]
    e = U + A * c_in
    if reverse:
        c_seg = jnp.where(row == SUBLANES - 1, c_in, pltpu.roll(e, SUBLANES - 1, 0))
        carry[...] = tile_row(e, 0)
    else:
        c_seg = jnp.where(row == 0, c_in, pltpu.roll(e, 1, 0))
        carry[...] = tile_row(e, SUBLANES - 1)

    for j in range(seg):
        rs = slice(SUBLANES * j, SUBLANES * (j + 1))
        hj = lbuf[rs, :] + pbuf[rs, :] * c_seg
        for k in range(nslab):
            slabs[k, pl.ds(j, SUBLANES, stride=pitch), :] = hj[:, LANES * k:LANES * (k + 1)]
    for k in range(nslab):
        for s in range(SUBLANES):
            h_ref[seg * s:seg * (s + 1), LANES * k:LANES * (k + 1)] = (
                slabs[k, pitch * s:pitch * s + seg, :].astype(h_ref.dtype))


def _rglru(x, cw, cb, gates_fwd, gates_bwd, *, B, S):
    T = B * S
    tb = TB_RG
    nblk = S // tb
    hb = tb // SUBLANES
    n_halo = T // SUBLANES
    full = lambda shape: pl.BlockSpec(shape, lambda b, j: (0,) * len(shape))
    gate_specs = [full((D_RG, D_RG)), full((D_RG, D_RG)), full((1, D_RG)), full((1, D_RG)), full((1, D_RG))]
    slabs = pltpu.VMEM((D_RG // LANES, tb + SUBLANES * SUBLANES, LANES), F32)
    scan_scratch = [
        pltpu.VMEM((tb, D_RG), F32),
        pltpu.VMEM((tb, D_RG), F32),
        pltpu.VMEM((SUBLANES, D_RG), F32),
    ]
    tile_fwd = pl.BlockSpec((tb, D_RG), lambda b, j: (b * nblk + j, 0))
    tile_bwd = pl.BlockSpec((tb, D_RG), lambda b, j: (b * nblk + nblk - 1 - j, 0))

    h_fwd, xc = pl.pallas_call(
        functools.partial(_rglru_fwd_kernel, nblk=nblk, tb=tb),
        grid=(B, nblk),
        in_specs=[
            tile_fwd,
            pl.BlockSpec((SUBLANES, D_RG), lambda b, j: (jnp.maximum((b * nblk + j) * hb - 1, 0), 0)),
            pl.BlockSpec((SUBLANES, D_RG), lambda b, j: (jnp.minimum((b * nblk + j + 1) * hb, n_halo - 1), 0)),
            full((4, D_RG)), full((1, D_RG)), *gate_specs,
        ],
        out_specs=[tile_fwd, tile_fwd],
        out_shape=[jax.ShapeDtypeStruct((T, D_RG), BF16),
                   jax.ShapeDtypeStruct((T, D_RG), F32)],
        scratch_shapes=[slabs, pltpu.VMEM((tb + 3 * SUBLANES, D_RG), F32), *scan_scratch],
        compiler_params=_params("parallel", "arbitrary"),
        name="rglru_fwd",
    )(x, x, x, cw, cb, *gates_fwd)
    h_bwd = pl.pallas_call(
        functools.partial(_rglru_bwd_kernel, tb=tb),
        grid=(B, nblk),
        in_specs=[tile_bwd, *gate_specs],
        out_specs=tile_bwd,
        out_shape=jax.ShapeDtypeStruct((T, D_RG), BF16),
        scratch_shapes=[slabs, *scan_scratch],
        compiler_params=_params("parallel", "arbitrary"),
        name="rglru_bwd",
    )(xc, *gates_bwd)
    return h_fwd, h_bwd


def _mlstm_kernel(q_ref, k_ref, vt_ref, gr_ref, uc_ref, h_ref, ct_st, n_st, m_st, *, reverse, bb):
    L = ML_CHUNK
    NH = ML_HEADS
    d = 1 if reverse else 0

    @pl.when(pl.program_id(1) == 0)
    def _():
        ct_st[...] = jnp.zeros_like(ct_st)
        n_st[...] = jnp.zeros_like(n_st)
        m_st[...] = jnp.zeros_like(m_st)

    row8 = lax.broadcasted_iota(jnp.int32, (SUBLANES, L), 0)
    s_id = lax.broadcasted_iota(jnp.int32, (L, L), 0)
    t_id = lax.broadcasted_iota(jnp.int32, (L, L), 1)
    valid = (s_id >= t_id) if reverse else (s_id <= t_id)
    zeros8 = jnp.zeros((SUBLANES, L), F32)
    tile = lambda rows, hd: jnp.broadcast_to(rows[hd:hd + 1, :], (L, L))
    row_of = lambda rows, hd: jnp.broadcast_to(rows[hd:hd + 1, :], (SUBLANES, L))
    pairs = [(b, hp) for b in range(bb) for hp in range(NH // 2)]

    seq = []
    for b in range(bb):
        base = G_ROWS * d
        bcum = gr_ref[b, base + G_B:base + G_B + 8, :]
        u = gr_ref[b, base + G_U:base + G_U + 8, :]
        b_last = gr_ref[b, base + G_BL:base + G_BL + 8, :]
        m_prev = m_st[b]
        inter = bcum + m_prev
        m_t = jnp.maximum(inter, gr_ref[b, base + G_A:base + G_A + 8, :])
        m_new = jnp.maximum(b_last + m_prev, gr_ref[b, base + G_GM:base + G_GM + 8, :])
        n_prev = n_st[b]
        wg = jnp.exp(u + (b_last - m_new))
        seq.append(dict(
            w_int=jnp.exp(inter - m_t), e_neg=jnp.exp(-m_t), v3=_split3(bcum - m_t),
            decay=jnp.exp(b_last + m_prev - m_new), m_new=m_new, n_prev=n_prev, wg=wg,
            n_lhs=jnp.concatenate([n_prev, zeros8], axis=0).astype(BF16),
            wg_lhs=jnp.concatenate([wg, zeros8], axis=0).astype(BF16)))

    cs = lambda hd: slice(hd * ML_HD, (hd + 1) * ML_HD)
    side = lambda i: slice(i * L, (i + 1) * L)
    twice = lambda x: jnp.concatenate([x, x], axis=1)

    def blockdiag(a0, a1):
        z = jnp.zeros_like(a0)
        return jnp.concatenate([jnp.concatenate([a0, z], axis=1), jnp.concatenate([z, a1], axis=1)], axis=0)

    def expo_rhs(b, hd):
        onehot = jnp.where(row8 == hd, 1.0, 0.0)
        v_hi, v_mid, v_lo = (row_of(x, hd) for x in seq[b]["v3"])
        v_rows = jnp.where(row8 == 0, v_hi, jnp.where(row8 == 1, v_mid, jnp.where(row8 == 2, v_lo, 0.0)))
        slabs = [zeros8] * (L // SUBLANES)
        for j in range(3):
            slabs[(UC_DIR * d) // SUBLANES + j] = onehot
        slabs[UC_ONES // SUBLANES] = v_rows
        return jnp.concatenate(slabs, axis=0).astype(BF16)

    st, expo, q_diag = {}, {}, {}
    for b, hp in pairs:
        h0, h1 = 2 * hp, 2 * hp + 1
        q_diag[b, hp] = blockdiag(q_ref[b, :, cs(h0)], q_ref[b, :, cs(h1)])
        st[b, hp] = _dot_nt(
            jnp.concatenate([k_ref[b, :, h0 * ML_HD:(h1 + 1) * ML_HD], twice(seq[b]["n_lhs"])], axis=0),
            q_diag[b, hp])
        expo[b, hp] = _dot(uc_ref[b], jnp.concatenate([expo_rhs(b, h0), expo_rhs(b, h1)], axis=1))

    for b, hp in pairs:
        sq = seq[b]
        heads = (2 * hp, 2 * hp + 1)
        p_t, rw = [], []
        for i, hd in enumerate(heads):
            s_t = st[b, hp][0:L, side(i)] * jnp.exp(jnp.where(valid, expo[b, hp][:, side(i)], -jnp.inf))
            w_h = sq["w_int"][hd:hd + 1, :]
            den = jnp.sum(s_t, axis=0, keepdims=True) + w_h * st[b, hp][L + hd:L + hd + 1, side(i)]
            r = 1.0 / jnp.maximum(jnp.abs(den), sq["e_neg"][hd:hd + 1, :])
            p_t.append((s_t * r).astype(BF16))
            rw.append(r * w_h)
        intra = _dot(jnp.concatenate([vt_ref[b, cs(hd), :] for hd in heads], axis=1), blockdiag(*p_t))
        inter = _dot_nt(jnp.concatenate([ct_st[b, hd].astype(BF16) for hd in heads], axis=1), q_diag[b, hp])
        out = (intra + inter * jnp.concatenate(rw, axis=1)).astype(h_ref.dtype)
        for i, hd in enumerate(heads):
            h_ref[b, cs(hd), :] = out[:, side(i)]

    for b, hp in pairs:
        sq = seq[b]
        heads = (2 * hp, 2 * hp + 1)
        vw = jnp.concatenate([(vt_ref[b, cs(hd), :].astype(F32) * sq["wg"][hd:hd + 1, :]).astype(BF16)
                              for hd in heads], axis=1)
        upd = _dot(jnp.concatenate([vw, twice(sq["wg_lhs"])], axis=0),
                   blockdiag(k_ref[b, :, cs(heads[0])], k_ref[b, :, cs(heads[1])]))
        for i, hd in enumerate(heads):
            ct_st[b, hd] = tile(sq["decay"], hd) * ct_st[b, hd] + upd[0:L, side(i)]
            n_st[b, hd:hd + 1, :] = (sq["decay"][hd:hd + 1, :] * sq["n_prev"][hd:hd + 1, :]
                                     + upd[L + hd:L + hd + 1, side(i)])
    for b in range(bb):
        m_st[b] = seq[b]["m_new"]


def _mlstm(qk3, vt, gr4, uc3, *, reverse):
    B, S, _ = qk3.shape
    L = ML_CHUNK
    nc = S // L
    bb = ML_SEQS

    def chunk(c):
        return (nc - 1 - c) if reverse else c

    kern = functools.partial(_mlstm_kernel, reverse=reverse, bb=bb)
    return pl.pallas_call(
        kern,
        grid=(B // bb, nc),
        in_specs=[
            pl.BlockSpec((bb, L, D_ML), lambda b, c: (b, chunk(c), 0)),
            pl.BlockSpec((bb, L, D_ML), lambda b, c: (b, chunk(c), 1)),
            pl.BlockSpec((bb, D_ML, L), lambda b, c: (b, 0, chunk(c))),
            pl.BlockSpec((bb, None, 2 * G_ROWS, L), lambda b, c: (b, chunk(c), 0, 0)),
            pl.BlockSpec((bb, L, LANES), lambda b, c: (b, chunk(c), 0)),
        ],
        out_specs=pl.BlockSpec((bb, D_ML, L), lambda b, c: (b, 0, chunk(c))),
        out_shape=jax.ShapeDtypeStruct((B, D_ML, S), BF16),
        scratch_shapes=[
            pltpu.VMEM((bb, ML_HEADS, ML_HD, ML_HD), F32),
            pltpu.VMEM((bb, 2 * ML_HEADS, ML_HD), F32),
            pltpu.VMEM((bb, 2 * ML_HEADS, LANES), F32),
        ],
        compiler_params=_params("parallel", "arbitrary"),
        name="mlstm_bwd" if reverse else "mlstm_fwd",
    )(qk3, qk3, vt, gr4, uc3)


def _outproj_kernel(rf_ref, rb_ref, gate_ref, mf_ref, mb_ref, ot_ref, mg_ref, wr_ref, wm_ref, y_ref):
    for r0 in range(0, y_ref.shape[0], TM_SUB):
        rows = slice(r0, r0 + TM_SUB)
        y_rg = _gelu_gate(gate_ref[rows, :].astype(F32),
                          rf_ref[rows, :].astype(F32) + rb_ref[rows, :].astype(F32))
        acc = _dot(y_rg.astype(BF16), wr_ref[...])
        h_t = mf_ref[:, rows].astype(F32) + mb_ref[:, rows].astype(F32)
        parts = []
        for hd in range(ML_HEADS):
            hh = h_t[hd * ML_HD:(hd + 1) * ML_HD]
            parts.append(hh * lax.rsqrt(jnp.mean(hh * hh, axis=0, keepdims=True) + EPS))
        mg = jnp.tile(mg_ref[...], (1, TM_SUB // LANES))
        y_t = jax.nn.sigmoid(ot_ref[:, rows].astype(F32)) * (jnp.concatenate(parts, axis=0) * mg)
        y_ref[rows, :] = acc + _dot(y_t.T.astype(BF16), wm_ref[...])


def _outproj(rf, rb, pa, mf_t, mb_t, o_t, mg_tile, w_rg, w_ml, *, S):
    T = rf.shape[0]
    tm = TM_PROJ
    nb = S // tm
    tok = lambda width, col: pl.BlockSpec((tm, width), lambda i: (i, col))
    seq_t = pl.BlockSpec((None, D_ML, tm), lambda i: (i // nb, 0, i % nb))
    full = lambda shape: pl.BlockSpec(shape, lambda i: (0,) * len(shape))
    return pl.pallas_call(
        _outproj_kernel,
        grid=(T // tm,),
        in_specs=[
            tok(D_RG, 0), tok(D_RG, 0), tok(D_RG, 0),
            seq_t, seq_t, seq_t,
            full((D_ML, LANES)), full((D_RG, D_MODEL)), full((D_ML, D_MODEL)),
        ],
        out_specs=tok(D_MODEL, 0),
        out_shape=jax.ShapeDtypeStruct((T, D_MODEL), F32),
        compiler_params=_params("parallel"),
        name="outproj",
    )(rf, rb, pa, mf_t, mb_t, o_t, mg_tile, w_rg, w_ml)


def _ffn_kernel(x_ref, prev_ref, next_ref, d_ref, dprev_ref, dnext_ref, g_ref, wu_ref, cw_ref, cb_ref,
                wd_ref, fg_ref, y_ref, slabs, hbuf, uvbuf, acts, acc, *, nblk, tb, final):
    blk = pl.program_id(0) % nblk
    sub = FF_SUB
    nsub = D_FF // sub
    nring = uvbuf.shape[0]
    nslab = D_MODEL // LANES
    seg = tb // SUBLANES
    pitch = seg + SUBLANES

    g = g_ref[...]
    hn = _rmsnorm(x_ref[...] + d_ref[...], g)
    for k in range(nslab):
        for s in range(SUBLANES):
            slabs[k, pitch * s:pitch * s + seg, :] = hn[seg * s:seg * (s + 1), LANES * k:LANES * (k + 1)]

    def perm_rows(j):
        return jnp.concatenate(
            [slabs[k, pl.ds(j, SUBLANES, stride=pitch), :] for k in range(nslab)], axis=1)

    for jj in range(seg // 2):
        hbuf[BF16_ROWS * jj:BF16_ROWS * (jj + 1), :] = jnp.concatenate(
            [perm_rows(2 * jj), perm_rows(2 * jj + 1)], axis=0).astype(BF16)
    row_x = lax.broadcasted_iota(jnp.int32, (SUBLANES, D_MODEL), 0)
    h_prev = jnp.where(blk == 0, 0.0, pltpu.roll(_rmsnorm(prev_ref[...] + dprev_ref[...], g), 1, 0))
    h_next = jnp.where(blk == nblk - 1, 0.0,
                       pltpu.roll(_rmsnorm(next_ref[...] + dnext_ref[...], g), 1, 0))
    halo = jnp.where(row_x == 0, h_prev, jnp.where(row_x == 1, h_next, 0.0))
    hbuf[tb:, :] = jnp.concatenate([halo, jnp.zeros_like(halo)], axis=0).astype(BF16)

    row_u = lax.broadcasted_iota(jnp.int32, (SUBLANES, 2 * sub), 0)

    def pair(ref, sc):
        return jnp.concatenate([ref[:, sub * sc:sub * (sc + 1)],
                                ref[:, D_FF + sub * sc:D_FF + sub * (sc + 1)]], axis=1)

    def up(sc):
        slot = sc % nring
        res = jnp.concatenate([_dot(hbuf[...], wu_ref[:, sub * sc:sub * (sc + 1)]),
                               _dot(hbuf[...], wu_ref[:, D_FF + sub * sc:D_FF + sub * (sc + 1)])], axis=1)
        uvbuf[slot, SUBLANES:SUBLANES + tb, :] = res[0:tb]
        uvbuf[slot, 0:SUBLANES, :] = jnp.where(
            row_u == 0, jnp.broadcast_to(res[tb:tb + 1], row_u.shape),
            pltpu.roll(res[tb - SUBLANES:tb], 1, 0))
        uvbuf[slot, SUBLANES + tb:, :] = jnp.where(
            row_u == SUBLANES - 1, jnp.broadcast_to(res[tb + 1:tb + 2], row_u.shape),
            pltpu.roll(res[0:SUBLANES], SUBLANES - 1, 0))

    def gate(sc):
        slot = sc % nring
        cw = pair(cw_ref, sc)
        c = pair(cb_ref, sc) + uvbuf[slot, 0:tb, :] * cw[0:1]
        c = c + uvbuf[slot, SUBLANES:SUBLANES + tb, :] * cw[1:2]
        c = c + uvbuf[slot, 2 * SUBLANES:2 * SUBLANES + tb, :] * cw[2:3]
        return _gelu_gate(c[:, :sub], c[:, sub:]).astype(BF16)

    up(0)
    up(1)
    for sc in range(nsub):
        if sc + 2 < nsub:
            up(sc + 2)
        acts[:, sub * sc:sub * (sc + 1)] = gate(sc)

    acc[...] = _dot(acts[...], wd_ref[...])

    for j in range(seg):
        for k in range(nslab):
            slabs[k, pl.ds(j, SUBLANES, stride=pitch), :] = acc[SUBLANES * j:SUBLANES * (j + 1),
                                                                LANES * k:LANES * (k + 1)]
    ffn = jnp.concatenate(
        [jnp.concatenate([slabs[k, pitch * s:pitch * s + seg, :] for s in range(SUBLANES)], axis=0)
         for k in range(nslab)], axis=1)
    y = (x_ref[...] + d_ref[...]) + ffn
    if final:
        y = _rmsnorm(y, fg_ref[...])
    y_ref[...] = y


def _ffn(x2, d2, g, w_up, cw, cb, w_down, fg, *, S, final):
    T = x2.shape[0]
    tb = TB_FFN
    nblk = S // tb
    hpb = tb // SUBLANES
    n_halo = T // SUBLANES
    full = lambda shape: pl.BlockSpec(shape, lambda i: (0,) * len(shape))
    tile = pl.BlockSpec((tb, D_MODEL), lambda i: (i, 0))
    halo_prev = pl.BlockSpec((SUBLANES, D_MODEL), lambda i: (jnp.maximum(i * hpb - 1, 0), 0))
    halo_next = pl.BlockSpec((SUBLANES, D_MODEL), lambda i: (jnp.minimum((i + 1) * hpb, n_halo - 1), 0))
    kern = functools.partial(_ffn_kernel, nblk=nblk, tb=tb, final=final)
    return pl.pallas_call(
        kern,
        grid=(T // tb,),
        in_specs=[
            tile, halo_prev, halo_next, tile, halo_prev, halo_next,
            full((1, D_MODEL)), full((D_MODEL, 2 * D_FF)), full((3, 2 * D_FF)), full((1, 2 * D_FF)),
            full((D_FF, D_MODEL)), full((1, D_MODEL)),
        ],
        out_specs=pl.BlockSpec((tb, D_MODEL), lambda i: (i, 0)),
        out_shape=jax.ShapeDtypeStruct((T, D_MODEL), F32),
        scratch_shapes=[
            pltpu.VMEM((D_MODEL // LANES, tb + SUBLANES * SUBLANES, LANES), F32),
            pltpu.VMEM((tb + BF16_ROWS, D_MODEL), BF16),
            pltpu.VMEM((FF_RING, tb + 2 * SUBLANES, 2 * FF_SUB), F32),
            pltpu.VMEM((tb, D_FF), BF16),
            pltpu.VMEM((tb, D_MODEL), F32),
        ],
        compiler_params=pltpu.CompilerParams(dimension_semantics=("parallel",),
                                             vmem_limit_bytes=VMEM_LIMIT_FFN),
        name="convffn",
    )(x2, x2, x2, d2, d2, d2, g, w_up, cw, cb, w_down, fg)


def _block_diag(w):
    eye = jnp.eye(RG_BLOCKS, dtype=w.dtype)
    return jnp.einsum('ncd,nm->ncmd', w, eye).reshape(D_RG, D_RG)


def _encoder(x, norm1_g, w_in, b_gates, rg_conv_w, rg_conv_b, rg_wa, rg_ba, rg_wx, rg_bx, rg_lambda,
             ml_norm_g, w_out, norm2_g, w_up, ffn_conv_w, ffn_conv_b, w_down, final_g):
    B, S, _ = x.shape
    T = B * S
    depth = w_in.shape[0]
    x2 = x.reshape(T, D_MODEL)
    row = lambda v: v.reshape(1, -1).astype(F32)
    n_nat = 2 * D_RG + 2 * D_ML
    for l in range(depth):
        w_nat = w_in[l, :, :n_nat].astype(BF16)
        w_tr = w_in[l, :, n_nat:n_nat + 2 * D_ML].T.astype(BF16)
        w_gate = w_in[l, :, n_nat + 2 * D_ML:].T.astype(BF16)
        bias = jnp.broadcast_to(b_gates[l].astype(F32).reshape(N_GATE, 1), (N_GATE, ML_CHUNK))
        rx, rgate, qk, v_t, o_t, gr, uc = _inproj(x2, row(norm1_g[l]), w_nat, w_tr, w_gate, bias,
                                                  B=B, S=S)
        qk3 = qk.reshape(B, S, 2 * D_ML)
        gr4 = gr.reshape(B, S // ML_CHUNK, 2 * G_ROWS, ML_CHUNK)
        uc3 = uc.reshape(B, S, LANES)
        rg_gates = [(_block_diag(rg_wa[l, d]).astype(BF16), _block_diag(rg_wx[l, d]).astype(BF16),
                     row(rg_ba[l, d]), row(rg_bx[l, d]), row(rg_lambda[l, d])) for d in range(2)]
        r_dir = _rglru(rx, rg_conv_w[l].astype(F32), row(rg_conv_b[l]), *rg_gates, B=B, S=S)
        m_dir = [_mlstm(qk3, v_t, gr4, uc3, reverse=reverse) for reverse in (False, True)]
        wo = w_out[l].astype(BF16)
        mg_tile = jnp.broadcast_to(ml_norm_g[l].astype(F32).reshape(D_ML, 1), (D_ML, LANES))
        mixed = _outproj(r_dir[0], r_dir[1], rgate, m_dir[0], m_dir[1], o_t, mg_tile,
                         wo[:D_RG], wo[D_RG:], S=S)
        x2 = _ffn(x2, mixed, row(norm2_g[l]), w_up[l].astype(BF16),
                  ffn_conv_w[l].astype(F32), row(ffn_conv_b[l]),
                  w_down[l].astype(BF16), row(final_g), S=S,
                  final=(l == depth - 1))
    return x2.reshape(B, S, D_MODEL)


def kernel(x_prompt, x_sample, norm1_g, w_in, b_gates, rg_conv_w, rg_conv_b, rg_wa, rg_ba, rg_wx, rg_bx,
           rg_lambda, ml_norm_g, w_out, norm2_g, w_up, ffn_conv_w, ffn_conv_b, w_down, final_g):
    weights = (norm1_g, w_in, b_gates, rg_conv_w, rg_conv_b, rg_wa, rg_ba, rg_wx, rg_bx, rg_lambda,
               ml_norm_g, w_out, norm2_g, w_up, ffn_conv_w, ffn_conv_b, w_down, final_g)
    return (_encoder(x_prompt, *weights), _encoder(x_sample, *weights))
```

```python
import functools

import jax
import jax.numpy as jnp
from jax import lax
from jax.experimental import pallas as pl
from jax.experimental.pallas import tpu as pltpu

F32 = jnp.float32
BF16 = jnp.bfloat16

D_MODEL = 1024
D_RG = 512
D_ML = 512
RG_BLOCKS = 8
RG_C = 8.0
ML_HEADS = 4
ML_HD = 128
ML_CHUNK = 128
D_FF = 3072
EPS = 1e-6
N_GATE = 4 * ML_HEADS

SUBLANES = 8
LANES = 128
BF16_ROWS = 16
VMEM_LIMIT = 48 * 1024 * 1024

TM_PROJ = 1024
TM_SUB = 512
TB_RG = 1024
ML_SEQS = 16
TB_FFN = 512
FF_SUB = 256
FF_RING = 4
VMEM_LIMIT_FFN = 56 * 1024 * 1024

G_B, G_U, G_A, G_BL, G_GM = 0, 8, 16, 24, 32
G_ROWS = 40
UC_DIR = 24
UC_ONES = 2 * UC_DIR


def _params(*sem):
    return pltpu.CompilerParams(dimension_semantics=sem, vmem_limit_bytes=VMEM_LIMIT)


def _softplus(z):
    return jnp.maximum(z, 0.0) + jnp.log1p(jnp.exp(-jnp.abs(z)))


def _gelu_gate(gate, val):
    k0 = -2.0 * 0.7978845608028654 * 1.4426950408889634
    z = gate * (k0 + (k0 * 0.044715) * (gate * gate))
    return (gate * val) / (1.0 + jnp.exp2(z))


def _rmsnorm(x, g):
    return x * lax.rsqrt(jnp.mean(x * x, axis=-1, keepdims=True) + EPS) * g


def _dot(a, b):
    return jnp.dot(a, b, preferred_element_type=F32)


def _dot_nt(a, b):
    return lax.dot_general(a, b, (((1,), (1,)), ((), ())), preferred_element_type=F32)


def _split3(x):
    hi = x.astype(BF16).astype(F32)
    r1 = x - hi
    mid = r1.astype(BF16).astype(F32)
    return hi, mid, (r1 - mid).astype(BF16).astype(F32)


def _lane_scan(x, op, fill, reverse):
    n = x.shape[-1]
    lane = lax.broadcasted_iota(jnp.int32, x.shape, 1)
    s = 1
    while s < n:
        if reverse:
            x = op(x, jnp.where(lane < n - s, pltpu.roll(x, n - s, 1), fill))
        else:
            x = op(x, jnp.where(lane >= s, pltpu.roll(x, s, 1), fill))
        s *= 2
    return x


def _inproj_kernel(x_ref, g_ref, wn_ref, wt_ref, wg_ref, bias_ref, rx_ref, rg_ref, qk_ref, vt_ref,
                   ot_ref, gr_ref, uc_ref):
    for r0 in range(0, x_ref.shape[0], TM_SUB):
        _inproj_piece(x_ref, g_ref, wn_ref, wt_ref, wg_ref, bias_ref, rx_ref, rg_ref, qk_ref, vt_ref,
                      ot_ref, gr_ref, uc_ref, r0)


def _inproj_piece(x_ref, g_ref, wn_ref, wt_ref, wg_ref, bias_ref, rx_ref, rg_ref, qk_ref, vt_ref,
                  ot_ref, gr_ref, uc_ref, r0):
    L = ML_CHUNK
    NH = ML_HEADS
    rows = slice(r0, r0 + TM_SUB)
    h = _rmsnorm(x_ref[rows, :], g_ref[...]).astype(BF16)

    gt = _dot_nt(wg_ref[...], h)
    rowid = lax.broadcasted_iota(jnp.int32, (2 * NH, L), 0)
    head_row = rowid < NH
    rep = lambda col: jnp.broadcast_to(col, (2 * NH, L))
    zeros8 = jnp.zeros((2 * NH, L), F32)
    for cl in range(TM_SUB // L):
        c = r0 // L + cl
        g16 = gt[:, cl * L:(cl + 1) * L] + bias_ref[...]
        tiles = []
        for d, reverse in enumerate((False, True)):
            gates = g16[2 * NH * d:2 * NH * (d + 1)]
            lf = jnp.where(head_row, 0.0, -_softplus(-gates))
            bcum = pltpu.roll(_lane_scan(lf, jnp.add, 0.0, reverse), NH, 0)
            u = jnp.where(head_row, gates - bcum, 0.0)
            last = 0 if reverse else L - 1
            bl = rep(bcum[:, last:last + 1])
            base = G_ROWS * d
            gr_ref[c, base + G_B:base + G_B + 8, :] = bcum
            gr_ref[c, base + G_U:base + G_U + 8, :] = u
            gr_ref[c, base + G_A:base + G_A + 8, :] = bcum + _lane_scan(u, jnp.maximum, -jnp.inf, reverse)
            gr_ref[c, base + G_BL:base + G_BL + 8, :] = bl
            gr_ref[c, base + G_GM:base + G_GM + 8, :] = rep(jnp.max(bl + u, axis=-1, keepdims=True))
            tiles.extend(_split3(u))
        tiles.append(jnp.ones((2 * NH, L), F32))
        tiles.extend([zeros8] * (L // 8 - len(tiles)))
        uc_ref[c * L:(c + 1) * L, :] = jnp.concatenate(tiles, axis=0).T.astype(BF16)

    nat = _dot(h, wn_ref[...])
    rx_ref[rows, :] = nat[:, :D_RG]
    rg_ref[rows, :] = nat[:, D_RG:2 * D_RG].astype(BF16)
    qk_ref[rows, :] = jnp.concatenate(
        [nat[:, 2 * D_RG:2 * D_RG + D_ML] * (ML_HD ** -0.5), nat[:, 2 * D_RG + D_ML:]], axis=-1).astype(BF16)
    tr = _dot_nt(wt_ref[...], h)
    vt_ref[:, rows] = tr[:D_ML].astype(BF16)
    ot_ref[:, rows] = tr[D_ML:].astype(BF16)


def _inproj(x2, g, w_nat, w_tr, w_gate, bias, *, B, S):
    T = B * S
    tm = TM_PROJ
    nb = S // tm
    full = lambda shape: pl.BlockSpec(shape, lambda i: (0,) * len(shape))
    tok = lambda width: pl.BlockSpec((tm, width), lambda i: (i, 0))
    seq_t = pl.BlockSpec((None, D_ML, tm), lambda i: (i // nb, 0, i % nb))
    return pl.pallas_call(
        _inproj_kernel,
        grid=(T // tm,),
        in_specs=[
            tok(D_MODEL), full((1, D_MODEL)), full(w_nat.shape), full(w_tr.shape), full(w_gate.shape),
            full((N_GATE, ML_CHUNK)),
        ],
        out_specs=[
            tok(D_RG), tok(D_RG), tok(2 * D_ML), seq_t, seq_t,
            pl.BlockSpec((tm // ML_CHUNK, 2 * G_ROWS, ML_CHUNK), lambda i: (i, 0, 0)),
            tok(LANES),
        ],
        out_shape=[
            jax.ShapeDtypeStruct((T, D_RG), F32),
            jax.ShapeDtypeStruct((T, D_RG), BF16),
            jax.ShapeDtypeStruct((T, 2 * D_ML), BF16),
            jax.ShapeDtypeStruct((B, D_ML, S), BF16),
            jax.ShapeDtypeStruct((B, D_ML, S), BF16),
            jax.ShapeDtypeStruct((T // ML_CHUNK, 2 * G_ROWS, ML_CHUNK), F32),
            jax.ShapeDtypeStruct((T, LANES), BF16),
        ],
        compiler_params=_params("parallel"),
        name="inproj",
    )(x2, g, w_nat, w_tr, w_gate, bias)


def _rglru_fwd_kernel(x_ref, prev_ref, next_ref, cw_ref, cb_ref, wa_ref, wx_ref, ba_ref, bx_ref,
                      lam_ref, h_ref, xc_ref, slabs, xbuf, pbuf, lbuf, carry, *, nblk, tb):
    blk = pl.program_id(1)

    @pl.when(blk == 0)
    def _():
        carry[...] = jnp.zeros_like(carry)

    npiece = x_ref.shape[0] // tb
    for p in range(npiece):
        rows = pl.ds(tb * p, tb)
        prev = (jnp.where(blk == 0, 0.0, prev_ref[...]) if p == 0
                else x_ref[tb * p - SUBLANES:tb * p, :])
        nxt = (jnp.where(blk == nblk - 1, 0.0, next_ref[...]) if p == npiece - 1
               else x_ref[tb * (p + 1):tb * (p + 1) + SUBLANES, :])
        xc = _rglru_conv(x_ref[rows, :], prev, nxt, cw_ref, cb_ref, slabs.at[p], xbuf.at[p], tb=tb)
        xc_ref[rows, :] = xc
        _rglru_scan(xc, wa_ref, wx_ref, ba_ref, bx_ref, lam_ref, h_ref.at[rows], slabs.at[p],
                    pbuf.at[p], lbuf.at[p], carry, reverse=False, tb=tb)


def _rglru_bwd_kernel(xc_ref, wa_ref, wx_ref, ba_ref, bx_ref, lam_ref, h_ref, slabs, pbuf, lbuf,
                      carry, *, tb):
    @pl.when(pl.program_id(1) == 0)
    def _():
        carry[...] = jnp.zeros_like(carry)

    for p in reversed(range(xc_ref.shape[0] // tb)):
        rows = pl.ds(tb * p, tb)
        _rglru_scan(xc_ref[rows, :], wa_ref, wx_ref, ba_ref, bx_ref, lam_ref, h_ref.at[rows],
                    slabs.at[p], pbuf.at[p], lbuf.at[p], carry, reverse=True, tb=tb)


def _rglru_conv(x, prev, nxt, cw_ref, cb_ref, slabs, xbuf, *, tb):
    nslab = D_RG // LANES
    seg = tb // SUBLANES
    pitch = seg + SUBLANES
    X0 = 2 * SUBLANES

    for k in range(nslab):
        for s in range(SUBLANES):
            slabs[k, pitch * s:pitch * s + seg, :] = x[seg * s:seg * (s + 1), LANES * k:LANES * (k + 1)]
    seam = {}
    for j in range(seg):
        rows = jnp.concatenate(
            [slabs[k, pl.ds(j, SUBLANES, stride=pitch), :] for k in range(nslab)], axis=1)
        xbuf[X0 + SUBLANES * j:X0 + SUBLANES * (j + 1), :] = rows
        if j in (0, seg - 2, seg - 1):
            seam[j] = rows
    row = lax.broadcasted_iota(jnp.int32, (SUBLANES, D_RG), 0)
    tile_row = lambda v, i: jnp.broadcast_to(v[i:i + 1, :], (SUBLANES, D_RG))
    xbuf[0:SUBLANES, :] = jnp.where(row == 0, tile_row(prev, SUBLANES - 2), pltpu.roll(seam[seg - 2], 1, 0))
    xbuf[SUBLANES:X0, :] = jnp.where(row == 0, tile_row(prev, SUBLANES - 1), pltpu.roll(seam[seg - 1], 1, 0))
    xbuf[X0 + tb:, :] = jnp.where(row == SUBLANES - 1, tile_row(nxt, 0),
                                  pltpu.roll(seam[0], SUBLANES - 1, 0))
    xc = cb_ref[...] + xbuf[0:tb, :] * cw_ref[0:1, :]
    xc = xc + xbuf[SUBLANES:SUBLANES + tb, :] * cw_ref[1:2, :]
    xc = xc + xbuf[X0:X0 + tb, :] * cw_ref[2:3, :]
    return xc + xbuf[X0 + SUBLANES:X0 + SUBLANES + tb, :] * cw_ref[3:4, :]


def _rglru_scan(xc, wa_ref, wx_ref, ba_ref, bx_ref, lam_ref, h_ref, slabs, pbuf, lbuf, carry, *,
                reverse, tb):
    nslab = D_RG // LANES
    seg = tb // SUBLANES
    pitch = seg + SUBLANES
    row = lax.broadcasted_iota(jnp.int32, (SUBLANES, D_RG), 0)
    tile_row = lambda v, i: jnp.broadcast_to(v[i:i + 1, :], (SUBLANES, D_RG))

    xcb = xc.astype(BF16)
    r = jax.nn.sigmoid(_dot(xcb, wa_ref[...]) + ba_ref[...])
    i = jax.nn.sigmoid(_dot(xcb, wx_ref[...]) + bx_ref[...])
    decay_rate = RG_C * _softplus(-lam_ref[...])
    a = jnp.exp2((decay_rate * -1.4426950408889634) * r)
    y = jnp.tanh(decay_rate * r) * (a * a + 1.0)
    u = jnp.where(y > 0.0, y * lax.rsqrt(y), 0.0) * (i * xc)

    order = range(seg - 1, -1, -1) if reverse else range(seg)
    P = L = None
    for j in order:
        rs = slice(SUBLANES * j, SUBLANES * (j + 1))
        if P is None:
            P, L = a[rs], u[rs]
        else:
            P, L = a[rs] * P, a[rs] * L + u[rs]
        pbuf[rs, :] = P
        lbuf[rs, :] = L

    A, U = P, L
    for s in (1, 2, 4):
        if reverse:
            keep = row < SUBLANES - s
            shift = SUBLANES - s
        else:
            keep = row >= s
            shift = s
        a_sh = jnp.where(keep, pltpu.roll(A, shift, 0), 1.0)
        u_sh = jnp.where(keep, pltpu.roll(U, shift, 0), 0.0)
        U = A * u_sh + U
        A = A * a_sh
    c_in = carry[...]
    e = U + A * c_in
    if reverse:
        c_seg = jnp.where(row == SUBLANES - 1, c_in, pltpu.roll(e, SUBLANES - 1, 0))
        carry[...] = tile_row(e, 0)
    else:
        c_seg = jnp.where(row == 0, c_in, pltpu.roll(e, 1, 0))
        carry[...] = tile_row(e, SUBLANES - 1)

    for j in range(seg):
        rs = slice(SUBLANES * j, SUBLANES * (j + 1))
        hj = lbuf[rs, :] + pbuf[rs, :] * c_seg
        for k in range(nslab):
            slabs[k, pl.ds(j, SUBLANES, stride=pitch), :] = hj[:, LANES * k:LANES * (k + 1)]
    for k in range(nslab):
        for s in range(SUBLANES):
            h_ref[seg * s:seg * (s + 1), LANES * k:LANES * (k + 1)] = (
                slabs[k, pitch * s:pitch * s + seg, :].astype(h_ref.dtype))


def _rglru(x, cw, cb, gates_fwd, gates_bwd, *, B, S):
    T = B * S
    tile = TB_RG
    tb = TM_SUB
    npiece = tile // tb
    nblk = S // tile
    hb = tile // SUBLANES
    n_halo = T // SUBLANES
    full = lambda shape: pl.BlockSpec(shape, lambda b, j: (0,) * len(shape))
    gate_specs = [full((D_RG, D_RG)), full((D_RG, D_RG)), full((1, D_RG)), full((1, D_RG)), full((1, D_RG))]
    slabs = pltpu.VMEM((npiece, D_RG // LANES, tb + SUBLANES * SUBLANES, LANES), F32)
    scan_scratch = [
        pltpu.VMEM((npiece, tb, D_RG), F32),
        pltpu.VMEM((npiece, tb, D_RG), F32),
        pltpu.VMEM((SUBLANES, D_RG), F32),
    ]
    tile_fwd = pl.BlockSpec((tile, D_RG), lambda b, j: (b * nblk + j, 0))
    tile_bwd = pl.BlockSpec((tile, D_RG), lambda b, j: (b * nblk + nblk - 1 - j, 0))

    h_fwd, xc = pl.pallas_call(
        functools.partial(_rglru_fwd_kernel, nblk=nblk, tb=tb),
        grid=(B, nblk),
        in_specs=[
            tile_fwd,
            pl.BlockSpec((SUBLANES, D_RG), lambda b, j: (jnp.maximum((b * nblk + j) * hb - 1, 0), 0)),
            pl.BlockSpec((SUBLANES, D_RG), lambda b, j: (jnp.minimum((b * nblk + j + 1) * hb, n_halo - 1), 0)),
            full((4, D_RG)), full((1, D_RG)), *gate_specs,
        ],
        out_specs=[tile_fwd, tile_fwd],
        out_shape=[jax.ShapeDtypeStruct((T, D_RG), BF16),
                   jax.ShapeDtypeStruct((T, D_RG), F32)],
        scratch_shapes=[slabs, pltpu.VMEM((npiece, tb + 3 * SUBLANES, D_RG), F32), *scan_scratch],
        compiler_params=_params("parallel", "arbitrary"),
        name="rglru_fwd",
    )(x, x, x, cw, cb, *gates_fwd)
    h_bwd = pl.pallas_call(
        functools.partial(_rglru_bwd_kernel, tb=tb),
        grid=(B, nblk),
        in_specs=[tile_bwd, *gate_specs],
        out_specs=tile_bwd,
        out_shape=jax.ShapeDtypeStruct((T, D_RG), BF16),
        scratch_shapes=[slabs, *scan_scratch],
        compiler_params=_params("parallel", "arbitrary"),
        name="rglru_bwd",
    )(xc, *gates_bwd)
    return h_fwd, h_bwd


def _mlstm_kernel(q_ref, k_ref, vt_ref, gr_ref, uc_ref, h_ref, ct_st, n_st, m_st, *, reverse, bb):
    L = ML_CHUNK
    NH = ML_HEADS
    d = 1 if reverse else 0

    @pl.when(pl.program_id(1) == 0)
    def _():
        ct_st[...] = jnp.zeros_like(ct_st)
        n_st[...] = jnp.zeros_like(n_st)
        m_st[...] = jnp.zeros_like(m_st)

    row8 = lax.broadcasted_iota(jnp.int32, (SUBLANES, L), 0)
    s_id = lax.broadcasted_iota(jnp.int32, (L, L), 0)
    t_id = lax.broadcasted_iota(jnp.int32, (L, L), 1)
    valid = (s_id >= t_id) if reverse else (s_id <= t_id)
    zeros8 = jnp.zeros((SUBLANES, L), F32)
    tile = lambda rows, hd: jnp.broadcast_to(rows[hd:hd + 1, :], (L, L))
    row_of = lambda rows, hd: jnp.broadcast_to(rows[hd:hd + 1, :], (SUBLANES, L))
    pairs = [(b, hp) for b in range(bb) for hp in range(NH // 2)]

    seq = []
    for b in range(bb):
        base = G_ROWS * d
        bcum = gr_ref[b, base + G_B:base + G_B + 8, :]
        u = gr_ref[b, base + G_U:base + G_U + 8, :]
        b_last = gr_ref[b, base + G_BL:base + G_BL + 8, :]
        m_prev = m_st[b]
        inter = bcum + m_prev
        m_t = jnp.maximum(inter, gr_ref[b, base + G_A:base + G_A + 8, :])
        m_new = jnp.maximum(b_last + m_prev, gr_ref[b, base + G_GM:base + G_GM + 8, :])
        n_prev = n_st[b]
        wg = jnp.exp(u + (b_last - m_new))
        seq.append(dict(
            w_int=jnp.exp(inter - m_t), e_neg=jnp.exp(-m_t), v3=_split3(bcum - m_t),
            decay=jnp.exp(b_last + m_prev - m_new), m_new=m_new, n_prev=n_prev, wg=wg,
            n_lhs=jnp.concatenate([n_prev, zeros8], axis=0).astype(BF16),
            wg_lhs=jnp.concatenate([wg, zeros8], axis=0).astype(BF16)))

    cs = lambda hd: slice(hd * ML_HD, (hd + 1) * ML_HD)
    side = lambda i: slice(i * L, (i + 1) * L)
    twice = lambda x: jnp.concatenate([x, x], axis=1)

    def blockdiag(a0, a1):
        z = jnp.zeros_like(a0)
        return jnp.concatenate([jnp.concatenate([a0, z], axis=1), jnp.concatenate([z, a1], axis=1)], axis=0)

    def expo_rhs(b, hd):
        onehot = jnp.where(row8 == hd, 1.0, 0.0)
        v_hi, v_mid, v_lo = (row_of(x, hd) for x in seq[b]["v3"])
        v_rows = jnp.where(row8 == 0, v_hi, jnp.where(row8 == 1, v_mid, jnp.where(row8 == 2, v_lo, 0.0)))
        slabs = [zeros8] * (L // SUBLANES)
        for j in range(3):
            slabs[(UC_DIR * d) // SUBLANES + j] = onehot
        slabs[UC_ONES // SUBLANES] = v_rows
        return jnp.concatenate(slabs, axis=0).astype(BF16)

    st, expo, q_diag = {}, {}, {}
    for b, hp in pairs:
        h0, h1 = 2 * hp, 2 * hp + 1
        q_diag[b, hp] = blockdiag(q_ref[b, :, cs(h0)], q_ref[b, :, cs(h1)])
        st[b, hp] = _dot_nt(
            jnp.concatenate([k_ref[b, :, h0 * ML_HD:(h1 + 1) * ML_HD], twice(seq[b]["n_lhs"])], axis=0),
            q_diag[b, hp])
        expo[b, hp] = _dot(uc_ref[b], jnp.concatenate([expo_rhs(b, h0), expo_rhs(b, h1)], axis=1))

    for b, hp in pairs:
        sq = seq[b]
        heads = (2 * hp, 2 * hp + 1)
        p_t, rw = [], []
        for i, hd in enumerate(heads):
            s_t = st[b, hp][0:L, side(i)] * jnp.exp(jnp.where(valid, expo[b, hp][:, side(i)], -jnp.inf))
            w_h = sq["w_int"][hd:hd + 1, :]
            den = jnp.sum(s_t, axis=0, keepdims=True) + w_h * st[b, hp][L + hd:L + hd + 1, side(i)]
            r = 1.0 / jnp.maximum(jnp.abs(den), sq["e_neg"][hd:hd + 1, :])
            p_t.append((s_t * r).astype(BF16))
            rw.append(r * w_h)
        intra = _dot(jnp.concatenate([vt_ref[b, cs(hd), :] for hd in heads], axis=1), blockdiag(*p_t))
        inter = _dot_nt(jnp.concatenate([ct_st[b, hd].astype(BF16) for hd in heads], axis=1), q_diag[b, hp])
        out = (intra + inter * jnp.concatenate(rw, axis=1)).astype(h_ref.dtype)
        for i, hd in enumerate(heads):
            h_ref[b, cs(hd), :] = out[:, side(i)]

    for b, hp in pairs:
        sq = seq[b]
        heads = (2 * hp, 2 * hp + 1)
        vw = jnp.concatenate([(vt_ref[b, cs(hd), :].astype(F32) * sq["wg"][hd:hd + 1, :]).astype(BF16)
                              for hd in heads], axis=1)
        upd = _dot(jnp.concatenate([vw, twice(sq["wg_lhs"])], axis=0),
                   blockdiag(k_ref[b, :, cs(heads[0])], k_ref[b, :, cs(heads[1])]))
        for i, hd in enumerate(heads):
            ct_st[b, hd] = tile(sq["decay"], hd) * ct_st[b, hd] + upd[0:L, side(i)]
            n_st[b, hd:hd + 1, :] = (sq["decay"][hd:hd + 1, :] * sq["n_prev"][hd:hd + 1, :]
                                     + upd[L + hd:L + hd + 1, side(i)])
    for b in range(bb):
        m_st[b] = seq[b]["m_new"]


def _mlstm(qk3, vt, gr4, uc3, *, reverse):
    B, S, _ = qk3.shape
    L = ML_CHUNK
    nc = S // L
    bb = ML_SEQS

    def chunk(c):
        return (nc - 1 - c) if reverse else c

    kern = functools.partial(_mlstm_kernel, reverse=reverse, bb=bb)
    return pl.pallas_call(
        kern,
        grid=(B // bb, nc),
        in_specs=[
            pl.BlockSpec((bb, L, D_ML), lambda b, c: (b, chunk(c), 0)),
            pl.BlockSpec((bb, L, D_ML), lambda b, c: (b, chunk(c), 1)),
            pl.BlockSpec((bb, D_ML, L), lambda b, c: (b, 0, chunk(c))),
            pl.BlockSpec((bb, None, 2 * G_ROWS, L), lambda b, c: (b, chunk(c), 0, 0)),
            pl.BlockSpec((bb, L, LANES), lambda b, c: (b, chunk(c), 0)),
        ],
        out_specs=pl.BlockSpec((bb, D_ML, L), lambda b, c: (b, 0, chunk(c))),
        out_shape=jax.ShapeDtypeStruct((B, D_ML, S), BF16),
        scratch_shapes=[
            pltpu.VMEM((bb, ML_HEADS, ML_HD, ML_HD), F32),
            pltpu.VMEM((bb, 2 * ML_HEADS, ML_HD), F32),
            pltpu.VMEM((bb, 2 * ML_HEADS, LANES), F32),
        ],
        compiler_params=_params("parallel", "arbitrary"),
        name="mlstm_bwd" if reverse else "mlstm_fwd",
    )(qk3, qk3, vt, gr4, uc3)


def _outproj_kernel(rf_ref, rb_ref, gate_ref, mf_ref, mb_ref, ot_ref, mg_ref, wr_ref, wm_ref, y_ref):
    for r0 in range(0, y_ref.shape[0], TM_SUB):
        rows = slice(r0, r0 + TM_SUB)
        y_rg = _gelu_gate(gate_ref[rows, :].astype(F32),
                          rf_ref[rows, :].astype(F32) + rb_ref[rows, :].astype(F32))
        acc = _dot(y_rg.astype(BF16), wr_ref[...])
        h_t = mf_ref[:, rows].astype(F32) + mb_ref[:, rows].astype(F32)
        parts = []
        for hd in range(ML_HEADS):
            hh = h_t[hd * ML_HD:(hd + 1) * ML_HD]
            parts.append(hh * lax.rsqrt(jnp.mean(hh * hh, axis=0, keepdims=True) + EPS))
        mg = jnp.tile(mg_ref[...], (1, TM_SUB // LANES))
        y_t = jax.nn.sigmoid(ot_ref[:, rows].astype(F32)) * (jnp.concatenate(parts, axis=0) * mg)
        y_ref[rows, :] = acc + _dot(y_t.T.astype(BF16), wm_ref[...])


def _outproj(rf, rb, pa, mf_t, mb_t, o_t, mg_tile, w_rg, w_ml, *, S):
    T = rf.shape[0]
    tm = TM_PROJ
    nb = S // tm
    tok = lambda width, col: pl.BlockSpec((tm, width), lambda i: (i, col))
    seq_t = pl.BlockSpec((None, D_ML, tm), lambda i: (i // nb, 0, i % nb))
    full = lambda shape: pl.BlockSpec(shape, lambda i: (0,) * len(shape))
    return pl.pallas_call(
        _outproj_kernel,
        grid=(T // tm,),
        in_specs=[
            tok(D_RG, 0), tok(D_RG, 0), tok(D_RG, 0),
            seq_t, seq_t, seq_t,
            full((D_ML, LANES)), full((D_RG, D_MODEL)), full((D_ML, D_MODEL)),
        ],
        out_specs=tok(D_MODEL, 0),
        out_shape=jax.ShapeDtypeStruct((T, D_MODEL), F32),
        compiler_params=_params("parallel"),
        name="outproj",
    )(rf, rb, pa, mf_t, mb_t, o_t, mg_tile, w_rg, w_ml)


def _ffn_kernel(x_ref, prev_ref, next_ref, d_ref, dprev_ref, dnext_ref, g_ref, wu_ref, cw_ref, cb_ref,
                wd_ref, fg_ref, y_ref, slabs, hbuf, uvbuf, acts, acc, *, nblk, tb, final):
    blk = pl.program_id(0) % nblk
    sub = FF_SUB
    nsub = D_FF // sub
    nring = uvbuf.shape[0]
    nslab = D_MODEL // LANES
    seg = tb // SUBLANES
    pitch = seg + SUBLANES

    g = g_ref[...]
    hn = _rmsnorm(x_ref[...] + d_ref[...], g)
    for k in range(nslab):
        for s in range(SUBLANES):
            slabs[k, pitch * s:pitch * s + seg, :] = hn[seg * s:seg * (s + 1), LANES * k:LANES * (k + 1)]

    def perm_rows(j):
        return jnp.concatenate(
            [slabs[k, pl.ds(j, SUBLANES, stride=pitch), :] for k in range(nslab)], axis=1)

    for jj in range(seg // 2):
        hbuf[BF16_ROWS * jj:BF16_ROWS * (jj + 1), :] = jnp.concatenate(
            [perm_rows(2 * jj), perm_rows(2 * jj + 1)], axis=0).astype(BF16)
    row_x = lax.broadcasted_iota(jnp.int32, (SUBLANES, D_MODEL), 0)
    h_prev = jnp.where(blk == 0, 0.0, pltpu.roll(_rmsnorm(prev_ref[...] + dprev_ref[...], g), 1, 0))
    h_next = jnp.where(blk == nblk - 1, 0.0,
                       pltpu.roll(_rmsnorm(next_ref[...] + dnext_ref[...], g), 1, 0))
    halo = jnp.where(row_x == 0, h_prev, jnp.where(row_x == 1, h_next, 0.0))
    hbuf[tb:, :] = jnp.concatenate([halo, jnp.zeros_like(halo)], axis=0).astype(BF16)

    row_u = lax.broadcasted_iota(jnp.int32, (SUBLANES, 2 * sub), 0)

    def pair(ref, sc):
        return jnp.concatenate([ref[:, sub * sc:sub * (sc + 1)],
                                ref[:, D_FF + sub * sc:D_FF + sub * (sc + 1)]], axis=1)

    def up(sc):
        slot = sc % nring
        res = jnp.concatenate([_dot(hbuf[...], wu_ref[:, sub * sc:sub * (sc + 1)]),
                               _dot(hbuf[...], wu_ref[:, D_FF + sub * sc:D_FF + sub * (sc + 1)])], axis=1)
        uvbuf[slot, SUBLANES:SUBLANES + tb, :] = res[0:tb]
        uvbuf[slot, 0:SUBLANES, :] = jnp.where(
            row_u == 0, jnp.broadcast_to(res[tb:tb + 1], row_u.shape),
            pltpu.roll(res[tb - SUBLANES:tb], 1, 0))
        uvbuf[slot, SUBLANES + tb:, :] = jnp.where(
            row_u == SUBLANES - 1, jnp.broadcast_to(res[tb + 1:tb + 2], row_u.shape),
            pltpu.roll(res[0:SUBLANES], SUBLANES - 1, 0))

    def gate(sc):
        slot = sc % nring
        cw = pair(cw_ref, sc)
        c = pair(cb_ref, sc) + uvbuf[slot, 0:tb, :] * cw[0:1]
        c = c + uvbuf[slot, SUBLANES:SUBLANES + tb, :] * cw[1:2]
        c = c + uvbuf[slot, 2 * SUBLANES:2 * SUBLANES + tb, :] * cw[2:3]
        return _gelu_gate(c[:, :sub], c[:, sub:]).astype(BF16)

    up(0)
    up(1)
    for sc in range(nsub):
        if sc + 2 < nsub:
            up(sc + 2)
        acts[:, sub * sc:sub * (sc + 1)] = gate(sc)

    acc[...] = _dot(acts[...], wd_ref[...])

    for j in range(seg):
        for k in range(nslab):
            slabs[k, pl.ds(j, SUBLANES, stride=pitch), :] = acc[SUBLANES * j:SUBLANES * (j + 1),
                                                                LANES * k:LANES * (k + 1)]
    ffn = jnp.concatenate(
        [jnp.concatenate([slabs[k, pitch * s:pitch * s + seg, :] for s in range(SUBLANES)], axis=0)
         for k in range(nslab)], axis=1)
    y = (x_ref[...] + d_ref[...]) + ffn
    if final:
        y = _rmsnorm(y, fg_ref[...])
    y_ref[...] = y


def _ffn(x2, d2, g, w_up, cw, cb, w_down, fg, *, S, final):
    T = x2.shape[0]
    tb = TB_FFN
    nblk = S // tb
    hpb = tb // SUBLANES
    n_halo = T // SUBLANES
    full = lambda shape: pl.BlockSpec(shape, lambda i: (0,) * len(shape))
    tile = pl.BlockSpec((tb, D_MODEL), lambda i: (i, 0))
    halo_prev = pl.BlockSpec((SUBLANES, D_MODEL), lambda i: (jnp.maximum(i * hpb - 1, 0), 0))
    halo_next = pl.BlockSpec((SUBLANES, D_MODEL), lambda i: (jnp.minimum((i + 1) * hpb, n_halo - 1), 0))
    kern = functools.partial(_ffn_kernel, nblk=nblk, tb=tb, final=final)
    return pl.pallas_call(
        kern,
        grid=(T // tb,),
        in_specs=[
            tile, halo_prev, halo_next, tile, halo_prev, halo_next,
            full((1, D_MODEL)), full((D_MODEL, 2 * D_FF)), full((3, 2 * D_FF)), full((1, 2 * D_FF)),
            full((D_FF, D_MODEL)), full((1, D_MODEL)),
        ],
        out_specs=pl.BlockSpec((tb, D_MODEL), lambda i: (i, 0)),
        out_shape=jax.ShapeDtypeStruct((T, D_MODEL), F32),
        scratch_shapes=[
            pltpu.VMEM((D_MODEL // LANES, tb + SUBLANES * SUBLANES, LANES), F32),
            pltpu.VMEM((tb + BF16_ROWS, D_MODEL), BF16),
            pltpu.VMEM((FF_RING, tb + 2 * SUBLANES, 2 * FF_SUB), F32),
            pltpu.VMEM((tb, D_FF), BF16),
            pltpu.VMEM((tb, D_MODEL), F32),
        ],
        compiler_params=pltpu.CompilerParams(dimension_semantics=("parallel",),
                                             vmem_limit_bytes=VMEM_LIMIT_FFN),
        name="convffn",
    )(x2, x2, x2, d2, d2, d2, g, w_up, cw, cb, w_down, fg)


def _block_diag(w):
    eye = jnp.eye(RG_BLOCKS, dtype=w.dtype)
    return jnp.einsum('ncd,nm->ncmd', w, eye).reshape(D_RG, D_RG)


def _encoder(x, norm1_g, w_in, b_gates, rg_conv_w, rg_conv_b, rg_wa, rg_ba, rg_wx, rg_bx, rg_lambda,
             ml_norm_g, w_out, norm2_g, w_up, ffn_conv_w, ffn_conv_b, w_down, final_g):
    B, S, _ = x.shape
    T = B * S
    depth = w_in.shape[0]
    x2 = x.reshape(T, D_MODEL)
    row = lambda v: v.reshape(1, -1).astype(F32)
    n_nat = 2 * D_RG + 2 * D_ML
    for l in range(depth):
        w_nat = w_in[l, :, :n_nat].astype(BF16)
        w_tr = w_in[l, :, n_nat:n_nat + 2 * D_ML].T.astype(BF16)
        w_gate = w_in[l, :, n_nat + 2 * D_ML:].T.astype(BF16)
        bias = jnp.broadcast_to(b_gates[l].astype(F32).reshape(N_GATE, 1), (N_GATE, ML_CHUNK))
        rx, rgate, qk, v_t, o_t, gr, uc = _inproj(x2, row(norm1_g[l]), w_nat, w_tr, w_gate, bias,
                                                  B=B, S=S)
        qk3 = qk.reshape(B, S, 2 * D_ML)
        gr4 = gr.reshape(B, S // ML_CHUNK, 2 * G_ROWS, ML_CHUNK)
        uc3 = uc.reshape(B, S, LANES)
        rg_gates = [(_block_diag(rg_wa[l, d]).astype(BF16), _block_diag(rg_wx[l, d]).astype(BF16),
                     row(rg_ba[l, d]), row(rg_bx[l, d]), row(rg_lambda[l, d])) for d in range(2)]
        r_dir = _rglru(rx, rg_conv_w[l].astype(F32), row(rg_conv_b[l]), *rg_gates, B=B, S=S)
        m_dir = [_mlstm(qk3, v_t, gr4, uc3, reverse=reverse) for reverse in (False, True)]
        wo = w_out[l].astype(BF16)
        mg_tile = jnp.broadcast_to(ml_norm_g[l].astype(F32).reshape(D_ML, 1), (D_ML, LANES))
        mixed = _outproj(r_dir[0], r_dir[1], rgate, m_dir[0], m_dir[1], o_t, mg_tile,
                         wo[:D_RG], wo[D_RG:], S=S)
        x2 = _ffn(x2, mixed, row(norm2_g[l]), w_up[l].astype(BF16),
                  ffn_conv_w[l].astype(F32), row(ffn_conv_b[l]),
                  w_down[l].astype(BF16), row(final_g), S=S,
                  final=(l == depth - 1))
    return x2.reshape(B, S, D_MODEL)


def kernel(x_prompt, x_sample, norm1_g, w_in, b_gates, rg_conv_w, rg_conv_b, rg_wa, rg_ba, rg_wx, rg_bx,
           rg_lambda, ml_norm_g, w_out, norm2_g, w_up, ffn_conv_w, ffn_conv_b, w_down, final_g):
    weights = (norm1_g, w_in, b_gates, rg_conv_w, rg_conv_b, rg_wa, rg_ba, rg_wx, rg_bx, rg_lambda,
               ml_norm_g, w_out, norm2_g, w_up, ffn_conv_w, ffn_conv_b, w_down, final_g)
    return (_encoder(x_prompt, *weights), _encoder(x_sample, *weights))
```

```python
import functools

import jax
import jax.numpy as jnp
from jax import lax
from jax.experimental import pallas as pl
from jax.experimental.pallas import tpu as pltpu

F32 = jnp.float32
BF16 = jnp.bfloat16

D_MODEL = 1024
D_RG = 512
D_ML = 512
RG_BLOCKS = 8
RG_C = 8.0
ML_HEADS = 4
ML_HD = 128
ML_CHUNK = 128
D_FF = 3072
EPS = 1e-6
N_GATE = 4 * ML_HEADS

SUBLANES = 8
LANES = 128
BF16_ROWS = 16
VMEM_LIMIT = 48 * 1024 * 1024

TM_PROJ = 1024
TM_SUB = 512
TB_RG = 1024
ML_SEQS = 16
TB_FFN = 512
FF_SUB = 256
FF_RING = 4
VMEM_LIMIT_FFN = 56 * 1024 * 1024

G_B, G_U, G_A, G_BL, G_GM = 0, 8, 16, 24, 32
G_ROWS = 40
UC_DIR = 24
UC_ONES = 2 * UC_DIR


def _params(*sem):
    return pltpu.CompilerParams(dimension_semantics=sem, vmem_limit_bytes=VMEM_LIMIT)


def _softplus(z):
    return jnp.maximum(z, 0.0) + jnp.log1p(jnp.exp(-jnp.abs(z)))


def _gelu_gate(gate, val):
    k0 = -2.0 * 0.7978845608028654 * 1.4426950408889634
    z = gate * (k0 + (k0 * 0.044715) * (gate * gate))
    return (gate * val) / (1.0 + jnp.exp2(z))


def _rmsnorm(x, g):
    return x * lax.rsqrt(jnp.mean(x * x, axis=-1, keepdims=True) + EPS) * g


def _dot(a, b):
    return jnp.dot(a, b, preferred_element_type=F32)


def _dot_nt(a, b):
    return lax.dot_general(a, b, (((1,), (1,)), ((), ())), preferred_element_type=F32)


def _split3(x):
    hi = x.astype(BF16).astype(F32)
    r1 = x - hi
    mid = r1.astype(BF16).astype(F32)
    return hi, mid, (r1 - mid).astype(BF16).astype(F32)


def _lane_scan(x, op, fill, reverse):
    n = x.shape[-1]
    lane = lax.broadcasted_iota(jnp.int32, x.shape, 1)
    s = 1
    while s < n:
        if reverse:
            x = op(x, jnp.where(lane < n - s, pltpu.roll(x, n - s, 1), fill))
        else:
            x = op(x, jnp.where(lane >= s, pltpu.roll(x, s, 1), fill))
        s *= 2
    return x


def _inproj_kernel(x_ref, g_ref, wn_ref, wt_ref, wg_ref, bias_ref, rx_ref, rg_ref, qk_ref, vt_ref,
                   ot_ref, gr_ref, uc_ref):
    for r0 in range(0, x_ref.shape[0], TM_SUB):
        _inproj_piece(x_ref, g_ref, wn_ref, wt_ref, wg_ref, bias_ref, rx_ref, rg_ref, qk_ref, vt_ref,
                      ot_ref, gr_ref, uc_ref, r0)


def _inproj_piece(x_ref, g_ref, wn_ref, wt_ref, wg_ref, bias_ref, rx_ref, rg_ref, qk_ref, vt_ref,
                  ot_ref, gr_ref, uc_ref, r0):
    L = ML_CHUNK
    NH = ML_HEADS
    rows = slice(r0, r0 + TM_SUB)
    h = _rmsnorm(x_ref[rows, :], g_ref[...]).astype(BF16)

    gt = _dot_nt(wg_ref[...], h)
    rowid = lax.broadcasted_iota(jnp.int32, (2 * NH, L), 0)
    head_row = rowid < NH
    rep = lambda col: jnp.broadcast_to(col, (2 * NH, L))
    zeros8 = jnp.zeros((2 * NH, L), F32)
    for cl in range(TM_SUB // L):
        c = r0 // L + cl
        g16 = gt[:, cl * L:(cl + 1) * L] + bias_ref[...]
        tiles = []
        for d, reverse in enumerate((False, True)):
            gates = g16[2 * NH * d:2 * NH * (d + 1)]
            lf = jnp.where(head_row, 0.0, -_softplus(-gates))
            bcum = pltpu.roll(_lane_scan(lf, jnp.add, 0.0, reverse), NH, 0)
            u = jnp.where(head_row, gates - bcum, 0.0)
            last = 0 if reverse else L - 1
            bl = rep(bcum[:, last:last + 1])
            base = G_ROWS * d
            gr_ref[c, base + G_B:base + G_B + 8, :] = bcum
            gr_ref[c, base + G_U:base + G_U + 8, :] = u
            gr_ref[c, base + G_A:base + G_A + 8, :] = bcum + _lane_scan(u, jnp.maximum, -jnp.inf, reverse)
            gr_ref[c, base + G_BL:base + G_BL + 8, :] = bl
            gr_ref[c, base + G_GM:base + G_GM + 8, :] = rep(jnp.max(bl + u, axis=-1, keepdims=True))
            tiles.extend(_split3(u))
        tiles.append(jnp.ones((2 * NH, L), F32))
        tiles.extend([zeros8] * (L // 8 - len(tiles)))
        uc_ref[c * L:(c + 1) * L, :] = jnp.concatenate(tiles, axis=0).T.astype(BF16)

    nat = _dot(h, wn_ref[...])
    rx_ref[rows, :] = nat[:, :D_RG]
    rg_ref[rows, :] = nat[:, D_RG:2 * D_RG].astype(BF16)
    qk_ref[rows, :] = jnp.concatenate(
        [nat[:, 2 * D_RG:2 * D_RG + D_ML] * (ML_HD ** -0.5), nat[:, 2 * D_RG + D_ML:]], axis=-1).astype(BF16)
    tr = _dot_nt(wt_ref[...], h)
    vt_ref[:, rows] = tr[:D_ML].astype(BF16)
    ot_ref[:, rows] = tr[D_ML:].astype(BF16)


def _inproj(x2, g, w_nat, w_tr, w_gate, bias, *, B, S):
    T = B * S
    tm = TM_PROJ
    nb = S // tm
    full = lambda shape: pl.BlockSpec(shape, lambda i: (0,) * len(shape))
    tok = lambda width: pl.BlockSpec((tm, width), lambda i: (i, 0))
    seq_t = pl.BlockSpec((None, D_ML, tm), lambda i: (i // nb, 0, i % nb))
    return pl.pallas_call(
        _inproj_kernel,
        grid=(T // tm,),
        in_specs=[
            tok(D_MODEL), full((1, D_MODEL)), full(w_nat.shape), full(w_tr.shape), full(w_gate.shape),
            full((N_GATE, ML_CHUNK)),
        ],
        out_specs=[
            tok(D_RG), tok(D_RG), tok(2 * D_ML), seq_t, seq_t,
            pl.BlockSpec((tm // ML_CHUNK, 2 * G_ROWS, ML_CHUNK), lambda i: (i, 0, 0)),
            tok(LANES),
        ],
        out_shape=[
            jax.ShapeDtypeStruct((T, D_RG), F32),
            jax.ShapeDtypeStruct((T, D_RG), BF16),
            jax.ShapeDtypeStruct((T, 2 * D_ML), BF16),
            jax.ShapeDtypeStruct((B, D_ML, S), BF16),
            jax.ShapeDtypeStruct((B, D_ML, S), BF16),
            jax.ShapeDtypeStruct((T // ML_CHUNK, 2 * G_ROWS, ML_CHUNK), F32),
            jax.ShapeDtypeStruct((T, LANES), BF16),
        ],
        compiler_params=_params("parallel"),
        name="inproj",
    )(x2, g, w_nat, w_tr, w_gate, bias)


def _rglru_fwd_kernel(x_ref, prev_ref, next_ref, cw_ref, cb_ref, wa_ref, wx_ref, ba_ref, bx_ref,
                      lam_ref, h_ref, xc_ref, slabs, xbuf, pbuf, lbuf, carry, *, nblk, tb):
    blk = pl.program_id(1)

    @pl.when(blk == 0)
    def _():
        carry[...] = jnp.zeros_like(carry)

    npiece = x_ref.shape[0] // tb
    for p in range(npiece):
        rows = pl.ds(tb * p, tb)
        prev = (jnp.where(blk == 0, 0.0, prev_ref[...]) if p == 0
                else x_ref[tb * p - SUBLANES:tb * p, :])
        nxt = (jnp.where(blk == nblk - 1, 0.0, next_ref[...]) if p == npiece - 1
               else x_ref[tb * (p + 1):tb * (p + 1) + SUBLANES, :])
        xc = _rglru_conv(x_ref[rows, :], prev, nxt, cw_ref, cb_ref, slabs.at[p], xbuf.at[p], tb=tb)
        xc_ref[rows, :] = xc
        _rglru_scan(xc, wa_ref, wx_ref, ba_ref, bx_ref, lam_ref, h_ref.at[rows], slabs.at[p],
                    pbuf.at[p], lbuf.at[p], carry, reverse=False, tb=tb)


def _rglru_bwd_kernel(xc_ref, wa_ref, wx_ref, ba_ref, bx_ref, lam_ref, h_ref, slabs, pbuf, lbuf,
                      carry, *, tb):
    @pl.when(pl.program_id(1) == 0)
    def _():
        carry[...] = jnp.zeros_like(carry)

    for p in reversed(range(xc_ref.shape[0] // tb)):
        rows = pl.ds(tb * p, tb)
        _rglru_scan(xc_ref[rows, :], wa_ref, wx_ref, ba_ref, bx_ref, lam_ref, h_ref.at[rows],
                    slabs.at[p], pbuf.at[p], lbuf.at[p], carry, reverse=True, tb=tb)


def _rglru_conv(x, prev, nxt, cw_ref, cb_ref, slabs, xbuf, *, tb):
    nslab = D_RG // LANES
    seg = tb // SUBLANES
    pitch = seg + SUBLANES
    X0 = 2 * SUBLANES

    for k in range(nslab):
        for s in range(SUBLANES):
            slabs[k, pitch * s:pitch * s + seg, :] = x[seg * s:seg * (s + 1), LANES * k:LANES * (k + 1)]
    seam = {}
    for j in range(seg):
        rows = jnp.concatenate(
            [slabs[k, pl.ds(j, SUBLANES, stride=pitch), :] for k in range(nslab)], axis=1)
        xbuf[X0 + SUBLANES * j:X0 + SUBLANES * (j + 1), :] = rows
        if j in (0, seg - 2, seg - 1):
            seam[j] = rows
    row = lax.broadcasted_iota(jnp.int32, (SUBLANES, D_RG), 0)
    tile_row = lambda v, i: jnp.broadcast_to(v[i:i + 1, :], (SUBLANES, D_RG))
    xbuf[0:SUBLANES, :] = jnp.where(row == 0, tile_row(prev, SUBLANES - 2), pltpu.roll(seam[seg - 2], 1, 0))
    xbuf[SUBLANES:X0, :] = jnp.where(row == 0, tile_row(prev, SUBLANES - 1), pltpu.roll(seam[seg - 1], 1, 0))
    xbuf[X0 + tb:, :] = jnp.where(row == SUBLANES - 1, tile_row(nxt, 0),
                                  pltpu.roll(seam[0], SUBLANES - 1, 0))
    xc = cb_ref[...] + xbuf[0:tb, :] * cw_ref[0:1, :]
    xc = xc + xbuf[SUBLANES:SUBLANES + tb, :] * cw_ref[1:2, :]
    xc = xc + xbuf[X0:X0 + tb, :] * cw_ref[2:3, :]
    return xc + xbuf[X0 + SUBLANES:X0 + SUBLANES + tb, :] * cw_ref[3:4, :]


def _rglru_scan(xc, wa_ref, wx_ref, ba_ref, bx_ref, lam_ref, h_ref, slabs, pbuf, lbuf, carry, *,
                reverse, tb):
    nslab = D_RG // LANES
    seg = tb // SUBLANES
    pitch = seg + SUBLANES
    row = lax.broadcasted_iota(jnp.int32, (SUBLANES, D_RG), 0)
    tile_row = lambda v, i: jnp.broadcast_to(v[i:i + 1, :], (SUBLANES, D_RG))

    xcb = xc.astype(BF16)
    r = jax.nn.sigmoid(_dot(xcb, wa_ref[...]) + ba_ref[...])
    i = jax.nn.sigmoid(_dot(xcb, wx_ref[...]) + bx_ref[...])
    decay_rate = RG_C * _softplus(-lam_ref[...])
    a = jnp.exp2((decay_rate * -1.4426950408889634) * r)
    y = jnp.tanh(decay_rate * r) * (a * a + 1.0)
    u = jnp.where(y > 0.0, y * lax.rsqrt(y), 0.0) * (i * xc)

    order = range(seg - 1, -1, -1) if reverse else range(seg)
    P = L = None
    for j in order:
        rs = slice(SUBLANES * j, SUBLANES * (j + 1))
        if P is None:
            P, L = a[rs], u[rs]
        else:
            P, L = a[rs] * P, a[rs] * L + u[rs]
        pbuf[rs, :] = P
        lbuf[rs, :] = L

    A, U = P, L
    for s in (1, 2, 4):
        if reverse:
            keep = row < SUBLANES - s
            shift = SUBLANES - s
        else:
            keep = row >= s
            shift = s
        a_sh = jnp.where(keep, pltpu.roll(A, shift, 0), 1.0)
        u_sh = jnp.where(keep, pltpu.roll(U, shift, 0), 0.0)
        U = A * u_sh + U
        A = A * a_sh
    c_in = carry[...]
    e = U + A * c_in
    if reverse:
        c_seg = jnp.where(row == SUBLANES - 1, c_in, pltpu.roll(e, SUBLANES - 1, 0))
        carry[...] = tile_row(e, 0)
    else:
        c_seg = jnp.where(row == 0, c_in, pltpu.roll(e, 1, 0))
        carry[...] = tile_row(e, SUBLANES - 1)

    for j in range(seg):
        rs = slice(SUBLANES * j, SUBLANES * (j + 1))
        hj = lbuf[rs, :] + pbuf[rs, :] * c_seg
        for k in range(nslab):
            slabs[k, pl.ds(j, SUBLANES, stride=pitch), :] = hj[:, LANES * k:LANES * (k + 1)]
    for k in range(nslab):
        for s in range(SUBLANES):
            h_ref[seg * s:seg * (s + 1), LANES * k:LANES * (k + 1)] = (
                slabs[k, pitch * s:pitch * s + seg, :].astype(h_ref.dtype))


def _rglru(x, cw, cb, gates_fwd, gates_bwd, *, B, S):
    T = B * S
    tile = TB_RG
    tb = TM_SUB
    npiece = tile // tb
    nblk = S // tile
    hb = tile // SUBLANES
    n_halo = T // SUBLANES
    full = lambda shape: pl.BlockSpec(shape, lambda b, j: (0,) * len(shape))
    gate_specs = [full((D_RG, D_RG)), full((D_RG, D_RG)), full((1, D_RG)), full((1, D_RG)), full((1, D_RG))]
    slabs = pltpu.VMEM((npiece, D_RG // LANES, tb + SUBLANES * SUBLANES, LANES), F32)
    scan_scratch = [
        pltpu.VMEM((npiece, tb, D_RG), F32),
        pltpu.VMEM((npiece, tb, D_RG), F32),
        pltpu.VMEM((SUBLANES, D_RG), F32),
    ]
    tile_fwd = pl.BlockSpec((tile, D_RG), lambda b, j: (b * nblk + j, 0))
    tile_bwd = pl.BlockSpec((tile, D_RG), lambda b, j: (b * nblk + nblk - 1 - j, 0))

    h_fwd, xc = pl.pallas_call(
        functools.partial(_rglru_fwd_kernel, nblk=nblk, tb=tb),
        grid=(B, nblk),
        in_specs=[
            tile_fwd,
            pl.BlockSpec((SUBLANES, D_RG), lambda b, j: (jnp.maximum((b * nblk + j) * hb - 1, 0), 0)),
            pl.BlockSpec((SUBLANES, D_RG), lambda b, j: (jnp.minimum((b * nblk + j + 1) * hb, n_halo - 1), 0)),
            full((4, D_RG)), full((1, D_RG)), *gate_specs,
        ],
        out_specs=[tile_fwd, tile_fwd],
        out_shape=[jax.ShapeDtypeStruct((T, D_RG), BF16),
                   jax.ShapeDtypeStruct((T, D_RG), F32)],
        scratch_shapes=[slabs, pltpu.VMEM((npiece, tb + 3 * SUBLANES, D_RG), F32), *scan_scratch],
        compiler_params=_params("parallel", "arbitrary"),
        name="rglru_fwd",
    )(x, x, x, cw, cb, *gates_fwd)
    h_bwd = pl.pallas_call(
        functools.partial(_rglru_bwd_kernel, tb=tb),
        grid=(B, nblk),
        in_specs=[tile_bwd, *gate_specs],
        out_specs=tile_bwd,
        out_shape=jax.ShapeDtypeStruct((T, D_RG), BF16),
        scratch_shapes=[slabs, *scan_scratch],
        compiler_params=_params("parallel", "arbitrary"),
        name="rglru_bwd",
    )(xc, *gates_bwd)
    return h_fwd, h_bwd


def _mlstm_kernel(q_ref, k_ref, vt_ref, gr_ref, uc_ref, h_ref, ct_st, n_st, m_st, *, reverse, bb):
    L = ML_CHUNK
    NH = ML_HEADS
    d = 1 if reverse else 0

    @pl.when(pl.program_id(1) == 0)
    def _():
        ct_st[...] = jnp.zeros_like(ct_st)
        n_st[...] = jnp.zeros_like(n_st)
        m_st[...] = jnp.zeros_like(m_st)

    row8 = lax.broadcasted_iota(jnp.int32, (SUBLANES, L), 0)
    s_id = lax.broadcasted_iota(jnp.int32, (L, L), 0)
    t_id = lax.broadcasted_iota(jnp.int32, (L, L), 1)
    valid = (s_id >= t_id) if reverse else (s_id <= t_id)
    zeros8 = jnp.zeros((SUBLANES, L), F32)
    tile = lambda rows, hd: jnp.broadcast_to(rows[hd:hd + 1, :], (L, L))
    row_of = lambda rows, hd: jnp.broadcast_to(rows[hd:hd + 1, :], (SUBLANES, L))
    pairs = [(b, hp) for b in range(bb) for hp in range(NH // 2)]

    seq = []
    for b in range(bb):
        base = G_ROWS * d
        bcum = gr_ref[b, base + G_B:base + G_B + 8, :]
        u = gr_ref[b, base + G_U:base + G_U + 8, :]
        b_last = gr_ref[b, base + G_BL:base + G_BL + 8, :]
        m_prev = m_st[b]
        inter = bcum + m_prev
        m_t = jnp.maximum(inter, gr_ref[b, base + G_A:base + G_A + 8, :])
        m_new = jnp.maximum(b_last + m_prev, gr_ref[b, base + G_GM:base + G_GM + 8, :])
        n_prev = n_st[b]
        wg = jnp.exp(u + (b_last - m_new))
        seq.append(dict(
            w_int=jnp.exp(inter - m_t), e_neg=jnp.exp(-m_t), v3=_split3(bcum - m_t),
            decay=jnp.exp(b_last + m_prev - m_new), m_new=m_new, n_prev=n_prev, wg=wg,
            n_lhs=jnp.concatenate([n_prev, zeros8], axis=0).astype(BF16),
            wg_lhs=jnp.concatenate([wg, zeros8], axis=0).astype(BF16)))

    cs = lambda hd: slice(hd * ML_HD, (hd + 1) * ML_HD)
    side = lambda i: slice(i * L, (i + 1) * L)
    twice = lambda x: jnp.concatenate([x, x], axis=1)

    def blockdiag(a0, a1):
        z = jnp.zeros_like(a0)
        return jnp.concatenate([jnp.concatenate([a0, z], axis=1), jnp.concatenate([z, a1], axis=1)], axis=0)

    def expo_rhs(b, hd):
        onehot = jnp.where(row8 == hd, 1.0, 0.0)
        v_hi, v_mid, v_lo = (row_of(x, hd) for x in seq[b]["v3"])
        v_rows = jnp.where(row8 == 0, v_hi, jnp.where(row8 == 1, v_mid, jnp.where(row8 == 2, v_lo, 0.0)))
        slabs = [zeros8] * (L // SUBLANES)
        for j in range(3):
            slabs[(UC_DIR * d) // SUBLANES + j] = onehot
        slabs[UC_ONES // SUBLANES] = v_rows
        return jnp.concatenate(slabs, axis=0).astype(BF16)

    st, expo, inter, upd = {}, {}, {}, {}
    for b, hp in pairs:
        sq = seq[b]
        heads = (2 * hp, 2 * hp + 1)
        q_diag = blockdiag(q_ref[b, :, cs(heads[0])], q_ref[b, :, cs(heads[1])])
        st[b, hp] = _dot_nt(
            jnp.concatenate([k_ref[b, :, heads[0] * ML_HD:(heads[1] + 1) * ML_HD], twice(sq["n_lhs"])],
                            axis=0), q_diag)
        expo[b, hp] = _dot(uc_ref[b], jnp.concatenate([expo_rhs(b, hd) for hd in heads], axis=1))
        inter[b, hp] = _dot_nt(jnp.concatenate([ct_st[b, hd].astype(BF16) for hd in heads], axis=1), q_diag)
        vw = jnp.concatenate([(vt_ref[b, cs(hd), :].astype(F32) * sq["wg"][hd:hd + 1, :]).astype(BF16)
                              for hd in heads], axis=1)
        upd[b, hp] = _dot(jnp.concatenate([vw, twice(sq["wg_lhs"])], axis=0),
                          blockdiag(k_ref[b, :, cs(heads[0])], k_ref[b, :, cs(heads[1])]))

    for b, hp in pairs:
        sq = seq[b]
        for i, hd in enumerate((2 * hp, 2 * hp + 1)):
            ct_st[b, hd] = tile(sq["decay"], hd) * ct_st[b, hd] + upd[b, hp][0:L, side(i)]
            n_st[b, hd:hd + 1, :] = (sq["decay"][hd:hd + 1, :] * sq["n_prev"][hd:hd + 1, :]
                                     + upd[b, hp][L + hd:L + hd + 1, side(i)])
    for b in range(bb):
        m_st[b] = seq[b]["m_new"]

    for b, hp in pairs:
        sq = seq[b]
        heads = (2 * hp, 2 * hp + 1)
        p_t, rw = [], []
        for i, hd in enumerate(heads):
            s_t = st[b, hp][0:L, side(i)] * jnp.exp(jnp.where(valid, expo[b, hp][:, side(i)], -jnp.inf))
            w_h = sq["w_int"][hd:hd + 1, :]
            den = jnp.sum(s_t, axis=0, keepdims=True) + w_h * st[b, hp][L + hd:L + hd + 1, side(i)]
            r = 1.0 / jnp.maximum(jnp.abs(den), sq["e_neg"][hd:hd + 1, :])
            p_t.append((s_t * r).astype(BF16))
            rw.append(r * w_h)
        intra = _dot(jnp.concatenate([vt_ref[b, cs(hd), :] for hd in heads], axis=1), blockdiag(*p_t))
        out = (intra + inter[b, hp] * jnp.concatenate(rw, axis=1)).astype(h_ref.dtype)
        for i, hd in enumerate(heads):
            h_ref[b, cs(hd), :] = out[:, side(i)]


def _mlstm(qk3, vt, gr4, uc3, *, reverse):
    B, S, _ = qk3.shape
    L = ML_CHUNK
    nc = S // L
    bb = ML_SEQS

    def chunk(c):
        return (nc - 1 - c) if reverse else c

    kern = functools.partial(_mlstm_kernel, reverse=reverse, bb=bb)
    return pl.pallas_call(
        kern,
        grid=(B // bb, nc),
        in_specs=[
            pl.BlockSpec((bb, L, D_ML), lambda b, c: (b, chunk(c), 0)),
            pl.BlockSpec((bb, L, D_ML), lambda b, c: (b, chunk(c), 1)),
            pl.BlockSpec((bb, D_ML, L), lambda b, c: (b, 0, chunk(c))),
            pl.BlockSpec((bb, None, 2 * G_ROWS, L), lambda b, c: (b, chunk(c), 0, 0)),
            pl.BlockSpec((bb, L, LANES), lambda b, c: (b, chunk(c), 0)),
        ],
        out_specs=pl.BlockSpec((bb, D_ML, L), lambda b, c: (b, 0, chunk(c))),
        out_shape=jax.ShapeDtypeStruct((B, D_ML, S), BF16),
        scratch_shapes=[
            pltpu.VMEM((bb, ML_HEADS, ML_HD, ML_HD), F32),
            pltpu.VMEM((bb, 2 * ML_HEADS, ML_HD), F32),
            pltpu.VMEM((bb, 2 * ML_HEADS, LANES), F32),
        ],
        compiler_params=_params("parallel", "arbitrary"),
        name="mlstm_bwd" if reverse else "mlstm_fwd",
    )(qk3, qk3, vt, gr4, uc3)


def _outproj_kernel(rf_ref, rb_ref, gate_ref, mf_ref, mb_ref, ot_ref, mg_ref, wr_ref, wm_ref, y_ref):
    for r0 in range(0, y_ref.shape[0], TM_SUB):
        rows = slice(r0, r0 + TM_SUB)
        y_rg = _gelu_gate(gate_ref[rows, :].astype(F32),
                          rf_ref[rows, :].astype(F32) + rb_ref[rows, :].astype(F32))
        acc = _dot(y_rg.astype(BF16), wr_ref[...])
        h_t = mf_ref[:, rows].astype(F32) + mb_ref[:, rows].astype(F32)
        parts = []
        for hd in range(ML_HEADS):
            hh = h_t[hd * ML_HD:(hd + 1) * ML_HD]
            parts.append(hh * lax.rsqrt(jnp.mean(hh * hh, axis=0, keepdims=True) + EPS))
        mg = jnp.tile(mg_ref[...], (1, TM_SUB // LANES))
        y_t = jax.nn.sigmoid(ot_ref[:, rows].astype(F32)) * (jnp.concatenate(parts, axis=0) * mg)
        y_ref[rows, :] = acc + _dot(y_t.T.astype(BF16), wm_ref[...])


def _outproj(rf, rb, pa, mf_t, mb_t, o_t, mg_tile, w_rg, w_ml, *, S):
    T = rf.shape[0]
    tm = TM_PROJ
    nb = S // tm
    tok = lambda width, col: pl.BlockSpec((tm, width), lambda i: (i, col))
    seq_t = pl.BlockSpec((None, D_ML, tm), lambda i: (i // nb, 0, i % nb))
    full = lambda shape: pl.BlockSpec(shape, lambda i: (0,) * len(shape))
    return pl.pallas_call(
        _outproj_kernel,
        grid=(T // tm,),
        in_specs=[
            tok(D_RG, 0), tok(D_RG, 0), tok(D_RG, 0),
            seq_t, seq_t, seq_t,
            full((D_ML, LANES)), full((D_RG, D_MODEL)), full((D_ML, D_MODEL)),
        ],
        out_specs=tok(D_MODEL, 0),
        out_shape=jax.ShapeDtypeStruct((T, D_MODEL), F32),
        compiler_params=_params("parallel"),
        name="outproj",
    )(rf, rb, pa, mf_t, mb_t, o_t, mg_tile, w_rg, w_ml)


def _ffn_kernel(x_ref, prev_ref, next_ref, d_ref, dprev_ref, dnext_ref, g_ref, wu_ref, cw_ref, cb_ref,
                wd_ref, fg_ref, y_ref, slabs, hbuf, uvbuf, acts, acc, *, nblk, tb, final):
    blk = pl.program_id(0) % nblk
    sub = FF_SUB
    nsub = D_FF // sub
    nring = uvbuf.shape[0]
    nslab = D_MODEL // LANES
    seg = tb // SUBLANES
    pitch = seg + SUBLANES

    g = g_ref[...]
    hn = _rmsnorm(x_ref[...] + d_ref[...], g)
    for k in range(nslab):
        for s in range(SUBLANES):
            slabs[k, pitch * s:pitch * s + seg, :] = hn[seg * s:seg * (s + 1), LANES * k:LANES * (k + 1)]

    def perm_rows(j):
        return jnp.concatenate(
            [slabs[k, pl.ds(j, SUBLANES, stride=pitch), :] for k in range(nslab)], axis=1)

    for jj in range(seg // 2):
        hbuf[BF16_ROWS * jj:BF16_ROWS * (jj + 1), :] = jnp.concatenate(
            [perm_rows(2 * jj), perm_rows(2 * jj + 1)], axis=0).astype(BF16)
    row_x = lax.broadcasted_iota(jnp.int32, (SUBLANES, D_MODEL), 0)
    h_prev = jnp.where(blk == 0, 0.0, pltpu.roll(_rmsnorm(prev_ref[...] + dprev_ref[...], g), 1, 0))
    h_next = jnp.where(blk == nblk - 1, 0.0,
                       pltpu.roll(_rmsnorm(next_ref[...] + dnext_ref[...], g), 1, 0))
    halo = jnp.where(row_x == 0, h_prev, jnp.where(row_x == 1, h_next, 0.0))
    hbuf[tb:, :] = jnp.concatenate([halo, jnp.zeros_like(halo)], axis=0).astype(BF16)

    row_u = lax.broadcasted_iota(jnp.int32, (SUBLANES, 2 * sub), 0)

    def pair(ref, sc):
        return jnp.concatenate([ref[:, sub * sc:sub * (sc + 1)],
                                ref[:, D_FF + sub * sc:D_FF + sub * (sc + 1)]], axis=1)

    def up(sc):
        slot = sc % nring
        res = jnp.concatenate([_dot(hbuf[...], wu_ref[:, sub * sc:sub * (sc + 1)]),
                               _dot(hbuf[...], wu_ref[:, D_FF + sub * sc:D_FF + sub * (sc + 1)])], axis=1)
        uvbuf[slot, SUBLANES:SUBLANES + tb, :] = res[0:tb]
        uvbuf[slot, 0:SUBLANES, :] = jnp.where(
            row_u == 0, jnp.broadcast_to(res[tb:tb + 1], row_u.shape),
            pltpu.roll(res[tb - SUBLANES:tb], 1, 0))
        uvbuf[slot, SUBLANES + tb:, :] = jnp.where(
            row_u == SUBLANES - 1, jnp.broadcast_to(res[tb + 1:tb + 2], row_u.shape),
            pltpu.roll(res[0:SUBLANES], SUBLANES - 1, 0))

    def gate(sc):
        slot = sc % nring
        cw = pair(cw_ref, sc)
        c = pair(cb_ref, sc) + uvbuf[slot, 0:tb, :] * cw[0:1]
        c = c + uvbuf[slot, SUBLANES:SUBLANES + tb, :] * cw[1:2]
        c = c + uvbuf[slot, 2 * SUBLANES:2 * SUBLANES + tb, :] * cw[2:3]
        return _gelu_gate(c[:, :sub], c[:, sub:]).astype(BF16)

    up(0)
    up(1)
    for sc in range(nsub):
        if sc + 2 < nsub:
            up(sc + 2)
        acts[:, sub * sc:sub * (sc + 1)] = gate(sc)

    acc[...] = _dot(acts[...], wd_ref[...])

    for j in range(seg):
        for k in range(nslab):
            slabs[k, pl.ds(j, SUBLANES, stride=pitch), :] = acc[SUBLANES * j:SUBLANES * (j + 1),
                                                                LANES * k:LANES * (k + 1)]
    ffn = jnp.concatenate(
        [jnp.concatenate([slabs[k, pitch * s:pitch * s + seg, :] for s in range(SUBLANES)], axis=0)
         for k in range(nslab)], axis=1)
    y = (x_ref[...] + d_ref[...]) + ffn
    if final:
        y = _rmsnorm(y, fg_ref[...])
    y_ref[...] = y


def _ffn(x2, d2, g, w_up, cw, cb, w_down, fg, *, S, final):
    T = x2.shape[0]
    tb = TB_FFN
    nblk = S // tb
    hpb = tb // SUBLANES
    n_halo = T // SUBLANES
    full = lambda shape: pl.BlockSpec(shape, lambda i: (0,) * len(shape))
    tile = pl.BlockSpec((tb, D_MODEL), lambda i: (i, 0))
    halo_prev = pl.BlockSpec((SUBLANES, D_MODEL), lambda i: (jnp.maximum(i * hpb - 1, 0), 0))
    halo_next = pl.BlockSpec((SUBLANES, D_MODEL), lambda i: (jnp.minimum((i + 1) * hpb, n_halo - 1), 0))
    kern = functools.partial(_ffn_kernel, nblk=nblk, tb=tb, final=final)
    return pl.pallas_call(
        kern,
        grid=(T // tb,),
        in_specs=[
            tile, halo_prev, halo_next, tile, halo_prev, halo_next,
            full((1, D_MODEL)), full((D_MODEL, 2 * D_FF)), full((3, 2 * D_FF)), full((1, 2 * D_FF)),
            full((D_FF, D_MODEL)), full((1, D_MODEL)),
        ],
        out_specs=pl.BlockSpec((tb, D_MODEL), lambda i: (i, 0)),
        out_shape=jax.ShapeDtypeStruct((T, D_MODEL), F32),
        scratch_shapes=[
            pltpu.VMEM((D_MODEL // LANES, tb + SUBLANES * SUBLANES, LANES), F32),
            pltpu.VMEM((tb + BF16_ROWS, D_MODEL), BF16),
            pltpu.VMEM((FF_RING, tb + 2 * SUBLANES, 2 * FF_SUB), F32),
            pltpu.VMEM((tb, D_FF), BF16),
            pltpu.VMEM((tb, D_MODEL), F32),
        ],
        compiler_params=pltpu.CompilerParams(dimension_semantics=("parallel",),
                                             vmem_limit_bytes=VMEM_LIMIT_FFN),
        name="convffn",
    )(x2, x2, x2, d2, d2, d2, g, w_up, cw, cb, w_down, fg)


def _block_diag(w):
    eye = jnp.eye(RG_BLOCKS, dtype=w.dtype)
    return jnp.einsum('ncd,nm->ncmd', w, eye).reshape(D_RG, D_RG)


def _encoder(x, norm1_g, w_in, b_gates, rg_conv_w, rg_conv_b, rg_wa, rg_ba, rg_wx, rg_bx, rg_lambda,
             ml_norm_g, w_out, norm2_g, w_up, ffn_conv_w, ffn_conv_b, w_down, final_g):
    B, S, _ = x.shape
    T = B * S
    depth = w_in.shape[0]
    x2 = x.reshape(T, D_MODEL)
    row = lambda v: v.reshape(1, -1).astype(F32)
    n_nat = 2 * D_RG + 2 * D_ML
    for l in range(depth):
        w_nat = w_in[l, :, :n_nat].astype(BF16)
        w_tr = w_in[l, :, n_nat:n_nat + 2 * D_ML].T.astype(BF16)
        w_gate = w_in[l, :, n_nat + 2 * D_ML:].T.astype(BF16)
        bias = jnp.broadcast_to(b_gates[l].astype(F32).reshape(N_GATE, 1), (N_GATE, ML_CHUNK))
        rx, rgate, qk, v_t, o_t, gr, uc = _inproj(x2, row(norm1_g[l]), w_nat, w_tr, w_gate, bias,
                                                  B=B, S=S)
        qk3 = qk.reshape(B, S, 2 * D_ML)
        gr4 = gr.reshape(B, S // ML_CHUNK, 2 * G_ROWS, ML_CHUNK)
        uc3 = uc.reshape(B, S, LANES)
        rg_gates = [(_block_diag(rg_wa[l, d]).astype(BF16), _block_diag(rg_wx[l, d]).astype(BF16),
                     row(rg_ba[l, d]), row(rg_bx[l, d]), row(rg_lambda[l, d])) for d in range(2)]
        r_dir = _rglru(rx, rg_conv_w[l].astype(F32), row(rg_conv_b[l]), *rg_gates, B=B, S=S)
        m_dir = [_mlstm(qk3, v_t, gr4, uc3, reverse=reverse) for reverse in (False, True)]
        wo = w_out[l].astype(BF16)
        mg_tile = jnp.broadcast_to(ml_norm_g[l].astype(F32).reshape(D_ML, 1), (D_ML, LANES))
        mixed = _outproj(r_dir[0], r_dir[1], rgate, m_dir[0], m_dir[1], o_t, mg_tile,
                         wo[:D_RG], wo[D_RG:], S=S)
        x2 = _ffn(x2, mixed, row(norm2_g[l]), w_up[l].astype(BF16),
                  ffn_conv_w[l].astype(F32), row(ffn_conv_b[l]),
                  w_down[l].astype(BF16), row(final_g), S=S,
                  final=(l == depth - 1))
    return x2.reshape(B, S, D_MODEL)


def kernel(x_prompt, x_sample, norm1_g, w_in, b_gates, rg_conv_w, rg_conv_b, rg_wa, rg_ba, rg_wx, rg_bx,
           rg_lambda, ml_norm_g, w_out, norm2_g, w_up, ffn_conv_w, ffn_conv_b, w_down, final_g):
    weights = (norm1_g, w_in, b_gates, rg_conv_w, rg_conv_b, rg_wa, rg_ba, rg_wx, rg_bx, rg_lambda,
               ml_norm_g, w_out, norm2_g, w_up, ffn_conv_w, ffn_conv_b, w_down, final_g)
    return (_encoder(x_prompt, *weights), _encoder(x_sample, *weights))
```

```python
import functools

import jax
import jax.numpy as jnp
from jax import lax
from jax.experimental import pallas as pl
from jax.experimental.pallas import tpu as pltpu

F32 = jnp.float32
BF16 = jnp.bfloat16

D_MODEL = 1024
D_RG = 512
D_ML = 512
RG_BLOCKS = 8
RG_C = 8.0
ML_HEADS = 4
ML_HD = 128
ML_CHUNK = 128
D_FF = 3072
EPS = 1e-6
N_GATE = 4 * ML_HEADS

SUBLANES = 8
LANES = 128
BF16_ROWS = 16
VMEM_LIMIT = 48 * 1024 * 1024

TM_PROJ = 1024
TM_SUB = 512
TB_RG = 1024
ML_SEQS = 16
TB_FFN = 512
FF_SUB = 256
FF_RING = 4
VMEM_LIMIT_FFN = 56 * 1024 * 1024

G_B, G_U, G_A, G_BL, G_GM = 0, 8, 16, 24, 32
G_ROWS = 40
UC_DIR = 24
UC_ONES = 2 * UC_DIR


def _params(*sem):
    return pltpu.CompilerParams(dimension_semantics=sem, vmem_limit_bytes=VMEM_LIMIT)


def _softplus(z):
    return jnp.maximum(z, 0.0) + jnp.log1p(jnp.exp(-jnp.abs(z)))


def _gelu_gate(gate, val):
    k0 = -2.0 * 0.7978845608028654 * 1.4426950408889634
    z = gate * (k0 + (k0 * 0.044715) * (gate * gate))
    return (gate * val) / (1.0 + jnp.exp2(z))


def _rmsnorm(x, g):
    return x * lax.rsqrt(jnp.mean(x * x, axis=-1, keepdims=True) + EPS) * g


def _dot(a, b):
    return jnp.dot(a, b, preferred_element_type=F32)


def _dot_nt(a, b):
    return lax.dot_general(a, b, (((1,), (1,)), ((), ())), preferred_element_type=F32)


def _split3(x):
    hi = x.astype(BF16).astype(F32)
    r1 = x - hi
    mid = r1.astype(BF16).astype(F32)
    return hi, mid, (r1 - mid).astype(BF16).astype(F32)


def _lane_scan(x, op, fill, reverse):
    n = x.shape[-1]
    lane = lax.broadcasted_iota(jnp.int32, x.shape, 1)
    s = 1
    while s < n:
        if reverse:
            x = op(x, jnp.where(lane < n - s, pltpu.roll(x, n - s, 1), fill))
        else:
            x = op(x, jnp.where(lane >= s, pltpu.roll(x, s, 1), fill))
        s *= 2
    return x


def _inproj_kernel(x_ref, g_ref, wn_ref, wt_ref, wg_ref, bias_ref, rx_ref, rg_ref, qk_ref, vt_ref,
                   ot_ref, gr_ref, uc_ref):
    for r0 in range(0, x_ref.shape[0], TM_SUB):
        _inproj_piece(x_ref, g_ref, wn_ref, wt_ref, wg_ref, bias_ref, rx_ref, rg_ref, qk_ref, vt_ref,
                      ot_ref, gr_ref, uc_ref, r0)


def _inproj_piece(x_ref, g_ref, wn_ref, wt_ref, wg_ref, bias_ref, rx_ref, rg_ref, qk_ref, vt_ref,
                  ot_ref, gr_ref, uc_ref, r0):
    L = ML_CHUNK
    NH = ML_HEADS
    rows = slice(r0, r0 + TM_SUB)
    h = _rmsnorm(x_ref[rows, :], g_ref[...]).astype(BF16)

    gt = _dot_nt(wg_ref[...], h)
    rowid = lax.broadcasted_iota(jnp.int32, (2 * NH, L), 0)
    head_row = rowid < NH
    rep = lambda col: jnp.broadcast_to(col, (2 * NH, L))
    zeros8 = jnp.zeros((2 * NH, L), F32)
    for cl in range(TM_SUB // L):
        c = r0 // L + cl
        g16 = gt[:, cl * L:(cl + 1) * L] + bias_ref[...]
        tiles = []
        for d, reverse in enumerate((False, True)):
            gates = g16[2 * NH * d:2 * NH * (d + 1)]
            lf = jnp.where(head_row, 0.0, -_softplus(-gates))
            bcum = pltpu.roll(_lane_scan(lf, jnp.add, 0.0, reverse), NH, 0)
            u = jnp.where(head_row, gates - bcum, 0.0)
            last = 0 if reverse else L - 1
            bl = rep(bcum[:, last:last + 1])
            base = G_ROWS * d
            gr_ref[c, base + G_B:base + G_B + 8, :] = bcum
            gr_ref[c, base + G_U:base + G_U + 8, :] = u
            gr_ref[c, base + G_A:base + G_A + 8, :] = bcum + _lane_scan(u, jnp.maximum, -jnp.inf, reverse)
            gr_ref[c, base + G_BL:base + G_BL + 8, :] = bl
            gr_ref[c, base + G_GM:base + G_GM + 8, :] = rep(jnp.max(bl + u, axis=-1, keepdims=True))
            tiles.extend(_split3(u))
        tiles.append(jnp.ones((2 * NH, L), F32))
        tiles.extend([zeros8] * (L // 8 - len(tiles)))
        uc_ref[c * L:(c + 1) * L, :] = jnp.concatenate(tiles, axis=0).T.astype(BF16)

    nat = _dot(h, wn_ref[...])
    rx_ref[rows, :] = nat[:, :D_RG]
    rg_ref[rows, :] = nat[:, D_RG:2 * D_RG].astype(BF16)
    qk_ref[rows, :] = jnp.concatenate(
        [nat[:, 2 * D_RG:2 * D_RG + D_ML] * (ML_HD ** -0.5), nat[:, 2 * D_RG + D_ML:]], axis=-1).astype(BF16)
    tr = _dot_nt(wt_ref[...], h)
    vt_ref[:, rows] = tr[:D_ML].astype(BF16)
    ot_ref[:, rows] = tr[D_ML:].astype(BF16)


def _inproj(x2, g, w_nat, w_tr, w_gate, bias, *, B, S):
    T = B * S
    tm = TM_PROJ
    nb = S // tm
    full = lambda shape: pl.BlockSpec(shape, lambda i: (0,) * len(shape))
    tok = lambda width: pl.BlockSpec((tm, width), lambda i: (i, 0))
    seq_t = pl.BlockSpec((None, D_ML, tm), lambda i: (i // nb, 0, i % nb))
    return pl.pallas_call(
        _inproj_kernel,
        grid=(T // tm,),
        in_specs=[
            tok(D_MODEL), full((1, D_MODEL)), full(w_nat.shape), full(w_tr.shape), full(w_gate.shape),
            full((N_GATE, ML_CHUNK)),
        ],
        out_specs=[
            tok(D_RG), tok(D_RG), tok(2 * D_ML), seq_t, seq_t,
            pl.BlockSpec((tm // ML_CHUNK, 2 * G_ROWS, ML_CHUNK), lambda i: (i, 0, 0)),
            tok(LANES),
        ],
        out_shape=[
            jax.ShapeDtypeStruct((T, D_RG), F32),
            jax.ShapeDtypeStruct((T, D_RG), BF16),
            jax.ShapeDtypeStruct((T, 2 * D_ML), BF16),
            jax.ShapeDtypeStruct((B, D_ML, S), BF16),
            jax.ShapeDtypeStruct((B, D_ML, S), BF16),
            jax.ShapeDtypeStruct((T // ML_CHUNK, 2 * G_ROWS, ML_CHUNK), F32),
            jax.ShapeDtypeStruct((T, LANES), BF16),
        ],
        compiler_params=_params("parallel"),
        name="inproj",
    )(x2, g, w_nat, w_tr, w_gate, bias)


def _rglru_fwd_kernel(x_ref, prev_ref, next_ref, cw_ref, cb_ref, wa_ref, wx_ref, ba_ref, bx_ref,
                      lam_ref, h_ref, xc_ref, slabs, xbuf, pbuf, lbuf, carry, *, nblk, tb):
    blk = pl.program_id(1)

    @pl.when(blk == 0)
    def _():
        carry[...] = jnp.zeros_like(carry)

    npiece = x_ref.shape[0] // tb
    for p in range(npiece):
        rows = pl.ds(tb * p, tb)
        prev = (jnp.where(blk == 0, 0.0, prev_ref[...]) if p == 0
                else x_ref[tb * p - SUBLANES:tb * p, :])
        nxt = (jnp.where(blk == nblk - 1, 0.0, next_ref[...]) if p == npiece - 1
               else x_ref[tb * (p + 1):tb * (p + 1) + SUBLANES, :])
        xc = _rglru_conv(x_ref[rows, :], prev, nxt, cw_ref, cb_ref, slabs.at[p], xbuf.at[p], tb=tb)
        xc_ref[rows, :] = xc
        _rglru_scan(xc, wa_ref, wx_ref, ba_ref, bx_ref, lam_ref, h_ref.at[rows], slabs.at[p],
                    pbuf.at[p], lbuf.at[p], carry, reverse=False, tb=tb)


def _rglru_bwd_kernel(xc_ref, wa_ref, wx_ref, ba_ref, bx_ref, lam_ref, h_ref, slabs, pbuf, lbuf,
                      carry, *, tb):
    @pl.when(pl.program_id(1) == 0)
    def _():
        carry[...] = jnp.zeros_like(carry)

    for p in reversed(range(xc_ref.shape[0] // tb)):
        rows = pl.ds(tb * p, tb)
        _rglru_scan(xc_ref[rows, :], wa_ref, wx_ref, ba_ref, bx_ref, lam_ref, h_ref.at[rows],
                    slabs.at[p], pbuf.at[p], lbuf.at[p], carry, reverse=True, tb=tb)


def _rglru_conv(x, prev, nxt, cw_ref, cb_ref, slabs, xbuf, *, tb):
    nslab = D_RG // LANES
    seg = tb // SUBLANES
    pitch = seg + SUBLANES
    X0 = 2 * SUBLANES

    for k in range(nslab):
        for s in range(SUBLANES):
            slabs[k, pitch * s:pitch * s + seg, :] = x[seg * s:seg * (s + 1), LANES * k:LANES * (k + 1)]
    seam = {}
    for j in range(seg):
        rows = jnp.concatenate(
            [slabs[k, pl.ds(j, SUBLANES, stride=pitch), :] for k in range(nslab)], axis=1)
        xbuf[X0 + SUBLANES * j:X0 + SUBLANES * (j + 1), :] = rows
        if j in (0, seg - 2, seg - 1):
            seam[j] = rows
    row = lax.broadcasted_iota(jnp.int32, (SUBLANES, D_RG), 0)
    tile_row = lambda v, i: jnp.broadcast_to(v[i:i + 1, :], (SUBLANES, D_RG))
    xbuf[0:SUBLANES, :] = jnp.where(row == 0, tile_row(prev, SUBLANES - 2), pltpu.roll(seam[seg - 2], 1, 0))
    xbuf[SUBLANES:X0, :] = jnp.where(row == 0, tile_row(prev, SUBLANES - 1), pltpu.roll(seam[seg - 1], 1, 0))
    xbuf[X0 + tb:, :] = jnp.where(row == SUBLANES - 1, tile_row(nxt, 0),
                                  pltpu.roll(seam[0], SUBLANES - 1, 0))
    xc = cb_ref[...] + xbuf[0:tb, :] * cw_ref[0:1, :]
    xc = xc + xbuf[SUBLANES:SUBLANES + tb, :] * cw_ref[1:2, :]
    xc = xc + xbuf[X0:X0 + tb, :] * cw_ref[2:3, :]
    return xc + xbuf[X0 + SUBLANES:X0 + SUBLANES + tb, :] * cw_ref[3:4, :]


def _rglru_scan(xc, wa_ref, wx_ref, ba_ref, bx_ref, lam_ref, h_ref, slabs, pbuf, lbuf, carry, *,
                reverse, tb):
    nslab = D_RG // LANES
    seg = tb // SUBLANES
    pitch = seg + SUBLANES
    row = lax.broadcasted_iota(jnp.int32, (SUBLANES, D_RG), 0)
    tile_row = lambda v, i: jnp.broadcast_to(v[i:i + 1, :], (SUBLANES, D_RG))

    xcb = xc.astype(BF16)
    r = jax.nn.sigmoid(_dot(xcb, wa_ref[...]) + ba_ref[...])
    i = jax.nn.sigmoid(_dot(xcb, wx_ref[...]) + bx_ref[...])
    decay_rate = RG_C * _softplus(-lam_ref[...])
    a = jnp.exp2((decay_rate * -1.4426950408889634) * r)
    y = jnp.tanh(decay_rate * r) * (a * a + 1.0)
    u = jnp.where(y > 0.0, y * lax.rsqrt(y), 0.0) * (i * xc)

    order = range(seg - 1, -1, -1) if reverse else range(seg)
    P = L = None
    for j in order:
        rs = slice(SUBLANES * j, SUBLANES * (j + 1))
        if P is None:
            P, L = a[rs], u[rs]
        else:
            P, L = a[rs] * P, a[rs] * L + u[rs]
        pbuf[rs, :] = P
        lbuf[rs, :] = L

    A, U = P, L
    for s in (1, 2, 4):
        if reverse:
            keep = row < SUBLANES - s
            shift = SUBLANES - s
        else:
            keep = row >= s
            shift = s
        a_sh = jnp.where(keep, pltpu.roll(A, shift, 0), 1.0)
        u_sh = jnp.where(keep, pltpu.roll(U, shift, 0), 0.0)
        U = A * u_sh + U
        A = A * a_sh
    c_in = carry[...]
    e = U + A * c_in
    if reverse:
        c_seg = jnp.where(row == SUBLANES - 1, c_in, pltpu.roll(e, SUBLANES - 1, 0))
        carry[...] = tile_row(e, 0)
    else:
        c_seg = jnp.where(row == 0, c_in, pltpu.roll(e, 1, 0))
        carry[...] = tile_row(e, SUBLANES - 1)

    for j in range(seg):
        rs = slice(SUBLANES * j, SUBLANES * (j + 1))
        hj = lbuf[rs, :] + pbuf[rs, :] * c_seg
        for k in range(nslab):
            slabs[k, pl.ds(j, SUBLANES, stride=pitch), :] = hj[:, LANES * k:LANES * (k + 1)]
    for k in range(nslab):
        for s in range(SUBLANES):
            h_ref[seg * s:seg * (s + 1), LANES * k:LANES * (k + 1)] = (
                slabs[k, pitch * s:pitch * s + seg, :].astype(h_ref.dtype))


def _rglru(x, cw, cb, gates_fwd, gates_bwd, *, B, S):
    T = B * S
    tile = TB_RG
    tb = TM_SUB
    npiece = tile // tb
    nblk = S // tile
    hb = tile // SUBLANES
    n_halo = T // SUBLANES
    full = lambda shape: pl.BlockSpec(shape, lambda b, j: (0,) * len(shape))
    gate_specs = [full((D_RG, D_RG)), full((D_RG, D_RG)), full((1, D_RG)), full((1, D_RG)), full((1, D_RG))]
    slabs = pltpu.VMEM((npiece, D_RG // LANES, tb + SUBLANES * SUBLANES, LANES), F32)
    scan_scratch = [
        pltpu.VMEM((npiece, tb, D_RG), F32),
        pltpu.VMEM((npiece, tb, D_RG), F32),
        pltpu.VMEM((SUBLANES, D_RG), F32),
    ]
    tile_fwd = pl.BlockSpec((tile, D_RG), lambda b, j: (b * nblk + j, 0))
    tile_bwd = pl.BlockSpec((tile, D_RG), lambda b, j: (b * nblk + nblk - 1 - j, 0))

    h_fwd, xc = pl.pallas_call(
        functools.partial(_rglru_fwd_kernel, nblk=nblk, tb=tb),
        grid=(B, nblk),
        in_specs=[
            tile_fwd,
            pl.BlockSpec((SUBLANES, D_RG), lambda b, j: (jnp.maximum((b * nblk + j) * hb - 1, 0), 0)),
            pl.BlockSpec((SUBLANES, D_RG), lambda b, j: (jnp.minimum((b * nblk + j + 1) * hb, n_halo - 1), 0)),
            full((4, D_RG)), full((1, D_RG)), *gate_specs,
        ],
        out_specs=[tile_fwd, tile_fwd],
        out_shape=[jax.ShapeDtypeStruct((T, D_RG), BF16),
                   jax.ShapeDtypeStruct((T, D_RG), F32)],
        scratch_shapes=[slabs, pltpu.VMEM((npiece, tb + 3 * SUBLANES, D_RG), F32), *scan_scratch],
        compiler_params=_params("parallel", "arbitrary"),
        name="rglru_fwd",
    )(x, x, x, cw, cb, *gates_fwd)
    h_bwd = pl.pallas_call(
        functools.partial(_rglru_bwd_kernel, tb=tb),
        grid=(B, nblk),
        in_specs=[tile_bwd, *gate_specs],
        out_specs=tile_bwd,
        out_shape=jax.ShapeDtypeStruct((T, D_RG), BF16),
        scratch_shapes=[slabs, *scan_scratch],
        compiler_params=_params("parallel", "arbitrary"),
        name="rglru_bwd",
    )(xc, *gates_bwd)
    return h_fwd, h_bwd


def _mlstm_kernel(q_ref, k_ref, vt_ref, gr_ref, uc_ref, h_ref, ct_st, n_st, m_st, *, reverse, bb):
    L = ML_CHUNK
    NH = ML_HEADS
    d = 1 if reverse else 0

    @pl.when(pl.program_id(1) == 0)
    def _():
        ct_st[...] = jnp.zeros_like(ct_st)
        n_st[...] = jnp.zeros_like(n_st)
        m_st[...] = jnp.zeros_like(m_st)

    row8 = lax.broadcasted_iota(jnp.int32, (SUBLANES, L), 0)
    s_id = lax.broadcasted_iota(jnp.int32, (L, L), 0)
    t_id = lax.broadcasted_iota(jnp.int32, (L, L), 1)
    valid = (s_id >= t_id) if reverse else (s_id <= t_id)
    zeros8 = jnp.zeros((SUBLANES, L), F32)
    tile = lambda rows, hd: jnp.broadcast_to(rows[hd:hd + 1, :], (L, L))
    row_of = lambda rows, hd: jnp.broadcast_to(rows[hd:hd + 1, :], (SUBLANES, L))
    pairs = [(b, hp) for b in range(bb) for hp in range(NH // 2)]

    seq = []
    for b in range(bb):
        base = G_ROWS * d
        bcum = gr_ref[b, base + G_B:base + G_B + 8, :]
        u = gr_ref[b, base + G_U:base + G_U + 8, :]
        b_last = gr_ref[b, base + G_BL:base + G_BL + 8, :]
        m_prev = m_st[b]
        inter = bcum + m_prev
        m_t = jnp.maximum(inter, gr_ref[b, base + G_A:base + G_A + 8, :])
        m_new = jnp.maximum(b_last + m_prev, gr_ref[b, base + G_GM:base + G_GM + 8, :])
        n_prev = n_st[b]
        wg = jnp.exp(u + (b_last - m_new))
        seq.append(dict(
            w_int=jnp.exp(inter - m_t), e_neg=jnp.exp(-m_t), v3=_split3(bcum - m_t),
            decay=jnp.exp(b_last + m_prev - m_new), m_new=m_new, n_prev=n_prev, wg=wg,
            n_lhs=jnp.concatenate([n_prev, zeros8], axis=0).astype(BF16),
            wg_lhs=jnp.concatenate([wg, zeros8], axis=0).astype(BF16)))

    cs = lambda hd: slice(hd * ML_HD, (hd + 1) * ML_HD)
    side = lambda i: slice(i * L, (i + 1) * L)
    twice = lambda x: jnp.concatenate([x, x], axis=1)

    def blockdiag(a0, a1):
        z = jnp.zeros_like(a0)
        return jnp.concatenate([jnp.concatenate([a0, z], axis=1), jnp.concatenate([z, a1], axis=1)], axis=0)

    def expo_rhs(b, hd):
        onehot = jnp.where(row8 == hd, 1.0, 0.0)
        v_hi, v_mid, v_lo = (row_of(x, hd) for x in seq[b]["v3"])
        v_rows = jnp.where(row8 == 0, v_hi, jnp.where(row8 == 1, v_mid, jnp.where(row8 == 2, v_lo, 0.0)))
        slabs = [zeros8] * (L // SUBLANES)
        for j in range(3):
            slabs[(UC_DIR * d) // SUBLANES + j] = onehot
        slabs[UC_ONES // SUBLANES] = v_rows
        return jnp.concatenate(slabs, axis=0).astype(BF16)

    st, expo, inter, upd = {}, {}, {}, {}
    for b, hp in pairs:
        sq = seq[b]
        heads = (2 * hp, 2 * hp + 1)
        q_diag = blockdiag(q_ref[b, :, cs(heads[0])], q_ref[b, :, cs(heads[1])])
        sq_all = _dot_nt(
            jnp.concatenate([k_ref[b, :, heads[0] * ML_HD:(heads[1] + 1) * ML_HD], twice(sq["n_lhs"]),
                             jnp.concatenate([ct_st[b, hd].astype(BF16) for hd in heads], axis=1)],
                            axis=0), q_diag)
        st[b, hp] = sq_all[0:L + BF16_ROWS]
        inter[b, hp] = sq_all[L + BF16_ROWS:]
        expo[b, hp] = _dot(uc_ref[b], jnp.concatenate([expo_rhs(b, hd) for hd in heads], axis=1))
        vw = jnp.concatenate([(vt_ref[b, cs(hd), :].astype(F32) * sq["wg"][hd:hd + 1, :]).astype(BF16)
                              for hd in heads], axis=1)
        upd[b, hp] = _dot(jnp.concatenate([vw, twice(sq["wg_lhs"])], axis=0),
                          blockdiag(k_ref[b, :, cs(heads[0])], k_ref[b, :, cs(heads[1])]))

    for b, hp in pairs:
        sq = seq[b]
        for i, hd in enumerate((2 * hp, 2 * hp + 1)):
            ct_st[b, hd] = tile(sq["decay"], hd) * ct_st[b, hd] + upd[b, hp][0:L, side(i)]
            n_st[b, hd:hd + 1, :] = (sq["decay"][hd:hd + 1, :] * sq["n_prev"][hd:hd + 1, :]
                                     + upd[b, hp][L + hd:L + hd + 1, side(i)])
    for b in range(bb):
        m_st[b] = seq[b]["m_new"]

    for b, hp in pairs:
        sq = seq[b]
        heads = (2 * hp, 2 * hp + 1)
        p_t, rw = [], []
        for i, hd in enumerate(heads):
            s_t = st[b, hp][0:L, side(i)] * jnp.exp(jnp.where(valid, expo[b, hp][:, side(i)], -jnp.inf))
            w_h = sq["w_int"][hd:hd + 1, :]
            den = jnp.sum(s_t, axis=0, keepdims=True) + w_h * st[b, hp][L + hd:L + hd + 1, side(i)]
            r = 1.0 / jnp.maximum(jnp.abs(den), sq["e_neg"][hd:hd + 1, :])
            p_t.append((s_t * r).astype(BF16))
            rw.append(r * w_h)
        intra = _dot(jnp.concatenate([vt_ref[b, cs(hd), :] for hd in heads], axis=1), blockdiag(*p_t))
        out = (intra + inter[b, hp] * jnp.concatenate(rw, axis=1)).astype(h_ref.dtype)
        for i, hd in enumerate(heads):
            h_ref[b, cs(hd), :] = out[:, side(i)]


def _mlstm(qk3, vt, gr4, uc3, *, reverse):
    B, S, _ = qk3.shape
    L = ML_CHUNK
    nc = S // L
    bb = ML_SEQS

    def chunk(c):
        return (nc - 1 - c) if reverse else c

    kern = functools.partial(_mlstm_kernel, reverse=reverse, bb=bb)
    return pl.pallas_call(
        kern,
        grid=(B // bb, nc),
        in_specs=[
            pl.BlockSpec((bb, L, D_ML), lambda b, c: (b, chunk(c), 0)),
            pl.BlockSpec((bb, L, D_ML), lambda b, c: (b, chunk(c), 1)),
            pl.BlockSpec((bb, D_ML, L), lambda b, c: (b, 0, chunk(c))),
            pl.BlockSpec((bb, None, 2 * G_ROWS, L), lambda b, c: (b, chunk(c), 0, 0)),
            pl.BlockSpec((bb, L, LANES), lambda b, c: (b, chunk(c), 0)),
        ],
        out_specs=pl.BlockSpec((bb, D_ML, L), lambda b, c: (b, 0, chunk(c))),
        out_shape=jax.ShapeDtypeStruct((B, D_ML, S), BF16),
        scratch_shapes=[
            pltpu.VMEM((bb, ML_HEADS, ML_HD, ML_HD), F32),
            pltpu.VMEM((bb, 2 * ML_HEADS, ML_HD), F32),
            pltpu.VMEM((bb, 2 * ML_HEADS, LANES), F32),
        ],
        compiler_params=_params("parallel", "arbitrary"),
        name="mlstm_bwd" if reverse else "mlstm_fwd",
    )(qk3, qk3, vt, gr4, uc3)


def _outproj_kernel(rf_ref, rb_ref, gate_ref, mf_ref, mb_ref, ot_ref, mg_ref, wr_ref, wm_ref, y_ref):
    for r0 in range(0, y_ref.shape[0], TM_SUB):
        rows = slice(r0, r0 + TM_SUB)
        y_rg = _gelu_gate(gate_ref[rows, :].astype(F32),
                          rf_ref[rows, :].astype(F32) + rb_ref[rows, :].astype(F32))
        acc = _dot(y_rg.astype(BF16), wr_ref[...])
        h_t = mf_ref[:, rows].astype(F32) + mb_ref[:, rows].astype(F32)
        parts = []
        for hd in range(ML_HEADS):
            hh = h_t[hd * ML_HD:(hd + 1) * ML_HD]
            parts.append(hh * lax.rsqrt(jnp.mean(hh * hh, axis=0, keepdims=True) + EPS))
        mg = jnp.tile(mg_ref[...], (1, TM_SUB // LANES))
        y_t = jax.nn.sigmoid(ot_ref[:, rows].astype(F32)) * (jnp.concatenate(parts, axis=0) * mg)
        y_ref[rows, :] = acc + _dot(y_t.T.astype(BF16), wm_ref[...])


def _outproj(rf, rb, pa, mf_t, mb_t, o_t, mg_tile, w_rg, w_ml, *, S):
    T = rf.shape[0]
    tm = TM_PROJ
    nb = S // tm
    tok = lambda width, col: pl.BlockSpec((tm, width), lambda i: (i, col))
    seq_t = pl.BlockSpec((None, D_ML, tm), lambda i: (i // nb, 0, i % nb))
    full = lambda shape: pl.BlockSpec(shape, lambda i: (0,) * len(shape))
    return pl.pallas_call(
        _outproj_kernel,
        grid=(T // tm,),
        in_specs=[
            tok(D_RG, 0), tok(D_RG, 0), tok(D_RG, 0),
            seq_t, seq_t, seq_t,
            full((D_ML, LANES)), full((D_RG, D_MODEL)), full((D_ML, D_MODEL)),
        ],
        out_specs=tok(D_MODEL, 0),
        out_shape=jax.ShapeDtypeStruct((T, D_MODEL), F32),
        compiler_params=_params("parallel"),
        name="outproj",
    )(rf, rb, pa, mf_t, mb_t, o_t, mg_tile, w_rg, w_ml)


def _ffn_kernel(x_ref, prev_ref, next_ref, d_ref, dprev_ref, dnext_ref, g_ref, wu_ref, cw_ref, cb_ref,
                wd_ref, fg_ref, y_ref, slabs, hbuf, uvbuf, acts, acc, *, nblk, tb, final):
    blk = pl.program_id(0) % nblk
    sub = FF_SUB
    nsub = D_FF // sub
    nring = uvbuf.shape[0]
    nslab = D_MODEL // LANES
    seg = tb // SUBLANES
    pitch = seg + SUBLANES

    g = g_ref[...]
    hn = _rmsnorm(x_ref[...] + d_ref[...], g)
    for k in range(nslab):
        for s in range(SUBLANES):
            slabs[k, pitch * s:pitch * s + seg, :] = hn[seg * s:seg * (s + 1), LANES * k:LANES * (k + 1)]

    def perm_rows(j):
        return jnp.concatenate(
            [slabs[k, pl.ds(j, SUBLANES, stride=pitch), :] for k in range(nslab)], axis=1)

    for jj in range(seg // 2):
        hbuf[BF16_ROWS * jj:BF16_ROWS * (jj + 1), :] = jnp.concatenate(
            [perm_rows(2 * jj), perm_rows(2 * jj + 1)], axis=0).astype(BF16)
    row_x = lax.broadcasted_iota(jnp.int32, (SUBLANES, D_MODEL), 0)
    h_prev = jnp.where(blk == 0, 0.0, pltpu.roll(_rmsnorm(prev_ref[...] + dprev_ref[...], g), 1, 0))
    h_next = jnp.where(blk == nblk - 1, 0.0,
                       pltpu.roll(_rmsnorm(next_ref[...] + dnext_ref[...], g), 1, 0))
    halo = jnp.where(row_x == 0, h_prev, jnp.where(row_x == 1, h_next, 0.0))
    hbuf[tb:, :] = jnp.concatenate([halo, jnp.zeros_like(halo)], axis=0).astype(BF16)

    row_u = lax.broadcasted_iota(jnp.int32, (SUBLANES, 2 * sub), 0)

    def pair(ref, sc):
        return jnp.concatenate([ref[:, sub * sc:sub * (sc + 1)],
                                ref[:, D_FF + sub * sc:D_FF + sub * (sc + 1)]], axis=1)

    def up(sc):
        slot = sc % nring
        res = jnp.concatenate([_dot(hbuf[...], wu_ref[:, sub * sc:sub * (sc + 1)]),
                               _dot(hbuf[...], wu_ref[:, D_FF + sub * sc:D_FF + sub * (sc + 1)])], axis=1)
        uvbuf[slot, SUBLANES:SUBLANES + tb, :] = res[0:tb]
        uvbuf[slot, 0:SUBLANES, :] = jnp.where(
            row_u == 0, jnp.broadcast_to(res[tb:tb + 1], row_u.shape),
            pltpu.roll(res[tb - SUBLANES:tb], 1, 0))
        uvbuf[slot, SUBLANES + tb:, :] = jnp.where(
            row_u == SUBLANES - 1, jnp.broadcast_to(res[tb + 1:tb + 2], row_u.shape),
            pltpu.roll(res[0:SUBLANES], SUBLANES - 1, 0))

    def gate(sc):
        slot = sc % nring
        cw = pair(cw_ref, sc)
        c = pair(cb_ref, sc) + uvbuf[slot, 0:tb, :] * cw[0:1]
        c = c + uvbuf[slot, SUBLANES:SUBLANES + tb, :] * cw[1:2]
        c = c + uvbuf[slot, 2 * SUBLANES:2 * SUBLANES + tb, :] * cw[2:3]
        return _gelu_gate(c[:, :sub], c[:, sub:]).astype(BF16)

    up(0)
    up(1)
    for sc in range(nsub):
        if sc + 2 < nsub:
            up(sc + 2)
        acts[:, sub * sc:sub * (sc + 1)] = gate(sc)

    acc[...] = _dot(acts[...], wd_ref[...])

    for j in range(seg):
        for k in range(nslab):
            slabs[k, pl.ds(j, SUBLANES, stride=pitch), :] = acc[SUBLANES * j:SUBLANES * (j + 1),
                                                                LANES * k:LANES * (k + 1)]
    ffn = jnp.concatenate(
        [jnp.concatenate([slabs[k, pitch * s:pitch * s + seg, :] for s in range(SUBLANES)], axis=0)
         for k in range(nslab)], axis=1)
    y = (x_ref[...] + d_ref[...]) + ffn
    if final:
        y = _rmsnorm(y, fg_ref[...])
    y_ref[...] = y


def _ffn(x2, d2, g, w_up, cw, cb, w_down, fg, *, S, final):
    T = x2.shape[0]
    tb = TB_FFN
    nblk = S // tb
    hpb = tb // SUBLANES
    n_halo = T // SUBLANES
    full = lambda shape: pl.BlockSpec(shape, lambda i: (0,) * len(shape))
    tile = pl.BlockSpec((tb, D_MODEL), lambda i: (i, 0))
    halo_prev = pl.BlockSpec((SUBLANES, D_MODEL), lambda i: (jnp.maximum(i * hpb - 1, 0), 0))
    halo_next = pl.BlockSpec((SUBLANES, D_MODEL), lambda i: (jnp.minimum((i + 1) * hpb, n_halo - 1), 0))
    kern = functools.partial(_ffn_kernel, nblk=nblk, tb=tb, final=final)
    return pl.pallas_call(
        kern,
        grid=(T // tb,),
        in_specs=[
            tile, halo_prev, halo_next, tile, halo_prev, halo_next,
            full((1, D_MODEL)), full((D_MODEL, 2 * D_FF)), full((3, 2 * D_FF)), full((1, 2 * D_FF)),
            full((D_FF, D_MODEL)), full((1, D_MODEL)),
        ],
        out_specs=pl.BlockSpec((tb, D_MODEL), lambda i: (i, 0)),
        out_shape=jax.ShapeDtypeStruct((T, D_MODEL), F32),
        scratch_shapes=[
            pltpu.VMEM((D_MODEL // LANES, tb + SUBLANES * SUBLANES, LANES), F32),
            pltpu.VMEM((tb + BF16_ROWS, D_MODEL), BF16),
            pltpu.VMEM((FF_RING, tb + 2 * SUBLANES, 2 * FF_SUB), F32),
            pltpu.VMEM((tb, D_FF), BF16),
            pltpu.VMEM((tb, D_MODEL), F32),
        ],
        compiler_params=pltpu.CompilerParams(dimension_semantics=("parallel",),
                                             vmem_limit_bytes=VMEM_LIMIT_FFN),
        name="convffn",
    )(x2, x2, x2, d2, d2, d2, g, w_up, cw, cb, w_down, fg)


def _block_diag(w):
    eye = jnp.eye(RG_BLOCKS, dtype=w.dtype)
    return jnp.einsum('ncd,nm->ncmd', w, eye).reshape(D_RG, D_RG)


def _encoder(x, norm1_g, w_in, b_gates, rg_conv_w, rg_conv_b, rg_wa, rg_ba, rg_wx, rg_bx, rg_lambda,
             ml_norm_g, w_out, norm2_g, w_up, ffn_conv_w, ffn_conv_b, w_down, final_g):
    B, S, _ = x.shape
    T = B * S
    depth = w_in.shape[0]
    x2 = x.reshape(T, D_MODEL)
    row = lambda v: v.reshape(1, -1).astype(F32)
    n_nat = 2 * D_RG + 2 * D_ML
    for l in range(depth):
        w_nat = w_in[l, :, :n_nat].astype(BF16)
        w_tr = w_in[l, :, n_nat:n_nat + 2 * D_ML].T.astype(BF16)
        w_gate = w_in[l, :, n_nat + 2 * D_ML:].T.astype(BF16)
        bias = jnp.broadcast_to(b_gates[l].astype(F32).reshape(N_GATE, 1), (N_GATE, ML_CHUNK))
        rx, rgate, qk, v_t, o_t, gr, uc = _inproj(x2, row(norm1_g[l]), w_nat, w_tr, w_gate, bias,
                                                  B=B, S=S)
        qk3 = qk.reshape(B, S, 2 * D_ML)
        gr4 = gr.reshape(B, S // ML_CHUNK, 2 * G_ROWS, ML_CHUNK)
        uc3 = uc.reshape(B, S, LANES)
        rg_gates = [(_block_diag(rg_wa[l, d]).astype(BF16), _block_diag(rg_wx[l, d]).astype(BF16),
                     row(rg_ba[l, d]), row(rg_bx[l, d]), row(rg_lambda[l, d])) for d in range(2)]
        r_dir = _rglru(rx, rg_conv_w[l].astype(F32), row(rg_conv_b[l]), *rg_gates, B=B, S=S)
        m_dir = [_mlstm(qk3, v_t, gr4, uc3, reverse=reverse) for reverse in (False, True)]
        wo = w_out[l].astype(BF16)
        mg_tile = jnp.broadcast_to(ml_norm_g[l].astype(F32).reshape(D_ML, 1), (D_ML, LANES))
        mixed = _outproj(r_dir[0], r_dir[1], rgate, m_dir[0], m_dir[1], o_t, mg_tile,
                         wo[:D_RG], wo[D_RG:], S=S)
        x2 = _ffn(x2, mixed, row(norm2_g[l]), w_up[l].astype(BF16),
                  ffn_conv_w[l].astype(F32), row(ffn_conv_b[l]),
                  w_down[l].astype(BF16), row(final_g), S=S,
                  final=(l == depth - 1))
    return x2.reshape(B, S, D_MODEL)


def kernel(x_prompt, x_sample, norm1_g, w_in, b_gates, rg_conv_w, rg_conv_b, rg_wa, rg_ba, rg_wx, rg_bx,
           rg_lambda, ml_norm_g, w_out, norm2_g, w_up, ffn_conv_w, ffn_conv_b, w_down, final_g):
    weights = (norm1_g, w_in, b_gates, rg_conv_w, rg_conv_b, rg_wa, rg_ba, rg_wx, rg_bx, rg_lambda,
               ml_norm_g, w_out, norm2_g, w_up, ffn_conv_w, ffn_conv_b, w_down, final_g)
    return (_encoder(x_prompt, *weights), _encoder(x_sample, *weights))
```

```python
import functools

import jax
import jax.numpy as jnp
from jax import lax
from jax.experimental import pallas as pl
from jax.experimental.pallas import tpu as pltpu

F32 = jnp.float32
BF16 = jnp.bfloat16

D_MODEL = 1024
D_RG = 512
D_ML = 512
RG_BLOCKS = 8
RG_C = 8.0
ML_HEADS = 4
ML_HD = 128
ML_CHUNK = 128
D_FF = 3072
EPS = 1e-6
N_GATE = 4 * ML_HEADS

SUBLANES = 8
LANES = 128
BF16_ROWS = 16
VMEM_LIMIT = 48 * 1024 * 1024

TM_PROJ = 1024
TM_SUB = 512
TB_RG = 1024
ML_SEQS = 16
TB_FFN = 512
FF_SUB = 256
FF_RING = 4
VMEM_LIMIT_FFN = 56 * 1024 * 1024

G_B, G_U, G_A, G_BL, G_GM = 0, 8, 16, 24, 32
G_ROWS = 40
UC_DIR = 24
UC_ONES = 2 * UC_DIR


def _params(*sem):
    return pltpu.CompilerParams(dimension_semantics=sem, vmem_limit_bytes=VMEM_LIMIT)


def _softplus(z):
    return jnp.maximum(z, 0.0) + jnp.log1p(jnp.exp(-jnp.abs(z)))


def _gelu_gate(gate, val):
    k0 = -2.0 * 0.7978845608028654 * 1.4426950408889634
    z = gate * (k0 + (k0 * 0.044715) * (gate * gate))
    return (gate * val) / (1.0 + jnp.exp2(z))


def _rmsnorm(x, g):
    return x * lax.rsqrt(jnp.mean(x * x, axis=-1, keepdims=True) + EPS) * g


def _dot(a, b):
    return jnp.dot(a, b, preferred_element_type=F32)


def _dot_nt(a, b):
    return lax.dot_general(a, b, (((1,), (1,)), ((), ())), preferred_element_type=F32)


def _split3(x):
    hi = x.astype(BF16).astype(F32)
    r1 = x - hi
    mid = r1.astype(BF16).astype(F32)
    return hi, mid, (r1 - mid).astype(BF16).astype(F32)


def _lane_scan(x, op, fill, reverse):
    n = x.shape[-1]
    lane = lax.broadcasted_iota(jnp.int32, x.shape, 1)
    s = 1
    while s < n:
        if reverse:
            x = op(x, jnp.where(lane < n - s, pltpu.roll(x, n - s, 1), fill))
        else:
            x = op(x, jnp.where(lane >= s, pltpu.roll(x, s, 1), fill))
        s *= 2
    return x


def _inproj_kernel(x_ref, g_ref, wn_ref, wt_ref, wg_ref, bias_ref, rx_ref, rg_ref, k_ref, qt_ref, vt_ref,
                   ot_ref, gr_ref, uc_ref):
    for r0 in range(0, x_ref.shape[0], TM_SUB):
        _inproj_piece(x_ref, g_ref, wn_ref, wt_ref, wg_ref, bias_ref, rx_ref, rg_ref, k_ref, qt_ref, vt_ref,
                      ot_ref, gr_ref, uc_ref, r0)


def _inproj_piece(x_ref, g_ref, wn_ref, wt_ref, wg_ref, bias_ref, rx_ref, rg_ref, k_ref, qt_ref, vt_ref,
                  ot_ref, gr_ref, uc_ref, r0):
    L = ML_CHUNK
    NH = ML_HEADS
    rows = slice(r0, r0 + TM_SUB)
    h = _rmsnorm(x_ref[rows, :], g_ref[...]).astype(BF16)

    gt = _dot_nt(wg_ref[...], h)
    rowid = lax.broadcasted_iota(jnp.int32, (2 * NH, L), 0)
    head_row = rowid < NH
    rep = lambda col: jnp.broadcast_to(col, (2 * NH, L))
    zeros8 = jnp.zeros((2 * NH, L), F32)
    for cl in range(TM_SUB // L):
        c = r0 // L + cl
        g16 = gt[:, cl * L:(cl + 1) * L] + bias_ref[...]
        tiles = []
        for d, reverse in enumerate((False, True)):
            gates = g16[2 * NH * d:2 * NH * (d + 1)]
            lf = jnp.where(head_row, 0.0, -_softplus(-gates))
            bcum = pltpu.roll(_lane_scan(lf, jnp.add, 0.0, reverse), NH, 0)
            u = jnp.where(head_row, gates - bcum, 0.0)
            last = 0 if reverse else L - 1
            bl = rep(bcum[:, last:last + 1])
            base = G_ROWS * d
            gr_ref[c, base + G_B:base + G_B + 8, :] = bcum
            gr_ref[c, base + G_U:base + G_U + 8, :] = u
            gr_ref[c, base + G_A:base + G_A + 8, :] = bcum + _lane_scan(u, jnp.maximum, -jnp.inf, reverse)
            gr_ref[c, base + G_BL:base + G_BL + 8, :] = bl
            gr_ref[c, base + G_GM:base + G_GM + 8, :] = rep(jnp.max(bl + u, axis=-1, keepdims=True))
            tiles.extend(_split3(u))
        tiles.append(jnp.ones((2 * NH, L), F32))
        tiles.extend([zeros8] * (L // 8 - len(tiles)))
        uc_ref[c * L:(c + 1) * L, :] = jnp.concatenate(tiles, axis=0).T.astype(BF16)

    nat = _dot(h, wn_ref[...])
    rx_ref[rows, :] = nat[:, :D_RG]
    rg_ref[rows, :] = nat[:, D_RG:2 * D_RG].astype(BF16)
    k_ref[rows, :] = nat[:, 2 * D_RG:].astype(BF16)
    tr = _dot_nt(wt_ref[...], h)
    qt_ref[:, rows] = (tr[:D_ML] * (ML_HD ** -0.5)).astype(BF16)
    vt_ref[:, rows] = tr[D_ML:2 * D_ML].astype(BF16)
    ot_ref[:, rows] = tr[2 * D_ML:].astype(BF16)


def _inproj(x2, g, w_nat, w_tr, w_gate, bias, *, B, S):
    T = B * S
    tm = TM_PROJ
    nb = S // tm
    full = lambda shape: pl.BlockSpec(shape, lambda i: (0,) * len(shape))
    tok = lambda width: pl.BlockSpec((tm, width), lambda i: (i, 0))
    seq_t = pl.BlockSpec((None, D_ML, tm), lambda i: (i // nb, 0, i % nb))
    return pl.pallas_call(
        _inproj_kernel,
        grid=(T // tm,),
        in_specs=[
            tok(D_MODEL), full((1, D_MODEL)), full(w_nat.shape), full(w_tr.shape), full(w_gate.shape),
            full((N_GATE, ML_CHUNK)),
        ],
        out_specs=[
            tok(D_RG), tok(D_RG), tok(D_ML), seq_t, seq_t, seq_t,
            pl.BlockSpec((tm // ML_CHUNK, 2 * G_ROWS, ML_CHUNK), lambda i: (i, 0, 0)),
            tok(LANES),
        ],
        out_shape=[
            jax.ShapeDtypeStruct((T, D_RG), F32),
            jax.ShapeDtypeStruct((T, D_RG), BF16),
            jax.ShapeDtypeStruct((T, D_ML), BF16),
            jax.ShapeDtypeStruct((B, D_ML, S), BF16),
            jax.ShapeDtypeStruct((B, D_ML, S), BF16),
            jax.ShapeDtypeStruct((B, D_ML, S), BF16),
            jax.ShapeDtypeStruct((T // ML_CHUNK, 2 * G_ROWS, ML_CHUNK), F32),
            jax.ShapeDtypeStruct((T, LANES), BF16),
        ],
        compiler_params=_params("parallel"),
        name="inproj",
    )(x2, g, w_nat, w_tr, w_gate, bias)


def _rglru_fwd_kernel(x_ref, prev_ref, next_ref, cw_ref, cb_ref, wa_ref, wx_ref, ba_ref, bx_ref,
                      lam_ref, h_ref, xc_ref, slabs, xbuf, pbuf, lbuf, carry, *, nblk, tb):
    blk = pl.program_id(1)

    @pl.when(blk == 0)
    def _():
        carry[...] = jnp.zeros_like(carry)

    npiece = x_ref.shape[0] // tb
    for p in range(npiece):
        rows = pl.ds(tb * p, tb)
        prev = (jnp.where(blk == 0, 0.0, prev_ref[...]) if p == 0
                else x_ref[tb * p - SUBLANES:tb * p, :])
        nxt = (jnp.where(blk == nblk - 1, 0.0, next_ref[...]) if p == npiece - 1
               else x_ref[tb * (p + 1):tb * (p + 1) + SUBLANES, :])
        xc = _rglru_conv(x_ref[rows, :], prev, nxt, cw_ref, cb_ref, slabs.at[p], xbuf.at[p], tb=tb)
        xc_ref[rows, :] = xc
        _rglru_scan(xc, wa_ref, wx_ref, ba_ref, bx_ref, lam_ref, h_ref.at[rows], slabs.at[p],
                    pbuf.at[p], lbuf.at[p], carry, reverse=False, tb=tb)


def _rglru_bwd_kernel(xc_ref, wa_ref, wx_ref, ba_ref, bx_ref, lam_ref, h_ref, slabs, pbuf, lbuf,
                      carry, *, tb):
    @pl.when(pl.program_id(1) == 0)
    def _():
        carry[...] = jnp.zeros_like(carry)

    for p in reversed(range(xc_ref.shape[0] // tb)):
        rows = pl.ds(tb * p, tb)
        _rglru_scan(xc_ref[rows, :], wa_ref, wx_ref, ba_ref, bx_ref, lam_ref, h_ref.at[rows],
                    slabs.at[p], pbuf.at[p], lbuf.at[p], carry, reverse=True, tb=tb)


def _rglru_conv(x, prev, nxt, cw_ref, cb_ref, slabs, xbuf, *, tb):
    nslab = D_RG // LANES
    seg = tb // SUBLANES
    pitch = seg + SUBLANES
    X0 = 2 * SUBLANES

    for k in range(nslab):
        for s in range(SUBLANES):
            slabs[k, pitch * s:pitch * s + seg, :] = x[seg * s:seg * (s + 1), LANES * k:LANES * (k + 1)]
    seam = {}
    for j in range(seg):
        rows = jnp.concatenate(
            [slabs[k, pl.ds(j, SUBLANES, stride=pitch), :] for k in range(nslab)], axis=1)
        xbuf[X0 + SUBLANES * j:X0 + SUBLANES * (j + 1), :] = rows
        if j in (0, seg - 2, seg - 1):
            seam[j] = rows
    row = lax.broadcasted_iota(jnp.int32, (SUBLANES, D_RG), 0)
    tile_row = lambda v, i: jnp.broadcast_to(v[i:i + 1, :], (SUBLANES, D_RG))
    xbuf[0:SUBLANES, :] = jnp.where(row == 0, tile_row(prev, SUBLANES - 2), pltpu.roll(seam[seg - 2], 1, 0))
    xbuf[SUBLANES:X0, :] = jnp.where(row == 0, tile_row(prev, SUBLANES - 1), pltpu.roll(seam[seg - 1], 1, 0))
    xbuf[X0 + tb:, :] = jnp.where(row == SUBLANES - 1, tile_row(nxt, 0),
                                  pltpu.roll(seam[0], SUBLANES - 1, 0))
    xc = cb_ref[...] + xbuf[0:tb, :] * cw_ref[0:1, :]
    xc = xc + xbuf[SUBLANES:SUBLANES + tb, :] * cw_ref[1:2, :]
    xc = xc + xbuf[X0:X0 + tb, :] * cw_ref[2:3, :]
    return xc + xbuf[X0 + SUBLANES:X0 + SUBLANES + tb, :] * cw_ref[3:4, :]


def _rglru_scan(xc, wa_ref, wx_ref, ba_ref, bx_ref, lam_ref, h_ref, slabs, pbuf, lbuf, carry, *,
                reverse, tb):
    nslab = D_RG // LANES
    seg = tb // SUBLANES
    pitch = seg + SUBLANES
    row = lax.broadcasted_iota(jnp.int32, (SUBLANES, D_RG), 0)
    tile_row = lambda v, i: jnp.broadcast_to(v[i:i + 1, :], (SUBLANES, D_RG))

    xcb = xc.astype(BF16)
    r = jax.nn.sigmoid(_dot(xcb, wa_ref[...]) + ba_ref[...])
    i = jax.nn.sigmoid(_dot(xcb, wx_ref[...]) + bx_ref[...])
    decay_rate = RG_C * _softplus(-lam_ref[...])
    a = jnp.exp2((decay_rate * -1.4426950408889634) * r)
    y = jnp.tanh(decay_rate * r) * (a * a + 1.0)
    u = jnp.where(y > 0.0, y * lax.rsqrt(y), 0.0) * (i * xc)

    order = range(seg - 1, -1, -1) if reverse else range(seg)
    P = L = None
    for j in order:
        rs = slice(SUBLANES * j, SUBLANES * (j + 1))
        if P is None:
            P, L = a[rs], u[rs]
        else:
            P, L = a[rs] * P, a[rs] * L + u[rs]
        pbuf[rs, :] = P
        lbuf[rs, :] = L

    A, U = P, L
    for s in (1, 2, 4):
        if reverse:
            keep = row < SUBLANES - s
            shift = SUBLANES - s
        else:
            keep = row >= s
            shift = s
        a_sh = jnp.where(keep, pltpu.roll(A, shift, 0), 1.0)
        u_sh = jnp.where(keep, pltpu.roll(U, shift, 0), 0.0)
        U = A * u_sh + U
        A = A * a_sh
    c_in = carry[...]
    e = U + A * c_in
    if reverse:
        c_seg = jnp.where(row == SUBLANES - 1, c_in, pltpu.roll(e, SUBLANES - 1, 0))
        carry[...] = tile_row(e, 0)
    else:
        c_seg = jnp.where(row == 0, c_in, pltpu.roll(e, 1, 0))
        carry[...] = tile_row(e, SUBLANES - 1)

    for j in range(seg):
        rs = slice(SUBLANES * j, SUBLANES * (j + 1))
        hj = lbuf[rs, :] + pbuf[rs, :] * c_seg
        for k in range(nslab):
            slabs[k, pl.ds(j, SUBLANES, stride=pitch), :] = hj[:, LANES * k:LANES * (k + 1)]
    for k in range(nslab):
        for s in range(SUBLANES):
            h_ref[seg * s:seg * (s + 1), LANES * k:LANES * (k + 1)] = (
                slabs[k, pitch * s:pitch * s + seg, :].astype(h_ref.dtype))


def _rglru(x, cw, cb, gates_fwd, gates_bwd, *, B, S):
    T = B * S
    tile = TB_RG
    tb = TM_SUB
    npiece = tile // tb
    nblk = S // tile
    hb = tile // SUBLANES
    n_halo = T // SUBLANES
    full = lambda shape: pl.BlockSpec(shape, lambda b, j: (0,) * len(shape))
    gate_specs = [full((D_RG, D_RG)), full((D_RG, D_RG)), full((1, D_RG)), full((1, D_RG)), full((1, D_RG))]
    slabs = pltpu.VMEM((npiece, D_RG // LANES, tb + SUBLANES * SUBLANES, LANES), F32)
    scan_scratch = [
        pltpu.VMEM((npiece, tb, D_RG), F32),
        pltpu.VMEM((npiece, tb, D_RG), F32),
        pltpu.VMEM((SUBLANES, D_RG), F32),
    ]
    tile_fwd = pl.BlockSpec((tile, D_RG), lambda b, j: (b * nblk + j, 0))
    tile_bwd = pl.BlockSpec((tile, D_RG), lambda b, j: (b * nblk + nblk - 1 - j, 0))

    h_fwd, xc = pl.pallas_call(
        functools.partial(_rglru_fwd_kernel, nblk=nblk, tb=tb),
        grid=(B, nblk),
        in_specs=[
            tile_fwd,
            pl.BlockSpec((SUBLANES, D_RG), lambda b, j: (jnp.maximum((b * nblk + j) * hb - 1, 0), 0)),
            pl.BlockSpec((SUBLANES, D_RG), lambda b, j: (jnp.minimum((b * nblk + j + 1) * hb, n_halo - 1), 0)),
            full((4, D_RG)), full((1, D_RG)), *gate_specs,
        ],
        out_specs=[tile_fwd, tile_fwd],
        out_shape=[jax.ShapeDtypeStruct((T, D_RG), BF16),
                   jax.ShapeDtypeStruct((T, D_RG), F32)],
        scratch_shapes=[slabs, pltpu.VMEM((npiece, tb + 3 * SUBLANES, D_RG), F32), *scan_scratch],
        compiler_params=_params("parallel", "arbitrary"),
        name="rglru_fwd",
    )(x, x, x, cw, cb, *gates_fwd)
    h_bwd = pl.pallas_call(
        functools.partial(_rglru_bwd_kernel, tb=tb),
        grid=(B, nblk),
        in_specs=[tile_bwd, *gate_specs],
        out_specs=tile_bwd,
        out_shape=jax.ShapeDtypeStruct((T, D_RG), BF16),
        scratch_shapes=[slabs, *scan_scratch],
        compiler_params=_params("parallel", "arbitrary"),
        name="rglru_bwd",
    )(xc, *gates_bwd)
    return h_fwd, h_bwd


def _mlstm_kernel(qt_ref, k_ref, vt_ref, gr_ref, uc_ref, h_ref, ct_st, n_st, m_st, *, reverse, bb):
    L = ML_CHUNK
    NH = ML_HEADS
    d = 1 if reverse else 0

    @pl.when(pl.program_id(1) == 0)
    def _():
        ct_st[...] = jnp.zeros_like(ct_st)
        n_st[...] = jnp.zeros_like(n_st)
        m_st[...] = jnp.zeros_like(m_st)

    row8 = lax.broadcasted_iota(jnp.int32, (SUBLANES, L), 0)
    s_id = lax.broadcasted_iota(jnp.int32, (L, L), 0)
    t_id = lax.broadcasted_iota(jnp.int32, (L, L), 1)
    valid = (s_id >= t_id) if reverse else (s_id <= t_id)
    zeros8 = jnp.zeros((SUBLANES, L), F32)
    tile = lambda rows, hd: jnp.broadcast_to(rows[hd:hd + 1, :], (L, L))
    row_of = lambda rows, hd: jnp.broadcast_to(rows[hd:hd + 1, :], (SUBLANES, L))
    pairs = [(b, hp) for b in range(bb) for hp in range(NH // 2)]

    seq = []
    for b in range(bb):
        base = G_ROWS * d
        bcum = gr_ref[b, base + G_B:base + G_B + 8, :]
        u = gr_ref[b, base + G_U:base + G_U + 8, :]
        b_last = gr_ref[b, base + G_BL:base + G_BL + 8, :]
        m_prev = m_st[b]
        inter = bcum + m_prev
        m_t = jnp.maximum(inter, gr_ref[b, base + G_A:base + G_A + 8, :])
        m_new = jnp.maximum(b_last + m_prev, gr_ref[b, base + G_GM:base + G_GM + 8, :])
        n_prev = n_st[b]
        wg = jnp.exp(u + (b_last - m_new))
        seq.append(dict(
            w_int=jnp.exp(inter - m_t), e_neg=jnp.exp(-m_t), v3=_split3(bcum - m_t),
            decay=jnp.exp(b_last + m_prev - m_new), m_new=m_new, n_prev=n_prev, wg=wg,
            n_lhs=jnp.concatenate([n_prev, zeros8], axis=0).astype(BF16),
            wg_lhs=jnp.concatenate([wg, zeros8], axis=0).astype(BF16)))

    cs = lambda hd: slice(hd * ML_HD, (hd + 1) * ML_HD)
    side = lambda i: slice(i * L, (i + 1) * L)
    twice = lambda x: jnp.concatenate([x, x], axis=1)

    def blockdiag(a0, a1):
        z = jnp.zeros_like(a0)
        return jnp.concatenate([jnp.concatenate([a0, z], axis=1), jnp.concatenate([z, a1], axis=1)], axis=0)

    def expo_rhs(b, hd):
        onehot = jnp.where(row8 == hd, 1.0, 0.0)
        v_hi, v_mid, v_lo = (row_of(x, hd) for x in seq[b]["v3"])
        v_rows = jnp.where(row8 == 0, v_hi, jnp.where(row8 == 1, v_mid, jnp.where(row8 == 2, v_lo, 0.0)))
        slabs = [zeros8] * (L // SUBLANES)
        for j in range(3):
            slabs[(UC_DIR * d) // SUBLANES + j] = onehot
        slabs[UC_ONES // SUBLANES] = v_rows
        return jnp.concatenate(slabs, axis=0).astype(BF16)

    st, expo, inter, upd = {}, {}, {}, {}
    for b, hp in pairs:
        sq = seq[b]
        heads = (2 * hp, 2 * hp + 1)
        q_diag = blockdiag(qt_ref[b, cs(heads[0]), :], qt_ref[b, cs(heads[1]), :])
        sq_all = _dot(
            jnp.concatenate([k_ref[b, :, heads[0] * ML_HD:(heads[1] + 1) * ML_HD], twice(sq["n_lhs"]),
                             jnp.concatenate([ct_st[b, hd].astype(BF16) for hd in heads], axis=1)],
                            axis=0), q_diag)
        st[b, hp] = sq_all[0:L + BF16_ROWS]
        inter[b, hp] = sq_all[L + BF16_ROWS:]
        expo[b, hp] = _dot(uc_ref[b], jnp.concatenate([expo_rhs(b, hd) for hd in heads], axis=1))
        vw = jnp.concatenate([(vt_ref[b, cs(hd), :].astype(F32) * sq["wg"][hd:hd + 1, :]).astype(BF16)
                              for hd in heads], axis=1)
        upd[b, hp] = _dot(jnp.concatenate([vw, twice(sq["wg_lhs"])], axis=0),
                          blockdiag(k_ref[b, :, cs(heads[0])], k_ref[b, :, cs(heads[1])]))

    for b, hp in pairs:
        sq = seq[b]
        for i, hd in enumerate((2 * hp, 2 * hp + 1)):
            ct_st[b, hd] = tile(sq["decay"], hd) * ct_st[b, hd] + upd[b, hp][0:L, side(i)]
            n_st[b, hd:hd + 1, :] = (sq["decay"][hd:hd + 1, :] * sq["n_prev"][hd:hd + 1, :]
                                     + upd[b, hp][L + hd:L + hd + 1, side(i)])
    for b in range(bb):
        m_st[b] = seq[b]["m_new"]

    for b, hp in pairs:
        sq = seq[b]
        heads = (2 * hp, 2 * hp + 1)
        p_t, rw = [], []
        for i, hd in enumerate(heads):
            s_t = st[b, hp][0:L, side(i)] * jnp.exp(jnp.where(valid, expo[b, hp][:, side(i)], -jnp.inf))
            w_h = sq["w_int"][hd:hd + 1, :]
            den = jnp.sum(s_t, axis=0, keepdims=True) + w_h * st[b, hp][L + hd:L + hd + 1, side(i)]
            r = 1.0 / jnp.maximum(jnp.abs(den), sq["e_neg"][hd:hd + 1, :])
            p_t.append((s_t * r).astype(BF16))
            rw.append(r * w_h)
        intra = _dot(jnp.concatenate([vt_ref[b, cs(hd), :] for hd in heads], axis=1), blockdiag(*p_t))
        out = (intra + inter[b, hp] * jnp.concatenate(rw, axis=1)).astype(h_ref.dtype)
        for i, hd in enumerate(heads):
            h_ref[b, cs(hd), :] = out[:, side(i)]


def _mlstm(qt, k3, vt, gr4, uc3, *, reverse):
    B, S, _ = k3.shape
    L = ML_CHUNK
    nc = S // L
    bb = ML_SEQS

    def chunk(c):
        return (nc - 1 - c) if reverse else c

    kern = functools.partial(_mlstm_kernel, reverse=reverse, bb=bb)
    return pl.pallas_call(
        kern,
        grid=(B // bb, nc),
        in_specs=[
            pl.BlockSpec((bb, D_ML, L), lambda b, c: (b, 0, chunk(c))),
            pl.BlockSpec((bb, L, D_ML), lambda b, c: (b, chunk(c), 0)),
            pl.BlockSpec((bb, D_ML, L), lambda b, c: (b, 0, chunk(c))),
            pl.BlockSpec((bb, None, 2 * G_ROWS, L), lambda b, c: (b, chunk(c), 0, 0)),
            pl.BlockSpec((bb, L, LANES), lambda b, c: (b, chunk(c), 0)),
        ],
        out_specs=pl.BlockSpec((bb, D_ML, L), lambda b, c: (b, 0, chunk(c))),
        out_shape=jax.ShapeDtypeStruct((B, D_ML, S), BF16),
        scratch_shapes=[
            pltpu.VMEM((bb, ML_HEADS, ML_HD, ML_HD), F32),
            pltpu.VMEM((bb, 2 * ML_HEADS, ML_HD), F32),
            pltpu.VMEM((bb, 2 * ML_HEADS, LANES), F32),
        ],
        compiler_params=_params("parallel", "arbitrary"),
        name="mlstm_bwd" if reverse else "mlstm_fwd",
    )(qt, k3, vt, gr4, uc3)


def _outproj_kernel(rf_ref, rb_ref, gate_ref, mf_ref, mb_ref, ot_ref, mg_ref, wr_ref, wm_ref, y_ref):
    for r0 in range(0, y_ref.shape[0], TM_SUB):
        rows = slice(r0, r0 + TM_SUB)
        y_rg = _gelu_gate(gate_ref[rows, :].astype(F32),
                          rf_ref[rows, :].astype(F32) + rb_ref[rows, :].astype(F32))
        acc = _dot(y_rg.astype(BF16), wr_ref[...])
        h_t = mf_ref[:, rows].astype(F32) + mb_ref[:, rows].astype(F32)
        parts = []
        for hd in range(ML_HEADS):
            hh = h_t[hd * ML_HD:(hd + 1) * ML_HD]
            parts.append(hh * lax.rsqrt(jnp.mean(hh * hh, axis=0, keepdims=True) + EPS))
        mg = jnp.tile(mg_ref[...], (1, TM_SUB // LANES))
        y_t = jax.nn.sigmoid(ot_ref[:, rows].astype(F32)) * (jnp.concatenate(parts, axis=0) * mg)
        y_ref[rows, :] = acc + _dot(y_t.T.astype(BF16), wm_ref[...])


def _outproj(rf, rb, pa, mf_t, mb_t, o_t, mg_tile, w_rg, w_ml, *, S):
    T = rf.shape[0]
    tm = TM_PROJ
    nb = S // tm
    tok = lambda width, col: pl.BlockSpec((tm, width), lambda i: (i, col))
    seq_t = pl.BlockSpec((None, D_ML, tm), lambda i: (i // nb, 0, i % nb))
    full = lambda shape: pl.BlockSpec(shape, lambda i: (0,) * len(shape))
    return pl.pallas_call(
        _outproj_kernel,
        grid=(T // tm,),
        in_specs=[
            tok(D_RG, 0), tok(D_RG, 0), tok(D_RG, 0),
            seq_t, seq_t, seq_t,
            full((D_ML, LANES)), full((D_RG, D_MODEL)), full((D_ML, D_MODEL)),
        ],
        out_specs=tok(D_MODEL, 0),
        out_shape=jax.ShapeDtypeStruct((T, D_MODEL), F32),
        compiler_params=_params("parallel"),
        name="outproj",
    )(rf, rb, pa, mf_t, mb_t, o_t, mg_tile, w_rg, w_ml)


def _ffn_kernel(x_ref, prev_ref, next_ref, d_ref, dprev_ref, dnext_ref, g_ref, wu_ref, cw_ref, cb_ref,
                wd_ref, fg_ref, y_ref, slabs, hbuf, uvbuf, acts, acc, *, nblk, tb, final):
    blk = pl.program_id(0) % nblk
    sub = FF_SUB
    nsub = D_FF // sub
    nring = uvbuf.shape[0]
    nslab = D_MODEL // LANES
    seg = tb // SUBLANES
    pitch = seg + SUBLANES

    g = g_ref[...]
    hn = _rmsnorm(x_ref[...] + d_ref[...], g)
    for k in range(nslab):
        for s in range(SUBLANES):
            slabs[k, pitch * s:pitch * s + seg, :] = hn[seg * s:seg * (s + 1), LANES * k:LANES * (k + 1)]

    def perm_rows(j):
        return jnp.concatenate(
            [slabs[k, pl.ds(j, SUBLANES, stride=pitch), :] for k in range(nslab)], axis=1)

    for jj in range(seg // 2):
        hbuf[BF16_ROWS * jj:BF16_ROWS * (jj + 1), :] = jnp.concatenate(
            [perm_rows(2 * jj), perm_rows(2 * jj + 1)], axis=0).astype(BF16)
    row_x = lax.broadcasted_iota(jnp.int32, (SUBLANES, D_MODEL), 0)
    h_prev = jnp.where(blk == 0, 0.0, pltpu.roll(_rmsnorm(prev_ref[...] + dprev_ref[...], g), 1, 0))
    h_next = jnp.where(blk == nblk - 1, 0.0,
                       pltpu.roll(_rmsnorm(next_ref[...] + dnext_ref[...], g), 1, 0))
    halo = jnp.where(row_x == 0, h_prev, jnp.where(row_x == 1, h_next, 0.0))
    hbuf[tb:, :] = jnp.concatenate([halo, jnp.zeros_like(halo)], axis=0).astype(BF16)

    row_u = lax.broadcasted_iota(jnp.int32, (SUBLANES, 2 * sub), 0)

    def pair(ref, sc):
        return jnp.concatenate([ref[:, sub * sc:sub * (sc + 1)],
                                ref[:, D_FF + sub * sc:D_FF + sub * (sc + 1)]], axis=1)

    def up(sc):
        slot = sc % nring
        res = jnp.concatenate([_dot(hbuf[...], wu_ref[:, sub * sc:sub * (sc + 1)]),
                               _dot(hbuf[...], wu_ref[:, D_FF + sub * sc:D_FF + sub * (sc + 1)])], axis=1)
        uvbuf[slot, SUBLANES:SUBLANES + tb, :] = res[0:tb]
        uvbuf[slot, 0:SUBLANES, :] = jnp.where(
            row_u == 0, jnp.broadcast_to(res[tb:tb + 1], row_u.shape),
            pltpu.roll(res[tb - SUBLANES:tb], 1, 0))
        uvbuf[slot, SUBLANES + tb:, :] = jnp.where(
            row_u == SUBLANES - 1, jnp.broadcast_to(res[tb + 1:tb + 2], row_u.shape),
            pltpu.roll(res[0:SUBLANES], SUBLANES - 1, 0))

    def gate(sc):
        slot = sc % nring
        cw = pair(cw_ref, sc)
        c = pair(cb_ref, sc) + uvbuf[slot, 0:tb, :] * cw[0:1]
        c = c + uvbuf[slot, SUBLANES:SUBLANES + tb, :] * cw[1:2]
        c = c + uvbuf[slot, 2 * SUBLANES:2 * SUBLANES + tb, :] * cw[2:3]
        return _gelu_gate(c[:, :sub], c[:, sub:]).astype(BF16)

    up(0)
    up(1)
    for sc in range(nsub):
        if sc + 2 < nsub:
            up(sc + 2)
        acts[:, sub * sc:sub * (sc + 1)] = gate(sc)

    acc[...] = _dot(acts[...], wd_ref[...])

    for j in range(seg):
        for k in range(nslab):
            slabs[k, pl.ds(j, SUBLANES, stride=pitch), :] = acc[SUBLANES * j:SUBLANES * (j + 1),
                                                                LANES * k:LANES * (k + 1)]
    ffn = jnp.concatenate(
        [jnp.concatenate([slabs[k, pitch * s:pitch * s + seg, :] for s in range(SUBLANES)], axis=0)
         for k in range(nslab)], axis=1)
    y = (x_ref[...] + d_ref[...]) + ffn
    if final:
        y = _rmsnorm(y, fg_ref[...])
    y_ref[...] = y


def _ffn(x2, d2, g, w_up, cw, cb, w_down, fg, *, S, final):
    T = x2.shape[0]
    tb = TB_FFN
    nblk = S // tb
    hpb = tb // SUBLANES
    n_halo = T // SUBLANES
    full = lambda shape: pl.BlockSpec(shape, lambda i: (0,) * len(shape))
    tile = pl.BlockSpec((tb, D_MODEL), lambda i: (i, 0))
    halo_prev = pl.BlockSpec((SUBLANES, D_MODEL), lambda i: (jnp.maximum(i * hpb - 1, 0), 0))
    halo_next = pl.BlockSpec((SUBLANES, D_MODEL), lambda i: (jnp.minimum((i + 1) * hpb, n_halo - 1), 0))
    kern = functools.partial(_ffn_kernel, nblk=nblk, tb=tb, final=final)
    return pl.pallas_call(
        kern,
        grid=(T // tb,),
        in_specs=[
            tile, halo_prev, halo_next, tile, halo_prev, halo_next,
            full((1, D_MODEL)), full((D_MODEL, 2 * D_FF)), full((3, 2 * D_FF)), full((1, 2 * D_FF)),
            full((D_FF, D_MODEL)), full((1, D_MODEL)),
        ],
        out_specs=pl.BlockSpec((tb, D_MODEL), lambda i: (i, 0)),
        out_shape=jax.ShapeDtypeStruct((T, D_MODEL), F32),
        scratch_shapes=[
            pltpu.VMEM((D_MODEL // LANES, tb + SUBLANES * SUBLANES, LANES), F32),
            pltpu.VMEM((tb + BF16_ROWS, D_MODEL), BF16),
            pltpu.VMEM((FF_RING, tb + 2 * SUBLANES, 2 * FF_SUB), F32),
            pltpu.VMEM((tb, D_FF), BF16),
            pltpu.VMEM((tb, D_MODEL), F32),
        ],
        compiler_params=pltpu.CompilerParams(dimension_semantics=("parallel",),
                                             vmem_limit_bytes=VMEM_LIMIT_FFN),
        name="convffn",
    )(x2, x2, x2, d2, d2, d2, g, w_up, cw, cb, w_down, fg)


def _block_diag(w):
    eye = jnp.eye(RG_BLOCKS, dtype=w.dtype)
    return jnp.einsum('ncd,nm->ncmd', w, eye).reshape(D_RG, D_RG)


def _encoder(x, norm1_g, w_in, b_gates, rg_conv_w, rg_conv_b, rg_wa, rg_ba, rg_wx, rg_bx, rg_lambda,
             ml_norm_g, w_out, norm2_g, w_up, ffn_conv_w, ffn_conv_b, w_down, final_g):
    B, S, _ = x.shape
    T = B * S
    depth = w_in.shape[0]
    x2 = x.reshape(T, D_MODEL)
    row = lambda v: v.reshape(1, -1).astype(F32)
    c_q, c_k, c_v, c_g = 2 * D_RG, 2 * D_RG + D_ML, 2 * D_RG + 2 * D_ML, 2 * D_RG + 4 * D_ML
    for l in range(depth):
        w_nat = jnp.concatenate([w_in[l, :, :c_q], w_in[l, :, c_k:c_v]], axis=1).astype(BF16)
        w_tr = jnp.concatenate([w_in[l, :, c_q:c_k], w_in[l, :, c_v:c_g]], axis=1).T.astype(BF16)
        w_gate = w_in[l, :, c_g:].T.astype(BF16)
        bias = jnp.broadcast_to(b_gates[l].astype(F32).reshape(N_GATE, 1), (N_GATE, ML_CHUNK))
        rx, rgate, kn, q_t, v_t, o_t, gr, uc = _inproj(x2, row(norm1_g[l]), w_nat, w_tr, w_gate, bias,
                                                       B=B, S=S)
        k3 = kn.reshape(B, S, D_ML)
        gr4 = gr.reshape(B, S // ML_CHUNK, 2 * G_ROWS, ML_CHUNK)
        uc3 = uc.reshape(B, S, LANES)
        rg_gates = [(_block_diag(rg_wa[l, d]).astype(BF16), _block_diag(rg_wx[l, d]).astype(BF16),
                     row(rg_ba[l, d]), row(rg_bx[l, d]), row(rg_lambda[l, d])) for d in range(2)]
        r_dir = _rglru(rx, rg_conv_w[l].astype(F32), row(rg_conv_b[l]), *rg_gates, B=B, S=S)
        m_dir = [_mlstm(q_t, k3, v_t, gr4, uc3, reverse=reverse) for reverse in (False, True)]
        wo = w_out[l].astype(BF16)
        mg_tile = jnp.broadcast_to(ml_norm_g[l].astype(F32).reshape(D_ML, 1), (D_ML, LANES))
        mixed = _outproj(r_dir[0], r_dir[1], rgate, m_dir[0], m_dir[1], o_t, mg_tile,
                         wo[:D_RG], wo[D_RG:], S=S)
        x2 = _ffn(x2, mixed, row(norm2_g[l]), w_up[l].astype(BF16),
                  ffn_conv_w[l].astype(F32), row(ffn_conv_b[l]),
                  w_down[l].astype(BF16), row(final_g), S=S,
                  final=(l == depth - 1))
    return x2.reshape(B, S, D_MODEL)


def kernel(x_prompt, x_sample, norm1_g, w_in, b_gates, rg_conv_w, rg_conv_b, rg_wa, rg_ba, rg_wx, rg_bx,
           rg_lambda, ml_norm_g, w_out, norm2_g, w_up, ffn_conv_w, ffn_conv_b, w_down, final_g):
    weights = (norm1_g, w_in, b_gates, rg_conv_w, rg_conv_b, rg_wa, rg_ba, rg_wx, rg_bx, rg_lambda,
               ml_norm_g, w_out, norm2_g, w_up, ffn_conv_w, ffn_conv_b, w_down, final_g)
    return (_encoder(x_prompt, *weights), _encoder(x_sample, *weights))
```

```python
import functools

import jax
import jax.numpy as jnp
from jax import lax
from jax.experimental import pallas as pl
from jax.experimental.pallas import tpu as pltpu

F32 = jnp.float32
BF16 = jnp.bfloat16

D_MODEL = 1024
D_RG = 512
D_ML = 512
RG_BLOCKS = 8
RG_C = 8.0
ML_HEADS = 4
ML_HD = 128
ML_CHUNK = 128
D_FF = 3072
EPS = 1e-6
N_GATE = 4 * ML_HEADS

SUBLANES = 8
LANES = 128
BF16_ROWS = 16
VMEM_LIMIT = 48 * 1024 * 1024

TM_PROJ = 1024
TM_SUB = 512
TB_RG = 1024
ML_SEQS = 16
TB_FFN = 512
FF_SUB = 256
FF_RING = 4
VMEM_LIMIT_FFN = 56 * 1024 * 1024

G_B, G_U, G_A, G_BL, G_GM = 0, 8, 16, 24, 32
G_ROWS = 40
UC_DIR = 24
UC_ONES = 2 * UC_DIR


def _params(*sem):
    return pltpu.CompilerParams(dimension_semantics=sem, vmem_limit_bytes=VMEM_LIMIT)


def _softplus(z):
    return jnp.maximum(z, 0.0) + jnp.log1p(jnp.exp(-jnp.abs(z)))


def _gelu_gate(gate, val):
    k0 = -2.0 * 0.7978845608028654 * 1.4426950408889634
    z = gate * (k0 + (k0 * 0.044715) * (gate * gate))
    return (gate * val) / (1.0 + jnp.exp2(z))


def _rmsnorm(x, g):
    return x * lax.rsqrt(jnp.mean(x * x, axis=-1, keepdims=True) + EPS) * g


def _dot(a, b):
    return jnp.dot(a, b, preferred_element_type=F32)


def _dot_nt(a, b):
    return lax.dot_general(a, b, (((1,), (1,)), ((), ())), preferred_element_type=F32)


def _split3(x):
    hi = x.astype(BF16).astype(F32)
    r1 = x - hi
    mid = r1.astype(BF16).astype(F32)
    return hi, mid, (r1 - mid).astype(BF16).astype(F32)


def _lane_scan(x, op, fill, reverse):
    n = x.shape[-1]
    lane = lax.broadcasted_iota(jnp.int32, x.shape, 1)
    s = 1
    while s < n:
        if reverse:
            x = op(x, jnp.where(lane < n - s, pltpu.roll(x, n - s, 1), fill))
        else:
            x = op(x, jnp.where(lane >= s, pltpu.roll(x, s, 1), fill))
        s *= 2
    return x


def _inproj_kernel(x_ref, g_ref, wn_ref, wt_ref, wg_ref, bias_ref, rx_ref, rg_ref, qk_ref, vt_ref,
                   ot_ref, gr_ref, uc_ref):
    for r0 in range(0, x_ref.shape[0], TM_SUB):
        _inproj_piece(x_ref, g_ref, wn_ref, wt_ref, wg_ref, bias_ref, rx_ref, rg_ref, qk_ref, vt_ref,
                      ot_ref, gr_ref, uc_ref, r0)


def _inproj_piece(x_ref, g_ref, wn_ref, wt_ref, wg_ref, bias_ref, rx_ref, rg_ref, qk_ref, vt_ref,
                  ot_ref, gr_ref, uc_ref, r0):
    L = ML_CHUNK
    NH = ML_HEADS
    rows = slice(r0, r0 + TM_SUB)
    h = _rmsnorm(x_ref[rows, :], g_ref[...]).astype(BF16)

    gt = _dot_nt(wg_ref[...], h)
    rowid = lax.broadcasted_iota(jnp.int32, (2 * NH, L), 0)
    head_row = rowid < NH
    rep = lambda col: jnp.broadcast_to(col, (2 * NH, L))
    zeros8 = jnp.zeros((2 * NH, L), F32)
    for cl in range(TM_SUB // L):
        c = r0 // L + cl
        g16 = gt[:, cl * L:(cl + 1) * L] + bias_ref[...]
        tiles = []
        for d, reverse in enumerate((False, True)):
            gates = g16[2 * NH * d:2 * NH * (d + 1)]
            lf = jnp.where(head_row, 0.0, -_softplus(-gates))
            bcum = pltpu.roll(_lane_scan(lf, jnp.add, 0.0, reverse), NH, 0)
            u = jnp.where(head_row, gates - bcum, 0.0)
            last = 0 if reverse else L - 1
            bl = rep(bcum[:, last:last + 1])
            base = G_ROWS * d
            gr_ref[c, base + G_B:base + G_B + 8, :] = bcum
            gr_ref[c, base + G_U:base + G_U + 8, :] = u
            gr_ref[c, base + G_A:base + G_A + 8, :] = bcum + _lane_scan(u, jnp.maximum, -jnp.inf, reverse)
            gr_ref[c, base + G_BL:base + G_BL + 8, :] = bl
            gr_ref[c, base + G_GM:base + G_GM + 8, :] = rep(jnp.max(bl + u, axis=-1, keepdims=True))
            tiles.extend(_split3(u))
        tiles.append(jnp.ones((2 * NH, L), F32))
        tiles.extend([zeros8] * (L // 8 - len(tiles)))
        uc_ref[c * L:(c + 1) * L, :] = jnp.concatenate(tiles, axis=0).T.astype(BF16)

    nat = _dot(h, wn_ref[...])
    rx_ref[rows, :] = nat[:, :D_RG]
    rg_ref[rows, :] = nat[:, D_RG:2 * D_RG].astype(BF16)
    qk_ref[rows, :] = jnp.concatenate(
        [nat[:, 2 * D_RG:2 * D_RG + D_ML] * (ML_HD ** -0.5), nat[:, 2 * D_RG + D_ML:]], axis=-1).astype(BF16)
    tr = _dot_nt(wt_ref[...], h)
    vt_ref[:, rows] = tr[:D_ML].astype(BF16)
    ot_ref[:, rows] = tr[D_ML:].astype(BF16)


def _inproj(x2, g, w_nat, w_tr, w_gate, bias, *, B, S):
    T = B * S
    tm = TM_PROJ
    nb = S // tm
    full = lambda shape: pl.BlockSpec(shape, lambda i: (0,) * len(shape))
    tok = lambda width: pl.BlockSpec((tm, width), lambda i: (i, 0))
    seq_t = pl.BlockSpec((None, D_ML, tm), lambda i: (i // nb, 0, i % nb))
    return pl.pallas_call(
        _inproj_kernel,
        grid=(T // tm,),
        in_specs=[
            tok(D_MODEL), full((1, D_MODEL)), full(w_nat.shape), full(w_tr.shape), full(w_gate.shape),
            full((N_GATE, ML_CHUNK)),
        ],
        out_specs=[
            tok(D_RG), tok(D_RG), tok(2 * D_ML), seq_t, seq_t,
            pl.BlockSpec((tm // ML_CHUNK, 2 * G_ROWS, ML_CHUNK), lambda i: (i, 0, 0)),
            tok(LANES),
        ],
        out_shape=[
            jax.ShapeDtypeStruct((T, D_RG), F32),
            jax.ShapeDtypeStruct((T, D_RG), BF16),
            jax.ShapeDtypeStruct((T, 2 * D_ML), BF16),
            jax.ShapeDtypeStruct((B, D_ML, S), BF16),
            jax.ShapeDtypeStruct((B, D_ML, S), BF16),
            jax.ShapeDtypeStruct((T // ML_CHUNK, 2 * G_ROWS, ML_CHUNK), F32),
            jax.ShapeDtypeStruct((T, LANES), BF16),
        ],
        compiler_params=_params("parallel"),
        name="inproj",
    )(x2, g, w_nat, w_tr, w_gate, bias)


def _rglru_fwd_kernel(x_ref, prev_ref, next_ref, cw_ref, cb_ref, wa_ref, wx_ref, ba_ref, bx_ref,
                      lam_ref, h_ref, xc_ref, slabs, xbuf, pbuf, lbuf, carry, *, nblk, tb):
    blk = pl.program_id(1)

    @pl.when(blk == 0)
    def _():
        carry[...] = jnp.zeros_like(carry)

    npiece = x_ref.shape[0] // tb
    for p in range(npiece):
        rows = pl.ds(tb * p, tb)
        prev = (jnp.where(blk == 0, 0.0, prev_ref[...]) if p == 0
                else x_ref[tb * p - SUBLANES:tb * p, :])
        nxt = (jnp.where(blk == nblk - 1, 0.0, next_ref[...]) if p == npiece - 1
               else x_ref[tb * (p + 1):tb * (p + 1) + SUBLANES, :])
        xc = _rglru_conv(x_ref[rows, :], prev, nxt, cw_ref, cb_ref, slabs.at[p], xbuf.at[p], tb=tb)
        xc_ref[rows, :] = xc
        _rglru_scan(xc, wa_ref, wx_ref, ba_ref, bx_ref, lam_ref, h_ref.at[rows], slabs.at[p],
                    pbuf.at[p], lbuf.at[p], carry, reverse=False, tb=tb)


def _rglru_bwd_kernel(xc_ref, wa_ref, wx_ref, ba_ref, bx_ref, lam_ref, h_ref, slabs, pbuf, lbuf,
                      carry, *, tb):
    @pl.when(pl.program_id(1) == 0)
    def _():
        carry[...] = jnp.zeros_like(carry)

    for p in reversed(range(xc_ref.shape[0] // tb)):
        rows = pl.ds(tb * p, tb)
        _rglru_scan(xc_ref[rows, :], wa_ref, wx_ref, ba_ref, bx_ref, lam_ref, h_ref.at[rows],
                    slabs.at[p], pbuf.at[p], lbuf.at[p], carry, reverse=True, tb=tb)


def _rglru_conv(x, prev, nxt, cw_ref, cb_ref, slabs, xbuf, *, tb):
    nslab = D_RG // LANES
    seg = tb // SUBLANES
    pitch = seg + SUBLANES
    X0 = 2 * SUBLANES

    for k in range(nslab):
        for s in range(SUBLANES):
            slabs[k, pitch * s:pitch * s + seg, :] = x[seg * s:seg * (s + 1), LANES * k:LANES * (k + 1)]
    seam = {}
    for j in range(seg):
        rows = jnp.concatenate(
            [slabs[k, pl.ds(j, SUBLANES, stride=pitch), :] for k in range(nslab)], axis=1)
        xbuf[X0 + SUBLANES * j:X0 + SUBLANES * (j + 1), :] = rows
        if j in (0, seg - 2, seg - 1):
            seam[j] = rows
    row = lax.broadcasted_iota(jnp.int32, (SUBLANES, D_RG), 0)
    tile_row = lambda v, i: jnp.broadcast_to(v[i:i + 1, :], (SUBLANES, D_RG))
    xbuf[0:SUBLANES, :] = jnp.where(row == 0, tile_row(prev, SUBLANES - 2), pltpu.roll(seam[seg - 2], 1, 0))
    xbuf[SUBLANES:X0, :] = jnp.where(row == 0, tile_row(prev, SUBLANES - 1), pltpu.roll(seam[seg - 1], 1, 0))
    xbuf[X0 + tb:, :] = jnp.where(row == SUBLANES - 1, tile_row(nxt, 0),
                                  pltpu.roll(seam[0], SUBLANES - 1, 0))
    xc = cb_ref[...] + xbuf[0:tb, :] * cw_ref[0:1, :]
    xc = xc + xbuf[SUBLANES:SUBLANES + tb, :] * cw_ref[1:2, :]
    xc = xc + xbuf[X0:X0 + tb, :] * cw_ref[2:3, :]
    return xc + xbuf[X0 + SUBLANES:X0 + SUBLANES + tb, :] * cw_ref[3:4, :]


def _rglru_scan(xc, wa_ref, wx_ref, ba_ref, bx_ref, lam_ref, h_ref, slabs, pbuf, lbuf, carry, *,
                reverse, tb):
    nslab = D_RG // LANES
    seg = tb // SUBLANES
    pitch = seg + SUBLANES
    row = lax.broadcasted_iota(jnp.int32, (SUBLANES, D_RG), 0)
    tile_row = lambda v, i: jnp.broadcast_to(v[i:i + 1, :], (SUBLANES, D_RG))

    xcb = xc.astype(BF16)
    r = jax.nn.sigmoid(_dot(xcb, wa_ref[...]) + ba_ref[...])
    i = jax.nn.sigmoid(_dot(xcb, wx_ref[...]) + bx_ref[...])
    decay_rate = RG_C * _softplus(-lam_ref[...])
    a = jnp.exp2((decay_rate * -1.4426950408889634) * r)
    y = jnp.tanh(decay_rate * r) * (a * a + 1.0)
    u = jnp.where(y > 0.0, y * lax.rsqrt(y), 0.0) * (i * xc)

    order = range(seg - 1, -1, -1) if reverse else range(seg)
    P = L = None
    for j in order:
        rs = slice(SUBLANES * j, SUBLANES * (j + 1))
        if P is None:
            P, L = a[rs], u[rs]
        else:
            P, L = a[rs] * P, a[rs] * L + u[rs]
        pbuf[rs, :] = P
        lbuf[rs, :] = L

    A, U = P, L
    for s in (1, 2, 4):
        if reverse:
            keep = row < SUBLANES - s
            shift = SUBLANES - s
        else:
            keep = row >= s
            shift = s
        a_sh = jnp.where(keep, pltpu.roll(A, shift, 0), 1.0)
        u_sh = jnp.where(keep, pltpu.roll(U, shift, 0), 0.0)
        U = A * u_sh + U
        A = A * a_sh
    c_in = carry[...]
    e = U + A * c_in
    if reverse:
        c_seg = jnp.where(row == SUBLANES - 1, c_in, pltpu.roll(e, SUBLANES - 1, 0))
        carry[...] = tile_row(e, 0)
    else:
        c_seg = jnp.where(row == 0, c_in, pltpu.roll(e, 1, 0))
        carry[...] = tile_row(e, SUBLANES - 1)

    for j in range(seg):
        rs = slice(SUBLANES * j, SUBLANES * (j + 1))
        hj = lbuf[rs, :] + pbuf[rs, :] * c_seg
        for k in range(nslab):
            slabs[k, pl.ds(j, SUBLANES, stride=pitch), :] = hj[:, LANES * k:LANES * (k + 1)]
    for k in range(nslab):
        for s in range(SUBLANES):
            h_ref[seg * s:seg * (s + 1), LANES * k:LANES * (k + 1)] = (
                slabs[k, pitch * s:pitch * s + seg, :].astype(h_ref.dtype))


def _rglru(x, cw, cb, gates_fwd, gates_bwd, *, B, S):
    T = B * S
    tile = TB_RG
    tb = TM_SUB
    npiece = tile // tb
    nblk = S // tile
    hb = tile // SUBLANES
    n_halo = T // SUBLANES
    full = lambda shape: pl.BlockSpec(shape, lambda b, j: (0,) * len(shape))
    gate_specs = [full((D_RG, D_RG)), full((D_RG, D_RG)), full((1, D_RG)), full((1, D_RG)), full((1, D_RG))]
    slabs = pltpu.VMEM((npiece, D_RG // LANES, tb + SUBLANES * SUBLANES, LANES), F32)
    scan_scratch = [
        pltpu.VMEM((npiece, tb, D_RG), F32),
        pltpu.VMEM((npiece, tb, D_RG), F32),
        pltpu.VMEM((SUBLANES, D_RG), F32),
    ]
    tile_fwd = pl.BlockSpec((tile, D_RG), lambda b, j: (b * nblk + j, 0))
    tile_bwd = pl.BlockSpec((tile, D_RG), lambda b, j: (b * nblk + nblk - 1 - j, 0))

    h_fwd, xc = pl.pallas_call(
        functools.partial(_rglru_fwd_kernel, nblk=nblk, tb=tb),
        grid=(B, nblk),
        in_specs=[
            tile_fwd,
            pl.BlockSpec((SUBLANES, D_RG), lambda b, j: (jnp.maximum((b * nblk + j) * hb - 1, 0), 0)),
            pl.BlockSpec((SUBLANES, D_RG), lambda b, j: (jnp.minimum((b * nblk + j + 1) * hb, n_halo - 1), 0)),
            full((4, D_RG)), full((1, D_RG)), *gate_specs,
        ],
        out_specs=[tile_fwd, tile_fwd],
        out_shape=[jax.ShapeDtypeStruct((T, D_RG), BF16),
                   jax.ShapeDtypeStruct((T, D_RG), F32)],
        scratch_shapes=[slabs, pltpu.VMEM((npiece, tb + 3 * SUBLANES, D_RG), F32), *scan_scratch],
        compiler_params=_params("parallel", "arbitrary"),
        name="rglru_fwd",
    )(x, x, x, cw, cb, *gates_fwd)
    h_bwd = pl.pallas_call(
        functools.partial(_rglru_bwd_kernel, tb=tb),
        grid=(B, nblk),
        in_specs=[tile_bwd, *gate_specs],
        out_specs=tile_bwd,
        out_shape=jax.ShapeDtypeStruct((T, D_RG), BF16),
        scratch_shapes=[slabs, *scan_scratch],
        compiler_params=_params("parallel", "arbitrary"),
        name="rglru_bwd",
    )(xc, *gates_bwd)
    return h_fwd, h_bwd


def _mlstm_kernel(q_ref, k_ref, vt_ref, gr_ref, uc_ref, h_ref, ct_st, n_st, m_st, *, reverse, bb):
    L = ML_CHUNK
    NH = ML_HEADS
    d = 1 if reverse else 0

    @pl.when(pl.program_id(1) == 0)
    def _():
        ct_st[...] = jnp.zeros_like(ct_st)
        n_st[...] = jnp.zeros_like(n_st)
        m_st[...] = jnp.zeros_like(m_st)

    row8 = lax.broadcasted_iota(jnp.int32, (SUBLANES, L), 0)
    s_id = lax.broadcasted_iota(jnp.int32, (L, L), 0)
    t_id = lax.broadcasted_iota(jnp.int32, (L, L), 1)
    valid = (s_id >= t_id) if reverse else (s_id <= t_id)
    zeros8 = jnp.zeros((SUBLANES, L), F32)
    tile = lambda rows, hd: jnp.broadcast_to(rows[hd:hd + 1, :], (L, L))
    row_of = lambda rows, hd: jnp.broadcast_to(rows[hd:hd + 1, :], (SUBLANES, L))
    pairs = [(b, hp) for b in range(bb) for hp in range(NH // 2)]

    seq = []
    for b in range(bb):
        base = G_ROWS * d
        bcum = gr_ref[b, base + G_B:base + G_B + 8, :]
        u = gr_ref[b, base + G_U:base + G_U + 8, :]
        b_last = gr_ref[b, base + G_BL:base + G_BL + 8, :]
        m_prev = m_st[b]
        inter = bcum + m_prev
        m_t = jnp.maximum(inter, gr_ref[b, base + G_A:base + G_A + 8, :])
        m_new = jnp.maximum(b_last + m_prev, gr_ref[b, base + G_GM:base + G_GM + 8, :])
        n_prev = n_st[b]
        wg = jnp.exp(u + (b_last - m_new))
        seq.append(dict(
            w_int=jnp.exp(inter - m_t), e_neg=jnp.exp(-m_t), v3=_split3(bcum - m_t),
            decay=jnp.exp(b_last + m_prev - m_new), m_new=m_new, n_prev=n_prev, wg=wg,
            n_lhs=jnp.concatenate([n_prev, zeros8], axis=0).astype(BF16),
            wg_lhs=jnp.concatenate([wg, zeros8], axis=0).astype(BF16)))

    cs = lambda hd: slice(hd * ML_HD, (hd + 1) * ML_HD)
    side = lambda i: slice(i * L, (i + 1) * L)
    twice = lambda x: jnp.concatenate([x, x], axis=1)

    def blockdiag(a0, a1):
        z = jnp.zeros_like(a0)
        return jnp.concatenate([jnp.concatenate([a0, z], axis=1), jnp.concatenate([z, a1], axis=1)], axis=0)

    def expo_rhs(b, hd):
        onehot = jnp.where(row8 == hd, 1.0, 0.0)
        v_hi, v_mid, v_lo = (row_of(x, hd) for x in seq[b]["v3"])
        v_rows = jnp.where(row8 == 0, v_hi, jnp.where(row8 == 1, v_mid, jnp.where(row8 == 2, v_lo, 0.0)))
        slabs = [zeros8] * (L // SUBLANES)
        for j in range(3):
            slabs[(UC_DIR * d) // SUBLANES + j] = onehot
        slabs[UC_ONES // SUBLANES] = v_rows
        return jnp.concatenate(slabs, axis=0).astype(BF16)

    st, expo, inter, upd = {}, {}, {}, {}
    for b, hp in pairs:
        sq = seq[b]
        heads = (2 * hp, 2 * hp + 1)
        q_diag = blockdiag(q_ref[b, :, cs(heads[0])], q_ref[b, :, cs(heads[1])])
        sq_all = _dot_nt(
            jnp.concatenate([k_ref[b, :, heads[0] * ML_HD:(heads[1] + 1) * ML_HD], twice(sq["n_lhs"]),
                             jnp.concatenate([ct_st[b, hd].astype(BF16) for hd in heads], axis=1)],
                            axis=0), q_diag)
        st[b, hp] = sq_all[0:L + BF16_ROWS]
        inter[b, hp] = sq_all[L + BF16_ROWS:]
        expo[b, hp] = _dot(uc_ref[b], jnp.concatenate([expo_rhs(b, hd) for hd in heads], axis=1))
        vw = jnp.concatenate([(vt_ref[b, cs(hd), :].astype(F32) * sq["wg"][hd:hd + 1, :]).astype(BF16)
                              for hd in heads], axis=1)
        upd[b, hp] = _dot(jnp.concatenate([vw, twice(sq["wg_lhs"])], axis=0),
                          blockdiag(k_ref[b, :, cs(heads[0])], k_ref[b, :, cs(heads[1])]))

    for b, hp in pairs:
        sq = seq[b]
        for i, hd in enumerate((2 * hp, 2 * hp + 1)):
            ct_st[b, hd] = tile(sq["decay"], hd) * ct_st[b, hd] + upd[b, hp][0:L, side(i)]
            n_st[b, hd:hd + 1, :] = (sq["decay"][hd:hd + 1, :] * sq["n_prev"][hd:hd + 1, :]
                                     + upd[b, hp][L + hd:L + hd + 1, side(i)])
    for b in range(bb):
        m_st[b] = seq[b]["m_new"]

    for b, hp in pairs:
        sq = seq[b]
        heads = (2 * hp, 2 * hp + 1)
        p_t, rw = [], []
        for i, hd in enumerate(heads):
            s_t = st[b, hp][0:L, side(i)] * jnp.exp(jnp.where(valid, expo[b, hp][:, side(i)], -jnp.inf))
            w_h = sq["w_int"][hd:hd + 1, :]
            den = jnp.sum(s_t, axis=0, keepdims=True) + w_h * st[b, hp][L + hd:L + hd + 1, side(i)]
            r = 1.0 / jnp.maximum(jnp.abs(den), sq["e_neg"][hd:hd + 1, :])
            p_t.append((s_t * r).astype(BF16))
            rw.append(r * w_h)
        intra = _dot(jnp.concatenate([vt_ref[b, cs(hd), :] for hd in heads], axis=1), blockdiag(*p_t))
        out = (intra + inter[b, hp] * jnp.concatenate(rw, axis=1)).astype(h_ref.dtype)
        for i, hd in enumerate(heads):
            h_ref[b, cs(hd), :] = out[:, side(i)]


def _mlstm(qk3, vt, gr4, uc3, *, reverse):
    B, S, _ = qk3.shape
    L = ML_CHUNK
    nc = S // L
    bb = ML_SEQS

    def chunk(c):
        return (nc - 1 - c) if reverse else c

    kern = functools.partial(_mlstm_kernel, reverse=reverse, bb=bb)
    return pl.pallas_call(
        kern,
        grid=(B // bb, nc),
        in_specs=[
            pl.BlockSpec((bb, L, D_ML), lambda b, c: (b, chunk(c), 0)),
            pl.BlockSpec((bb, L, D_ML), lambda b, c: (b, chunk(c), 1)),
            pl.BlockSpec((bb, D_ML, L), lambda b, c: (b, 0, chunk(c))),
            pl.BlockSpec((bb, None, 2 * G_ROWS, L), lambda b, c: (b, chunk(c), 0, 0)),
            pl.BlockSpec((bb, L, LANES), lambda b, c: (b, chunk(c), 0)),
        ],
        out_specs=pl.BlockSpec((bb, D_ML, L), lambda b, c: (b, 0, chunk(c))),
        out_shape=jax.ShapeDtypeStruct((B, D_ML, S), BF16),
        scratch_shapes=[
            pltpu.VMEM((bb, ML_HEADS, ML_HD, ML_HD), F32),
            pltpu.VMEM((bb, 2 * ML_HEADS, ML_HD), F32),
            pltpu.VMEM((bb, 2 * ML_HEADS, LANES), F32),
        ],
        compiler_params=_params("parallel", "arbitrary"),
        name="mlstm_bwd" if reverse else "mlstm_fwd",
    )(qk3, qk3, vt, gr4, uc3)


def _outproj_kernel(rf_ref, rb_ref, gate_ref, mf_ref, mb_ref, ot_ref, mg_ref, wr_ref, wm_ref, y_ref):
    for r0 in range(0, y_ref.shape[0], TM_SUB):
        rows = slice(r0, r0 + TM_SUB)
        y_rg = _gelu_gate(gate_ref[rows, :].astype(F32),
                          rf_ref[rows, :].astype(F32) + rb_ref[rows, :].astype(F32))
        acc = _dot(y_rg.astype(BF16), wr_ref[...])
        h_t = mf_ref[:, rows].astype(F32) + mb_ref[:, rows].astype(F32)
        parts = []
        for hd in range(ML_HEADS):
            hh = h_t[hd * ML_HD:(hd + 1) * ML_HD]
            parts.append(hh * lax.rsqrt(jnp.mean(hh * hh, axis=0, keepdims=True) + EPS))
        mg = jnp.tile(mg_ref[...], (1, TM_SUB // LANES))
        y_t = jax.nn.sigmoid(ot_ref[:, rows].astype(F32)) * (jnp.concatenate(parts, axis=0) * mg)
        y_ref[rows, :] = acc + _dot(y_t.T.astype(BF16), wm_ref[...])


def _outproj(rf, rb, pa, mf_t, mb_t, o_t, mg_tile, w_rg, w_ml, *, S):
    T = rf.shape[0]
    tm = TM_PROJ
    nb = S // tm
    tok = lambda width, col: pl.BlockSpec((tm, width), lambda i: (i, col))
    seq_t = pl.BlockSpec((None, D_ML, tm), lambda i: (i // nb, 0, i % nb))
    full = lambda shape: pl.BlockSpec(shape, lambda i: (0,) * len(shape))
    return pl.pallas_call(
        _outproj_kernel,
        grid=(T // tm,),
        in_specs=[
            tok(D_RG, 0), tok(D_RG, 0), tok(D_RG, 0),
            seq_t, seq_t, seq_t,
            full((D_ML, LANES)), full((D_RG, D_MODEL)), full((D_ML, D_MODEL)),
        ],
        out_specs=tok(D_MODEL, 0),
        out_shape=jax.ShapeDtypeStruct((T, D_MODEL), F32),
        compiler_params=_params("parallel"),
        name="outproj",
    )(rf, rb, pa, mf_t, mb_t, o_t, mg_tile, w_rg, w_ml)


def _ffn_kernel(x_ref, prev_ref, next_ref, d_ref, dprev_ref, dnext_ref, g_ref, wu_ref, cw_ref, cb_ref,
                wd_ref, fg_ref, y_ref, slabs, hbuf, uvbuf, acts, acc, *, nblk, tb, final):
    blk = pl.program_id(0) % nblk
    sub = FF_SUB
    nsub = D_FF // sub
    nring = uvbuf.shape[0]
    nslab = D_MODEL // LANES
    seg = tb // SUBLANES
    pitch = seg + SUBLANES

    g = g_ref[...]
    hn = _rmsnorm(x_ref[...] + d_ref[...], g)
    for k in range(nslab):
        for s in range(SUBLANES):
            slabs[k, pitch * s:pitch * s + seg, :] = hn[seg * s:seg * (s + 1), LANES * k:LANES * (k + 1)]

    def perm_rows(j):
        return jnp.concatenate(
            [slabs[k, pl.ds(j, SUBLANES, stride=pitch), :] for k in range(nslab)], axis=1)

    for jj in range(seg // 2):
        hbuf[BF16_ROWS * jj:BF16_ROWS * (jj + 1), :] = jnp.concatenate(
            [perm_rows(2 * jj), perm_rows(2 * jj + 1)], axis=0).astype(BF16)
    row_x = lax.broadcasted_iota(jnp.int32, (SUBLANES, D_MODEL), 0)
    h_prev = jnp.where(blk == 0, 0.0, pltpu.roll(_rmsnorm(prev_ref[...] + dprev_ref[...], g), 1, 0))
    h_next = jnp.where(blk == nblk - 1, 0.0,
                       pltpu.roll(_rmsnorm(next_ref[...] + dnext_ref[...], g), 1, 0))
    halo = jnp.where(row_x == 0, h_prev, jnp.where(row_x == 1, h_next, 0.0))
    hbuf[tb:, :] = jnp.concatenate([halo, jnp.zeros_like(halo)], axis=0).astype(BF16)

    row_u = lax.broadcasted_iota(jnp.int32, (SUBLANES, 2 * sub), 0)

    def pair(ref, sc):
        return jnp.concatenate([ref[:, sub * sc:sub * (sc + 1)],
                                ref[:, D_FF + sub * sc:D_FF + sub * (sc + 1)]], axis=1)

    def up(sc):
        slot = sc % nring
        res = jnp.concatenate([_dot(hbuf[...], wu_ref[:, sub * sc:sub * (sc + 1)]),
                               _dot(hbuf[...], wu_ref[:, D_FF + sub * sc:D_FF + sub * (sc + 1)])], axis=1)
        uvbuf[slot, SUBLANES:SUBLANES + tb, :] = res[0:tb]
        uvbuf[slot, 0:SUBLANES, :] = jnp.where(
            row_u == 0, jnp.broadcast_to(res[tb:tb + 1], row_u.shape),
            pltpu.roll(res[tb - SUBLANES:tb], 1, 0))
        uvbuf[slot, SUBLANES + tb:, :] = jnp.where(
            row_u == SUBLANES - 1, jnp.broadcast_to(res[tb + 1:tb + 2], row_u.shape),
            pltpu.roll(res[0:SUBLANES], SUBLANES - 1, 0))

    def gate(sc):
        slot = sc % nring
        cw = pair(cw_ref, sc)
        c = pair(cb_ref, sc) + uvbuf[slot, 0:tb, :] * cw[0:1]
        c = c + uvbuf[slot, SUBLANES:SUBLANES + tb, :] * cw[1:2]
        c = c + uvbuf[slot, 2 * SUBLANES:2 * SUBLANES + tb, :] * cw[2:3]
        return _gelu_gate(c[:, :sub], c[:, sub:]).astype(BF16)

    up(0)
    up(1)
    for sc in range(nsub):
        if sc + 2 < nsub:
            up(sc + 2)
        acts[:, sub * sc:sub * (sc + 1)] = gate(sc)

    acc[...] = _dot(acts[...], wd_ref[...])

    for j in range(seg):
        for k in range(nslab):
            slabs[k, pl.ds(j, SUBLANES, stride=pitch), :] = acc[SUBLANES * j:SUBLANES * (j + 1),
                                                                LANES * k:LANES * (k + 1)]
    ffn = jnp.concatenate(
        [jnp.concatenate([slabs[k, pitch * s:pitch * s + seg, :] for s in range(SUBLANES)], axis=0)
         for k in range(nslab)], axis=1)
    y = (x_ref[...] + d_ref[...]) + ffn
    if final:
        y = _rmsnorm(y, fg_ref[...])
    y_ref[...] = y


def _ffn(x2, d2, g, w_up, cw, cb, w_down, fg, *, S, final):
    T = x2.shape[0]
    tb = TB_FFN
    nblk = S // tb
    hpb = tb // SUBLANES
    n_halo = T // SUBLANES
    full = lambda shape: pl.BlockSpec(shape, lambda i: (0,) * len(shape))
    tile = pl.BlockSpec((tb, D_MODEL), lambda i: (i, 0))
    halo_prev = pl.BlockSpec((SUBLANES, D_MODEL), lambda i: (jnp.maximum(i * hpb - 1, 0), 0))
    halo_next = pl.BlockSpec((SUBLANES, D_MODEL), lambda i: (jnp.minimum((i + 1) * hpb, n_halo - 1), 0))
    kern = functools.partial(_ffn_kernel, nblk=nblk, tb=tb, final=final)
    return pl.pallas_call(
        kern,
        grid=(T // tb,),
        in_specs=[
            tile, halo_prev, halo_next, tile, halo_prev, halo_next,
            full((1, D_MODEL)), full((D_MODEL, 2 * D_FF)), full((3, 2 * D_FF)), full((1, 2 * D_FF)),
            full((D_FF, D_MODEL)), full((1, D_MODEL)),
        ],
        out_specs=pl.BlockSpec((tb, D_MODEL), lambda i: (i, 0)),
        out_shape=jax.ShapeDtypeStruct((T, D_MODEL), F32),
        scratch_shapes=[
            pltpu.VMEM((D_MODEL // LANES, tb + SUBLANES * SUBLANES, LANES), F32),
            pltpu.VMEM((tb + BF16_ROWS, D_MODEL), BF16),
            pltpu.VMEM((FF_RING, tb + 2 * SUBLANES, 2 * FF_SUB), F32),
            pltpu.VMEM((tb, D_FF), BF16),
            pltpu.VMEM((tb, D_MODEL), F32),
        ],
        compiler_params=pltpu.CompilerParams(dimension_semantics=("parallel",),
                                             vmem_limit_bytes=VMEM_LIMIT_FFN),
        name="convffn",
    )(x2, x2, x2, d2, d2, d2, g, w_up, cw, cb, w_down, fg)


def _block_diag(w):
    eye = jnp.eye(RG_BLOCKS, dtype=w.dtype)
    return jnp.einsum('ncd,nm->ncmd', w, eye).reshape(D_RG, D_RG)


def _encoder(x, norm1_g, w_in, b_gates, rg_conv_w, rg_conv_b, rg_wa, rg_ba, rg_wx, rg_bx, rg_lambda,
             ml_norm_g, w_out, norm2_g, w_up, ffn_conv_w, ffn_conv_b, w_down, final_g):
    B, S, _ = x.shape
    T = B * S
    depth = w_in.shape[0]
    x2 = x.reshape(T, D_MODEL)
    row = lambda v: v.reshape(1, -1).astype(F32)
    n_nat = 2 * D_RG + 2 * D_ML
    for l in range(depth):
        w_nat = w_in[l, :, :n_nat].astype(BF16)
        w_tr = w_in[l, :, n_nat:n_nat + 2 * D_ML].T.astype(BF16)
        w_gate = w_in[l, :, n_nat + 2 * D_ML:].T.astype(BF16)
        bias = jnp.broadcast_to(b_gates[l].astype(F32).reshape(N_GATE, 1), (N_GATE, ML_CHUNK))
        rx, rgate, qk, v_t, o_t, gr, uc = _inproj(x2, row(norm1_g[l]), w_nat, w_tr, w_gate, bias,
                                                  B=B, S=S)
        qk3 = qk.reshape(B, S, 2 * D_ML)
        gr4 = gr.reshape(B, S // ML_CHUNK, 2 * G_ROWS, ML_CHUNK)
        uc3 = uc.reshape(B, S, LANES)
        rg_gates = [(_block_diag(rg_wa[l, d]).astype(BF16), _block_diag(rg_wx[l, d]).astype(BF16),
                     row(rg_ba[l, d]), row(rg_bx[l, d]), row(rg_lambda[l, d])) for d in range(2)]
        r_dir = _rglru(rx, rg_conv_w[l].astype(F32), row(rg_conv_b[l]), *rg_gates, B=B, S=S)
        m_dir = [_mlstm(qk3, v_t, gr4, uc3, reverse=reverse) for reverse in (False, True)]
        wo = w_out[l].astype(BF16)
        mg_tile = jnp.broadcast_to(ml_norm_g[l].astype(F32).reshape(D_ML, 1), (D_ML, LANES))
        mixed = _outproj(r_dir[0], r_dir[1], rgate, m_dir[0], m_dir[1], o_t, mg_tile,
                         wo[:D_RG], wo[D_RG:], S=S)
        x2 = _ffn(x2, mixed, row(norm2_g[l]), w_up[l].astype(BF16),
                  ffn_conv_w[l].astype(F32), row(ffn_conv_b[l]),
                  w_down[l].astype(BF16), row(final_g), S=S,
                  final=(l == depth - 1))
    return x2.reshape(B, S, D_MODEL)


def kernel(x_prompt, x_sample, norm1_g, w_in, b_gates, rg_conv_w, rg_conv_b, rg_wa, rg_ba, rg_wx, rg_bx,
           rg_lambda, ml_norm_g, w_out, norm2_g, w_up, ffn_conv_w, ffn_conv_b, w_down, final_g):
    weights = (norm1_g, w_in, b_gates, rg_conv_w, rg_conv_b, rg_wa, rg_ba, rg_wx, rg_bx, rg_lambda,
               ml_norm_g, w_out, norm2_g, w_up, ffn_conv_w, ffn_conv_b, w_down, final_g)
    return (_encoder(x_prompt, *weights), _encoder(x_sample, *weights))
```

```python
import functools

import jax
import jax.numpy as jnp
from jax import lax
from jax.experimental import pallas as pl
from jax.experimental.pallas import tpu as pltpu

F32 = jnp.float32
BF16 = jnp.bfloat16

D_MODEL = 1024
D_RG = 512
D_ML = 512
RG_BLOCKS = 8
RG_C = 8.0
ML_HEADS = 4
ML_HD = 128
ML_CHUNK = 128
D_FF = 3072
EPS = 1e-6
N_GATE = 4 * ML_HEADS

SUBLANES = 8
LANES = 128
BF16_ROWS = 16
VMEM_LIMIT = 48 * 1024 * 1024

TM_PROJ = 1024
TM_SUB = 512
TB_RG = 2048
ML_SEQS = 16
TB_FFN = 512
FF_SUB = 256
FF_RING = 4
VMEM_LIMIT_FFN = 56 * 1024 * 1024

G_B, G_U, G_A, G_BL, G_GM = 0, 8, 16, 24, 32
G_ROWS = 40
UC_DIR = 24
UC_ONES = 2 * UC_DIR


def _params(*sem):
    return pltpu.CompilerParams(dimension_semantics=sem, vmem_limit_bytes=VMEM_LIMIT)


def _softplus(z):
    return jnp.maximum(z, 0.0) + jnp.log1p(jnp.exp(-jnp.abs(z)))


def _gelu_gate(gate, val):
    k0 = -2.0 * 0.7978845608028654 * 1.4426950408889634
    z = gate * (k0 + (k0 * 0.044715) * (gate * gate))
    return (gate * val) / (1.0 + jnp.exp2(z))


def _rmsnorm(x, g):
    return x * lax.rsqrt(jnp.mean(x * x, axis=-1, keepdims=True) + EPS) * g


def _dot(a, b):
    return jnp.dot(a, b, preferred_element_type=F32)


def _dot_nt(a, b):
    return lax.dot_general(a, b, (((1,), (1,)), ((), ())), preferred_element_type=F32)


def _split3(x):
    hi = x.astype(BF16).astype(F32)
    r1 = x - hi
    mid = r1.astype(BF16).astype(F32)
    return hi, mid, (r1 - mid).astype(BF16).astype(F32)


def _lane_scan(x, op, fill, reverse):
    n = x.shape[-1]
    lane = lax.broadcasted_iota(jnp.int32, x.shape, 1)
    s = 1
    while s < n:
        if reverse:
            x = op(x, jnp.where(lane < n - s, pltpu.roll(x, n - s, 1), fill))
        else:
            x = op(x, jnp.where(lane >= s, pltpu.roll(x, s, 1), fill))
        s *= 2
    return x


def _inproj_kernel(x_ref, g_ref, wn_ref, wt_ref, wg_ref, bias_ref, rx_ref, rg_ref, qk_ref, vt_ref,
                   ot_ref, gr_ref, uc_ref):
    for r0 in range(0, x_ref.shape[0], TM_SUB):
        _inproj_piece(x_ref, g_ref, wn_ref, wt_ref, wg_ref, bias_ref, rx_ref, rg_ref, qk_ref, vt_ref,
                      ot_ref, gr_ref, uc_ref, r0)


def _inproj_piece(x_ref, g_ref, wn_ref, wt_ref, wg_ref, bias_ref, rx_ref, rg_ref, qk_ref, vt_ref,
                  ot_ref, gr_ref, uc_ref, r0):
    L = ML_CHUNK
    NH = ML_HEADS
    rows = slice(r0, r0 + TM_SUB)
    h = _rmsnorm(x_ref[rows, :], g_ref[...]).astype(BF16)

    gt = _dot_nt(wg_ref[...], h)
    rowid = lax.broadcasted_iota(jnp.int32, (2 * NH, L), 0)
    head_row = rowid < NH
    rep = lambda col: jnp.broadcast_to(col, (2 * NH, L))
    zeros8 = jnp.zeros((2 * NH, L), F32)
    for cl in range(TM_SUB // L):
        c = r0 // L + cl
        g16 = gt[:, cl * L:(cl + 1) * L] + bias_ref[...]
        tiles = []
        for d, reverse in enumerate((False, True)):
            gates = g16[2 * NH * d:2 * NH * (d + 1)]
            lf = jnp.where(head_row, 0.0, -_softplus(-gates))
            bcum = pltpu.roll(_lane_scan(lf, jnp.add, 0.0, reverse), NH, 0)
            u = jnp.where(head_row, gates - bcum, 0.0)
            last = 0 if reverse else L - 1
            bl = rep(bcum[:, last:last + 1])
            base = G_ROWS * d
            gr_ref[c, base + G_B:base + G_B + 8, :] = bcum
            gr_ref[c, base + G_U:base + G_U + 8, :] = u
            gr_ref[c, base + G_A:base + G_A + 8, :] = bcum + _lane_scan(u, jnp.maximum, -jnp.inf, reverse)
            gr_ref[c, base + G_BL:base + G_BL + 8, :] = bl
            gr_ref[c, base + G_GM:base + G_GM + 8, :] = rep(jnp.max(bl + u, axis=-1, keepdims=True))
            tiles.extend(_split3(u))
        tiles.append(jnp.ones((2 * NH, L), F32))
        tiles.extend([zeros8] * (L // 8 - len(tiles)))
        uc_ref[c * L:(c + 1) * L, :] = jnp.concatenate(tiles, axis=0).T.astype(BF16)

    nat = _dot(h, wn_ref[...])
    rx_ref[rows, :] = nat[:, :D_RG]
    rg_ref[rows, :] = nat[:, D_RG:2 * D_RG].astype(BF16)
    qk_ref[rows, :] = jnp.concatenate(
        [nat[:, 2 * D_RG:2 * D_RG + D_ML] * (ML_HD ** -0.5), nat[:, 2 * D_RG + D_ML:]], axis=-1).astype(BF16)
    tr = _dot_nt(wt_ref[...], h)
    vt_ref[:, rows] = tr[:D_ML].astype(BF16)
    ot_ref[:, rows] = tr[D_ML:].astype(BF16)


def _inproj(x2, g, w_nat, w_tr, w_gate, bias, *, B, S):
    T = B * S
    tm = TM_PROJ
    nb = S // tm
    full = lambda shape: pl.BlockSpec(shape, lambda i: (0,) * len(shape))
    tok = lambda width: pl.BlockSpec((tm, width), lambda i: (i, 0))
    seq_t = pl.BlockSpec((None, D_ML, tm), lambda i: (i // nb, 0, i % nb))
    return pl.pallas_call(
        _inproj_kernel,
        grid=(T // tm,),
        in_specs=[
            tok(D_MODEL), full((1, D_MODEL)), full(w_nat.shape), full(w_tr.shape), full(w_gate.shape),
            full((N_GATE, ML_CHUNK)),
        ],
        out_specs=[
            tok(D_RG), tok(D_RG), tok(2 * D_ML), seq_t, seq_t,
            pl.BlockSpec((tm // ML_CHUNK, 2 * G_ROWS, ML_CHUNK), lambda i: (i, 0, 0)),
            tok(LANES),
        ],
        out_shape=[
            jax.ShapeDtypeStruct((T, D_RG), F32),
            jax.ShapeDtypeStruct((T, D_RG), BF16),
            jax.ShapeDtypeStruct((T, 2 * D_ML), BF16),
            jax.ShapeDtypeStruct((B, D_ML, S), BF16),
            jax.ShapeDtypeStruct((B, D_ML, S), BF16),
            jax.ShapeDtypeStruct((T // ML_CHUNK, 2 * G_ROWS, ML_CHUNK), F32),
            jax.ShapeDtypeStruct((T, LANES), BF16),
        ],
        compiler_params=_params("parallel"),
        name="inproj",
    )(x2, g, w_nat, w_tr, w_gate, bias)


def _rglru_fwd_kernel(x_ref, prev_ref, next_ref, cw_ref, cb_ref, wa_ref, wx_ref, ba_ref, bx_ref,
                      lam_ref, h_ref, xc_ref, slabs, xbuf, pbuf, lbuf, carry, *, nblk, tb):
    blk = pl.program_id(1)

    @pl.when(blk == 0)
    def _():
        carry[...] = jnp.zeros_like(carry)

    npiece = x_ref.shape[0] // tb
    for p in range(npiece):
        rows = pl.ds(tb * p, tb)
        prev = (jnp.where(blk == 0, 0.0, prev_ref[...]) if p == 0
                else x_ref[tb * p - SUBLANES:tb * p, :])
        nxt = (jnp.where(blk == nblk - 1, 0.0, next_ref[...]) if p == npiece - 1
               else x_ref[tb * (p + 1):tb * (p + 1) + SUBLANES, :])
        xc = _rglru_conv(x_ref[rows, :], prev, nxt, cw_ref, cb_ref, slabs.at[p], xbuf.at[p], tb=tb)
        xc_ref[rows, :] = xc
        _rglru_scan(xc, wa_ref, wx_ref, ba_ref, bx_ref, lam_ref, h_ref.at[rows], slabs.at[p],
                    pbuf.at[p], lbuf.at[p], carry, reverse=False, tb=tb)


def _rglru_bwd_kernel(xc_ref, wa_ref, wx_ref, ba_ref, bx_ref, lam_ref, h_ref, slabs, pbuf, lbuf,
                      carry, *, tb):
    @pl.when(pl.program_id(1) == 0)
    def _():
        carry[...] = jnp.zeros_like(carry)

    for p in reversed(range(xc_ref.shape[0] // tb)):
        rows = pl.ds(tb * p, tb)
        _rglru_scan(xc_ref[rows, :], wa_ref, wx_ref, ba_ref, bx_ref, lam_ref, h_ref.at[rows],
                    slabs.at[p], pbuf.at[p], lbuf.at[p], carry, reverse=True, tb=tb)


def _rglru_conv(x, prev, nxt, cw_ref, cb_ref, slabs, xbuf, *, tb):
    nslab = D_RG // LANES
    seg = tb // SUBLANES
    pitch = seg + SUBLANES
    X0 = 2 * SUBLANES

    for k in range(nslab):
        for s in range(SUBLANES):
            slabs[k, pitch * s:pitch * s + seg, :] = x[seg * s:seg * (s + 1), LANES * k:LANES * (k + 1)]
    seam = {}
    for j in range(seg):
        rows = jnp.concatenate(
            [slabs[k, pl.ds(j, SUBLANES, stride=pitch), :] for k in range(nslab)], axis=1)
        xbuf[X0 + SUBLANES * j:X0 + SUBLANES * (j + 1), :] = rows
        if j in (0, seg - 2, seg - 1):
            seam[j] = rows
    row = lax.broadcasted_iota(jnp.int32, (SUBLANES, D_RG), 0)
    tile_row = lambda v, i: jnp.broadcast_to(v[i:i + 1, :], (SUBLANES, D_RG))
    xbuf[0:SUBLANES, :] = jnp.where(row == 0, tile_row(prev, SUBLANES - 2), pltpu.roll(seam[seg - 2], 1, 0))
    xbuf[SUBLANES:X0, :] = jnp.where(row == 0, tile_row(prev, SUBLANES - 1), pltpu.roll(seam[seg - 1], 1, 0))
    xbuf[X0 + tb:, :] = jnp.where(row == SUBLANES - 1, tile_row(nxt, 0),
                                  pltpu.roll(seam[0], SUBLANES - 1, 0))
    xc = cb_ref[...] + xbuf[0:tb, :] * cw_ref[0:1, :]
    xc = xc + xbuf[SUBLANES:SUBLANES + tb, :] * cw_ref[1:2, :]
    xc = xc + xbuf[X0:X0 + tb, :] * cw_ref[2:3, :]
    return xc + xbuf[X0 + SUBLANES:X0 + SUBLANES + tb, :] * cw_ref[3:4, :]


def _rglru_scan(xc, wa_ref, wx_ref, ba_ref, bx_ref, lam_ref, h_ref, slabs, pbuf, lbuf, carry, *,
                reverse, tb):
    nslab = D_RG // LANES
    seg = tb // SUBLANES
    pitch = seg + SUBLANES
    row = lax.broadcasted_iota(jnp.int32, (SUBLANES, D_RG), 0)
    tile_row = lambda v, i: jnp.broadcast_to(v[i:i + 1, :], (SUBLANES, D_RG))

    xcb = xc.astype(BF16)
    r = jax.nn.sigmoid(_dot(xcb, wa_ref[...]) + ba_ref[...])
    i = jax.nn.sigmoid(_dot(xcb, wx_ref[...]) + bx_ref[...])
    decay_rate = RG_C * _softplus(-lam_ref[...])
    a = jnp.exp2((decay_rate * -1.4426950408889634) * r)
    y = jnp.tanh(decay_rate * r) * (a * a + 1.0)
    u = jnp.where(y > 0.0, y * lax.rsqrt(y), 0.0) * (i * xc)

    order = range(seg - 1, -1, -1) if reverse else range(seg)
    P = L = None
    for j in order:
        rs = slice(SUBLANES * j, SUBLANES * (j + 1))
        if P is None:
            P, L = a[rs], u[rs]
        else:
            P, L = a[rs] * P, a[rs] * L + u[rs]
        pbuf[rs, :] = P
        lbuf[rs, :] = L

    A, U = P, L
    for s in (1, 2, 4):
        if reverse:
            keep = row < SUBLANES - s
            shift = SUBLANES - s
        else:
            keep = row >= s
            shift = s
        a_sh = jnp.where(keep, pltpu.roll(A, shift, 0), 1.0)
        u_sh = jnp.where(keep, pltpu.roll(U, shift, 0), 0.0)
        U = A * u_sh + U
        A = A * a_sh
    c_in = carry[...]
    e = U + A * c_in
    if reverse:
        c_seg = jnp.where(row == SUBLANES - 1, c_in, pltpu.roll(e, SUBLANES - 1, 0))
        carry[...] = tile_row(e, 0)
    else:
        c_seg = jnp.where(row == 0, c_in, pltpu.roll(e, 1, 0))
        carry[...] = tile_row(e, SUBLANES - 1)

    for j in range(seg):
        rs = slice(SUBLANES * j, SUBLANES * (j + 1))
        hj = lbuf[rs, :] + pbuf[rs, :] * c_seg
        for k in range(nslab):
            slabs[k, pl.ds(j, SUBLANES, stride=pitch), :] = hj[:, LANES * k:LANES * (k + 1)]
    for k in range(nslab):
        for s in range(SUBLANES):
            h_ref[seg * s:seg * (s + 1), LANES * k:LANES * (k + 1)] = (
                slabs[k, pitch * s:pitch * s + seg, :].astype(h_ref.dtype))


def _rglru(x, cw, cb, gates_fwd, gates_bwd, *, B, S):
    T = B * S
    tile = TB_RG
    tb = TM_SUB
    npiece = tile // tb
    nblk = S // tile
    hb = tile // SUBLANES
    n_halo = T // SUBLANES
    full = lambda shape: pl.BlockSpec(shape, lambda b, j: (0,) * len(shape))
    gate_specs = [full((D_RG, D_RG)), full((D_RG, D_RG)), full((1, D_RG)), full((1, D_RG)), full((1, D_RG))]
    slabs = pltpu.VMEM((npiece, D_RG // LANES, tb + SUBLANES * SUBLANES, LANES), F32)
    scan_scratch = [
        pltpu.VMEM((npiece, tb, D_RG), F32),
        pltpu.VMEM((npiece, tb, D_RG), F32),
        pltpu.VMEM((SUBLANES, D_RG), F32),
    ]
    tile_fwd = pl.BlockSpec((tile, D_RG), lambda b, j: (b * nblk + j, 0))
    tile_bwd = pl.BlockSpec((tile, D_RG), lambda b, j: (b * nblk + nblk - 1 - j, 0))

    h_fwd, xc = pl.pallas_call(
        functools.partial(_rglru_fwd_kernel, nblk=nblk, tb=tb),
        grid=(B, nblk),
        in_specs=[
            tile_fwd,
            pl.BlockSpec((SUBLANES, D_RG), lambda b, j: (jnp.maximum((b * nblk + j) * hb - 1, 0), 0)),
            pl.BlockSpec((SUBLANES, D_RG), lambda b, j: (jnp.minimum((b * nblk + j + 1) * hb, n_halo - 1), 0)),
            full((4, D_RG)), full((1, D_RG)), *gate_specs,
        ],
        out_specs=[tile_fwd, tile_fwd],
        out_shape=[jax.ShapeDtypeStruct((T, D_RG), BF16),
                   jax.ShapeDtypeStruct((T, D_RG), F32)],
        scratch_shapes=[slabs, pltpu.VMEM((npiece, tb + 3 * SUBLANES, D_RG), F32), *scan_scratch],
        compiler_params=_params("parallel", "arbitrary"),
        name="rglru_fwd",
    )(x, x, x, cw, cb, *gates_fwd)
    h_bwd = pl.pallas_call(
        functools.partial(_rglru_bwd_kernel, tb=tb),
        grid=(B, nblk),
        in_specs=[tile_bwd, *gate_specs],
        out_specs=tile_bwd,
        out_shape=jax.ShapeDtypeStruct((T, D_RG), BF16),
        scratch_shapes=[slabs, *scan_scratch],
        compiler_params=_params("parallel", "arbitrary"),
        name="rglru_bwd",
    )(xc, *gates_bwd)
    return h_fwd, h_bwd


def _mlstm_kernel(q_ref, k_ref, vt_ref, gr_ref, uc_ref, h_ref, ct_st, n_st, m_st, *, reverse, bb):
    L = ML_CHUNK
    NH = ML_HEADS
    d = 1 if reverse else 0

    @pl.when(pl.program_id(1) == 0)
    def _():
        ct_st[...] = jnp.zeros_like(ct_st)
        n_st[...] = jnp.zeros_like(n_st)
        m_st[...] = jnp.zeros_like(m_st)

    row8 = lax.broadcasted_iota(jnp.int32, (SUBLANES, L), 0)
    s_id = lax.broadcasted_iota(jnp.int32, (L, L), 0)
    t_id = lax.broadcasted_iota(jnp.int32, (L, L), 1)
    valid = (s_id >= t_id) if reverse else (s_id <= t_id)
    zeros8 = jnp.zeros((SUBLANES, L), F32)
    tile = lambda rows, hd: jnp.broadcast_to(rows[hd:hd + 1, :], (L, L))
    row_of = lambda rows, hd: jnp.broadcast_to(rows[hd:hd + 1, :], (SUBLANES, L))
    pairs = [(b, hp) for b in range(bb) for hp in range(NH // 2)]

    seq = []
    for b in range(bb):
        base = G_ROWS * d
        bcum = gr_ref[b, base + G_B:base + G_B + 8, :]
        u = gr_ref[b, base + G_U:base + G_U + 8, :]
        b_last = gr_ref[b, base + G_BL:base + G_BL + 8, :]
        m_prev = m_st[b]
        inter = bcum + m_prev
        m_t = jnp.maximum(inter, gr_ref[b, base + G_A:base + G_A + 8, :])
        m_new = jnp.maximum(b_last + m_prev, gr_ref[b, base + G_GM:base + G_GM + 8, :])
        n_prev = n_st[b]
        wg = jnp.exp(u + (b_last - m_new))
        seq.append(dict(
            w_int=jnp.exp(inter - m_t), e_neg=jnp.exp(-m_t), v3=_split3(bcum - m_t),
            decay=jnp.exp(b_last + m_prev - m_new), m_new=m_new, n_prev=n_prev, wg=wg,
            n_lhs=jnp.concatenate([n_prev, zeros8], axis=0).astype(BF16),
            wg_lhs=jnp.concatenate([wg, zeros8], axis=0).astype(BF16)))

    cs = lambda hd: slice(hd * ML_HD, (hd + 1) * ML_HD)
    side = lambda i: slice(i * L, (i + 1) * L)
    twice = lambda x: jnp.concatenate([x, x], axis=1)

    def blockdiag(a0, a1):
        z = jnp.zeros_like(a0)
        return jnp.concatenate([jnp.concatenate([a0, z], axis=1), jnp.concatenate([z, a1], axis=1)], axis=0)

    def expo_rhs(b, hd):
        onehot = jnp.where(row8 == hd, 1.0, 0.0)
        v_hi, v_mid, v_lo = (row_of(x, hd) for x in seq[b]["v3"])
        v_rows = jnp.where(row8 == 0, v_hi, jnp.where(row8 == 1, v_mid, jnp.where(row8 == 2, v_lo, 0.0)))
        slabs = [zeros8] * (L // SUBLANES)
        for j in range(3):
            slabs[(UC_DIR * d) // SUBLANES + j] = onehot
        slabs[UC_ONES // SUBLANES] = v_rows
        return jnp.concatenate(slabs, axis=0).astype(BF16)

    st, expo, inter, upd = {}, {}, {}, {}
    for b, hp in pairs:
        sq = seq[b]
        heads = (2 * hp, 2 * hp + 1)
        q_diag = blockdiag(q_ref[b, :, cs(heads[0])], q_ref[b, :, cs(heads[1])])
        sq_all = _dot_nt(
            jnp.concatenate([k_ref[b, :, heads[0] * ML_HD:(heads[1] + 1) * ML_HD], twice(sq["n_lhs"]),
                             jnp.concatenate([ct_st[b, hd].astype(BF16) for hd in heads], axis=1)],
                            axis=0), q_diag)
        st[b, hp] = sq_all[0:L + BF16_ROWS]
        inter[b, hp] = sq_all[L + BF16_ROWS:]
        expo[b, hp] = _dot(uc_ref[b], jnp.concatenate([expo_rhs(b, hd) for hd in heads], axis=1))
        vw = jnp.concatenate([(vt_ref[b, cs(hd), :].astype(F32) * sq["wg"][hd:hd + 1, :]).astype(BF16)
                              for hd in heads], axis=1)
        upd[b, hp] = _dot(jnp.concatenate([vw, twice(sq["wg_lhs"])], axis=0),
                          blockdiag(k_ref[b, :, cs(heads[0])], k_ref[b, :, cs(heads[1])]))

    for b, hp in pairs:
        sq = seq[b]
        for i, hd in enumerate((2 * hp, 2 * hp + 1)):
            ct_st[b, hd] = tile(sq["decay"], hd) * ct_st[b, hd] + upd[b, hp][0:L, side(i)]
            n_st[b, hd:hd + 1, :] = (sq["decay"][hd:hd + 1, :] * sq["n_prev"][hd:hd + 1, :]
                                     + upd[b, hp][L + hd:L + hd + 1, side(i)])
    for b in range(bb):
        m_st[b] = seq[b]["m_new"]

    for b, hp in pairs:
        sq = seq[b]
        heads = (2 * hp, 2 * hp + 1)
        p_t, rw = [], []
        for i, hd in enumerate(heads):
            s_t = st[b, hp][0:L, side(i)] * jnp.exp(jnp.where(valid, expo[b, hp][:, side(i)], -jnp.inf))
            w_h = sq["w_int"][hd:hd + 1, :]
            den = jnp.sum(s_t, axis=0, keepdims=True) + w_h * st[b, hp][L + hd:L + hd + 1, side(i)]
            r = 1.0 / jnp.maximum(jnp.abs(den), sq["e_neg"][hd:hd + 1, :])
            p_t.append((s_t * r).astype(BF16))
            rw.append(r * w_h)
        intra = _dot(jnp.concatenate([vt_ref[b, cs(hd), :] for hd in heads], axis=1), blockdiag(*p_t))
        out = (intra + inter[b, hp] * jnp.concatenate(rw, axis=1)).astype(h_ref.dtype)
        for i, hd in enumerate(heads):
            h_ref[b, cs(hd), :] = out[:, side(i)]


def _mlstm(qk3, vt, gr4, uc3, *, reverse):
    B, S, _ = qk3.shape
    L = ML_CHUNK
    nc = S // L
    bb = ML_SEQS

    def chunk(c):
        return (nc - 1 - c) if reverse else c

    kern = functools.partial(_mlstm_kernel, reverse=reverse, bb=bb)
    return pl.pallas_call(
        kern,
        grid=(B // bb, nc),
        in_specs=[
            pl.BlockSpec((bb, L, D_ML), lambda b, c: (b, chunk(c), 0)),
            pl.BlockSpec((bb, L, D_ML), lambda b, c: (b, chunk(c), 1)),
            pl.BlockSpec((bb, D_ML, L), lambda b, c: (b, 0, chunk(c))),
            pl.BlockSpec((bb, None, 2 * G_ROWS, L), lambda b, c: (b, chunk(c), 0, 0)),
            pl.BlockSpec((bb, L, LANES), lambda b, c: (b, chunk(c), 0)),
        ],
        out_specs=pl.BlockSpec((bb, D_ML, L), lambda b, c: (b, 0, chunk(c))),
        out_shape=jax.ShapeDtypeStruct((B, D_ML, S), BF16),
        scratch_shapes=[
            pltpu.VMEM((bb, ML_HEADS, ML_HD, ML_HD), F32),
            pltpu.VMEM((bb, 2 * ML_HEADS, ML_HD), F32),
            pltpu.VMEM((bb, 2 * ML_HEADS, LANES), F32),
        ],
        compiler_params=_params("parallel", "arbitrary"),
        name="mlstm_bwd" if reverse else "mlstm_fwd",
    )(qk3, qk3, vt, gr4, uc3)


def _outproj_kernel(rf_ref, rb_ref, gate_ref, mf_ref, mb_ref, ot_ref, mg_ref, wr_ref, wm_ref, y_ref):
    for r0 in range(0, y_ref.shape[0], TM_SUB):
        rows = slice(r0, r0 + TM_SUB)
        y_rg = _gelu_gate(gate_ref[rows, :].astype(F32),
                          rf_ref[rows, :].astype(F32) + rb_ref[rows, :].astype(F32))
        acc = _dot(y_rg.astype(BF16), wr_ref[...])
        h_t = mf_ref[:, rows].astype(F32) + mb_ref[:, rows].astype(F32)
        parts = []
        for hd in range(ML_HEADS):
            hh = h_t[hd * ML_HD:(hd + 1) * ML_HD]
            parts.append(hh * lax.rsqrt(jnp.mean(hh * hh, axis=0, keepdims=True) + EPS))
        mg = jnp.tile(mg_ref[...], (1, TM_SUB // LANES))
        y_t = jax.nn.sigmoid(ot_ref[:, rows].astype(F32)) * (jnp.concatenate(parts, axis=0) * mg)
        y_ref[rows, :] = acc + _dot(y_t.T.astype(BF16), wm_ref[...])


def _outproj(rf, rb, pa, mf_t, mb_t, o_t, mg_tile, w_rg, w_ml, *, S):
    T = rf.shape[0]
    tm = TM_PROJ
    nb = S // tm
    tok = lambda width, col: pl.BlockSpec((tm, width), lambda i: (i, col))
    seq_t = pl.BlockSpec((None, D_ML, tm), lambda i: (i // nb, 0, i % nb))
    full = lambda shape: pl.BlockSpec(shape, lambda i: (0,) * len(shape))
    return pl.pallas_call(
        _outproj_kernel,
        grid=(T // tm,),
        in_specs=[
            tok(D_RG, 0), tok(D_RG, 0), tok(D_RG, 0),
            seq_t, seq_t, seq_t,
            full((D_ML, LANES)), full((D_RG, D_MODEL)), full((D_ML, D_MODEL)),
        ],
        out_specs=tok(D_MODEL, 0),
        out_shape=jax.ShapeDtypeStruct((T, D_MODEL), F32),
        compiler_params=_params("parallel"),
        name="outproj",
    )(rf, rb, pa, mf_t, mb_t, o_t, mg_tile, w_rg, w_ml)


def _ffn_kernel(x_ref, prev_ref, next_ref, d_ref, dprev_ref, dnext_ref, g_ref, wu_ref, cw_ref, cb_ref,
                wd_ref, fg_ref, y_ref, slabs, hbuf, uvbuf, acts, acc, *, nblk, tb, final):
    blk = pl.program_id(0) % nblk
    sub = FF_SUB
    nsub = D_FF // sub
    nring = uvbuf.shape[0]
    nslab = D_MODEL // LANES
    seg = tb // SUBLANES
    pitch = seg + SUBLANES

    g = g_ref[...]
    hn = _rmsnorm(x_ref[...] + d_ref[...], g)
    for k in range(nslab):
        for s in range(SUBLANES):
            slabs[k, pitch * s:pitch * s + seg, :] = hn[seg * s:seg * (s + 1), LANES * k:LANES * (k + 1)]

    def perm_rows(j):
        return jnp.concatenate(
            [slabs[k, pl.ds(j, SUBLANES, stride=pitch), :] for k in range(nslab)], axis=1)

    for jj in range(seg // 2):
        hbuf[BF16_ROWS * jj:BF16_ROWS * (jj + 1), :] = jnp.concatenate(
            [perm_rows(2 * jj), perm_rows(2 * jj + 1)], axis=0).astype(BF16)
    row_x = lax.broadcasted_iota(jnp.int32, (SUBLANES, D_MODEL), 0)
    h_prev = jnp.where(blk == 0, 0.0, pltpu.roll(_rmsnorm(prev_ref[...] + dprev_ref[...], g), 1, 0))
    h_next = jnp.where(blk == nblk - 1, 0.0,
                       pltpu.roll(_rmsnorm(next_ref[...] + dnext_ref[...], g), 1, 0))
    halo = jnp.where(row_x == 0, h_prev, jnp.where(row_x == 1, h_next, 0.0))
    hbuf[tb:, :] = jnp.concatenate([halo, jnp.zeros_like(halo)], axis=0).astype(BF16)

    row_u = lax.broadcasted_iota(jnp.int32, (SUBLANES, 2 * sub), 0)

    def pair(ref, sc):
        return jnp.concatenate([ref[:, sub * sc:sub * (sc + 1)],
                                ref[:, D_FF + sub * sc:D_FF + sub * (sc + 1)]], axis=1)

    def up(sc):
        slot = sc % nring
        res = jnp.concatenate([_dot(hbuf[...], wu_ref[:, sub * sc:sub * (sc + 1)]),
                               _dot(hbuf[...], wu_ref[:, D_FF + sub * sc:D_FF + sub * (sc + 1)])], axis=1)
        uvbuf[slot, SUBLANES:SUBLANES + tb, :] = res[0:tb]
        uvbuf[slot, 0:SUBLANES, :] = jnp.where(
            row_u == 0, jnp.broadcast_to(res[tb:tb + 1], row_u.shape),
            pltpu.roll(res[tb - SUBLANES:tb], 1, 0))
        uvbuf[slot, SUBLANES + tb:, :] = jnp.where(
            row_u == SUBLANES - 1, jnp.broadcast_to(res[tb + 1:tb + 2], row_u.shape),
            pltpu.roll(res[0:SUBLANES], SUBLANES - 1, 0))

    def gate(sc):
        slot = sc % nring
        cw = pair(cw_ref, sc)
        c = pair(cb_ref, sc) + uvbuf[slot, 0:tb, :] * cw[0:1]
        c = c + uvbuf[slot, SUBLANES:SUBLANES + tb, :] * cw[1:2]
        c = c + uvbuf[slot, 2 * SUBLANES:2 * SUBLANES + tb, :] * cw[2:3]
        return _gelu_gate(c[:, :sub], c[:, sub:]).astype(BF16)

    up(0)
    up(1)
    for sc in range(nsub):
        if sc + 2 < nsub:
            up(sc + 2)
        acts[:, sub * sc:sub * (sc + 1)] = gate(sc)

    acc[...] = _dot(acts[...], wd_ref[...])

    for j in range(seg):
        for k in range(nslab):
            slabs[k, pl.ds(j, SUBLANES, stride=pitch), :] = acc[SUBLANES * j:SUBLANES * (j + 1),
                                                                LANES * k:LANES * (k + 1)]
    ffn = jnp.concatenate(
        [jnp.concatenate([slabs[k, pitch * s:pitch * s + seg, :] for s in range(SUBLANES)], axis=0)
         for k in range(nslab)], axis=1)
    y = (x_ref[...] + d_ref[...]) + ffn
    if final:
        y = _rmsnorm(y, fg_ref[...])
    y_ref[...] = y


def _ffn(x2, d2, g, w_up, cw, cb, w_down, fg, *, S, final):
    T = x2.shape[0]
    tb = TB_FFN
    nblk = S // tb
    hpb = tb // SUBLANES
    n_halo = T // SUBLANES
    full = lambda shape: pl.BlockSpec(shape, lambda i: (0,) * len(shape))
    tile = pl.BlockSpec((tb, D_MODEL), lambda i: (i, 0))
    halo_prev = pl.BlockSpec((SUBLANES, D_MODEL), lambda i: (jnp.maximum(i * hpb - 1, 0), 0))
    halo_next = pl.BlockSpec((SUBLANES, D_MODEL), lambda i: (jnp.minimum((i + 1) * hpb, n_halo - 1), 0))
    kern = functools.partial(_ffn_kernel, nblk=nblk, tb=tb, final=final)
    return pl.pallas_call(
        kern,
        grid=(T // tb,),
        in_specs=[
            tile, halo_prev, halo_next, tile, halo_prev, halo_next,
            full((1, D_MODEL)), full((D_MODEL, 2 * D_FF)), full((3, 2 * D_FF)), full((1, 2 * D_FF)),
            full((D_FF, D_MODEL)), full((1, D_MODEL)),
        ],
        out_specs=pl.BlockSpec((tb, D_MODEL), lambda i: (i, 0)),
        out_shape=jax.ShapeDtypeStruct((T, D_MODEL), F32),
        scratch_shapes=[
            pltpu.VMEM((D_MODEL // LANES, tb + SUBLANES * SUBLANES, LANES), F32),
            pltpu.VMEM((tb + BF16_ROWS, D_MODEL), BF16),
            pltpu.VMEM((FF_RING, tb + 2 * SUBLANES, 2 * FF_SUB), F32),
            pltpu.VMEM((tb, D_FF), BF16),
            pltpu.VMEM((tb, D_MODEL), F32),
        ],
        compiler_params=pltpu.CompilerParams(dimension_semantics=("parallel",),
                                             vmem_limit_bytes=VMEM_LIMIT_FFN),
        name="convffn",
    )(x2, x2, x2, d2, d2, d2, g, w_up, cw, cb, w_down, fg)


def _block_diag(w):
    eye = jnp.eye(RG_BLOCKS, dtype=w.dtype)
    return jnp.einsum('ncd,nm->ncmd', w, eye).reshape(D_RG, D_RG)


def _encoder(x, norm1_g, w_in, b_gates, rg_conv_w, rg_conv_b, rg_wa, rg_ba, rg_wx, rg_bx, rg_lambda,
             ml_norm_g, w_out, norm2_g, w_up, ffn_conv_w, ffn_conv_b, w_down, final_g):
    B, S, _ = x.shape
    T = B * S
    depth = w_in.shape[0]
    x2 = x.reshape(T, D_MODEL)
    row = lambda v: v.reshape(1, -1).astype(F32)
    n_nat = 2 * D_RG + 2 * D_ML
    for l in range(depth):
        w_nat = w_in[l, :, :n_nat].astype(BF16)
        w_tr = w_in[l, :, n_nat:n_nat + 2 * D_ML].T.astype(BF16)
        w_gate = w_in[l, :, n_nat + 2 * D_ML:].T.astype(BF16)
        bias = jnp.broadcast_to(b_gates[l].astype(F32).reshape(N_GATE, 1), (N_GATE, ML_CHUNK))
        rx, rgate, qk, v_t, o_t, gr, uc = _inproj(x2, row(norm1_g[l]), w_nat, w_tr, w_gate, bias,
                                                  B=B, S=S)
        qk3 = qk.reshape(B, S, 2 * D_ML)
        gr4 = gr.reshape(B, S // ML_CHUNK, 2 * G_ROWS, ML_CHUNK)
        uc3 = uc.reshape(B, S, LANES)
        rg_gates = [(_block_diag(rg_wa[l, d]).astype(BF16), _block_diag(rg_wx[l, d]).astype(BF16),
                     row(rg_ba[l, d]), row(rg_bx[l, d]), row(rg_lambda[l, d])) for d in range(2)]
        r_dir = _rglru(rx, rg_conv_w[l].astype(F32), row(rg_conv_b[l]), *rg_gates, B=B, S=S)
        m_dir = [_mlstm(qk3, v_t, gr4, uc3, reverse=reverse) for reverse in (False, True)]
        wo = w_out[l].astype(BF16)
        mg_tile = jnp.broadcast_to(ml_norm_g[l].astype(F32).reshape(D_ML, 1), (D_ML, LANES))
        mixed = _outproj(r_dir[0], r_dir[1], rgate, m_dir[0], m_dir[1], o_t, mg_tile,
                         wo[:D_RG], wo[D_RG:], S=S)
        x2 = _ffn(x2, mixed, row(norm2_g[l]), w_up[l].astype(BF16),
                  ffn_conv_w[l].astype(F32), row(ffn_conv_b[l]),
                  w_down[l].astype(BF16), row(final_g), S=S,
                  final=(l == depth - 1))
    return x2.reshape(B, S, D_MODEL)


def kernel(x_prompt, x_sample, norm1_g, w_in, b_gates, rg_conv_w, rg_conv_b, rg_wa, rg_ba, rg_wx, rg_bx,
           rg_lambda, ml_norm_g, w_out, norm2_g, w_up, ffn_conv_w, ffn_conv_b, w_down, final_g):
    weights = (norm1_g, w_in, b_gates, rg_conv_w, rg_conv_b, rg_wa, rg_ba, rg_wx, rg_bx, rg_lambda,
               ml_norm_g, w_out, norm2_g, w_up, ffn_conv_w, ffn_conv_b, w_down, final_g)
    return (_encoder(x_prompt, *weights), _encoder(x_sample, *weights))
```

```python
import functools

import jax
import jax.numpy as jnp
from jax import lax
from jax.experimental import pallas as pl
from jax.experimental.pallas import tpu as pltpu

F32 = jnp.float32
BF16 = jnp.bfloat16

D_MODEL = 1024
D_RG = 512
D_ML = 512
RG_BLOCKS = 8
RG_C = 8.0
ML_HEADS = 4
ML_HD = 128
ML_CHUNK = 128
D_FF = 3072
EPS = 1e-6
N_GATE = 4 * ML_HEADS

SUBLANES = 8
LANES = 128
BF16_ROWS = 16
VMEM_LIMIT = 48 * 1024 * 1024

TM_PROJ = 1024
TM_SUB = 512
TB_RG = 2048
ML_SEQS = 16
TB_FFN = 512
FF_SUB = 256
FF_ROWS = 128
FF_RING = 4
VMEM_LIMIT_FFN = 56 * 1024 * 1024

G_B, G_U, G_A, G_BL, G_GM = 0, 8, 16, 24, 32
G_ROWS = 40
UC_DIR = 24
UC_ONES = 2 * UC_DIR


def _params(*sem):
    return pltpu.CompilerParams(dimension_semantics=sem, vmem_limit_bytes=VMEM_LIMIT)


def _softplus(z):
    return jnp.maximum(z, 0.0) + jnp.log1p(jnp.exp(-jnp.abs(z)))


def _gelu_gate(gate, val):
    k0 = -2.0 * 0.7978845608028654 * 1.4426950408889634
    z = gate * (k0 + (k0 * 0.044715) * (gate * gate))
    return (gate * val) / (1.0 + jnp.exp2(z))


def _rmsnorm(x, g):
    return x * lax.rsqrt(jnp.mean(x * x, axis=-1, keepdims=True) + EPS) * g


def _dot(a, b):
    return jnp.dot(a, b, preferred_element_type=F32)


def _dot_nt(a, b):
    return lax.dot_general(a, b, (((1,), (1,)), ((), ())), preferred_element_type=F32)


def _split3(x):
    hi = x.astype(BF16).astype(F32)
    r1 = x - hi
    mid = r1.astype(BF16).astype(F32)
    return hi, mid, (r1 - mid).astype(BF16).astype(F32)


def _lane_scan(x, op, fill, reverse):
    n = x.shape[-1]
    lane = lax.broadcasted_iota(jnp.int32, x.shape, 1)
    s = 1
    while s < n:
        if reverse:
            x = op(x, jnp.where(lane < n - s, pltpu.roll(x, n - s, 1), fill))
        else:
            x = op(x, jnp.where(lane >= s, pltpu.roll(x, s, 1), fill))
        s *= 2
    return x


def _inproj_kernel(x_ref, g_ref, wn_ref, wt_ref, wg_ref, bias_ref, rx_ref, rg_ref, qk_ref, vt_ref,
                   ot_ref, gr_ref, uc_ref):
    for r0 in range(0, x_ref.shape[0], TM_SUB):
        _inproj_piece(x_ref, g_ref, wn_ref, wt_ref, wg_ref, bias_ref, rx_ref, rg_ref, qk_ref, vt_ref,
                      ot_ref, gr_ref, uc_ref, r0)


def _inproj_piece(x_ref, g_ref, wn_ref, wt_ref, wg_ref, bias_ref, rx_ref, rg_ref, qk_ref, vt_ref,
                  ot_ref, gr_ref, uc_ref, r0):
    L = ML_CHUNK
    NH = ML_HEADS
    rows = slice(r0, r0 + TM_SUB)
    h = _rmsnorm(x_ref[rows, :], g_ref[...]).astype(BF16)

    gt = _dot_nt(wg_ref[...], h)
    rowid = lax.broadcasted_iota(jnp.int32, (2 * NH, L), 0)
    head_row = rowid < NH
    rep = lambda col: jnp.broadcast_to(col, (2 * NH, L))
    zeros8 = jnp.zeros((2 * NH, L), F32)
    for cl in range(TM_SUB // L):
        c = r0 // L + cl
        g16 = gt[:, cl * L:(cl + 1) * L] + bias_ref[...]
        tiles = []
        for d, reverse in enumerate((False, True)):
            gates = g16[2 * NH * d:2 * NH * (d + 1)]
            lf = jnp.where(head_row, 0.0, -_softplus(-gates))
            bcum = pltpu.roll(_lane_scan(lf, jnp.add, 0.0, reverse), NH, 0)
            u = jnp.where(head_row, gates - bcum, 0.0)
            last = 0 if reverse else L - 1
            bl = rep(bcum[:, last:last + 1])
            base = G_ROWS * d
            gr_ref[c, base + G_B:base + G_B + 8, :] = bcum
            gr_ref[c, base + G_U:base + G_U + 8, :] = u
            gr_ref[c, base + G_A:base + G_A + 8, :] = bcum + _lane_scan(u, jnp.maximum, -jnp.inf, reverse)
            gr_ref[c, base + G_BL:base + G_BL + 8, :] = bl
            gr_ref[c, base + G_GM:base + G_GM + 8, :] = rep(jnp.max(bl + u, axis=-1, keepdims=True))
            tiles.extend(_split3(u))
        tiles.append(jnp.ones((2 * NH, L), F32))
        tiles.extend([zeros8] * (L // 8 - len(tiles)))
        uc_ref[c * L:(c + 1) * L, :] = jnp.concatenate(tiles, axis=0).T.astype(BF16)

    nat = _dot(h, wn_ref[...])
    rx_ref[rows, :] = nat[:, :D_RG]
    rg_ref[rows, :] = nat[:, D_RG:2 * D_RG].astype(BF16)
    qk_ref[rows, :] = jnp.concatenate(
        [nat[:, 2 * D_RG:2 * D_RG + D_ML] * (ML_HD ** -0.5), nat[:, 2 * D_RG + D_ML:]], axis=-1).astype(BF16)
    tr = _dot_nt(wt_ref[...], h)
    vt_ref[:, rows] = tr[:D_ML].astype(BF16)
    ot_ref[:, rows] = tr[D_ML:].astype(BF16)


def _inproj(x2, g, w_nat, w_tr, w_gate, bias, *, B, S):
    T = B * S
    tm = TM_PROJ
    nb = S // tm
    full = lambda shape: pl.BlockSpec(shape, lambda i: (0,) * len(shape))
    tok = lambda width: pl.BlockSpec((tm, width), lambda i: (i, 0))
    seq_t = pl.BlockSpec((None, D_ML, tm), lambda i: (i // nb, 0, i % nb))
    return pl.pallas_call(
        _inproj_kernel,
        grid=(T // tm,),
        in_specs=[
            tok(D_MODEL), full((1, D_MODEL)), full(w_nat.shape), full(w_tr.shape), full(w_gate.shape),
            full((N_GATE, ML_CHUNK)),
        ],
        out_specs=[
            tok(D_RG), tok(D_RG), tok(2 * D_ML), seq_t, seq_t,
            pl.BlockSpec((tm // ML_CHUNK, 2 * G_ROWS, ML_CHUNK), lambda i: (i, 0, 0)),
            tok(LANES),
        ],
        out_shape=[
            jax.ShapeDtypeStruct((T, D_RG), F32),
            jax.ShapeDtypeStruct((T, D_RG), BF16),
            jax.ShapeDtypeStruct((T, 2 * D_ML), BF16),
            jax.ShapeDtypeStruct((B, D_ML, S), BF16),
            jax.ShapeDtypeStruct((B, D_ML, S), BF16),
            jax.ShapeDtypeStruct((T // ML_CHUNK, 2 * G_ROWS, ML_CHUNK), F32),
            jax.ShapeDtypeStruct((T, LANES), BF16),
        ],
        compiler_params=_params("parallel"),
        name="inproj",
    )(x2, g, w_nat, w_tr, w_gate, bias)


def _rglru_fwd_kernel(x_ref, prev_ref, next_ref, cw_ref, cb_ref, wa_ref, wx_ref, ba_ref, bx_ref,
                      lam_ref, h_ref, xc_ref, slabs, xbuf, pbuf, lbuf, carry, *, nblk, tb):
    blk = pl.program_id(1)

    @pl.when(blk == 0)
    def _():
        carry[...] = jnp.zeros_like(carry)

    npiece = x_ref.shape[0] // tb
    for p in range(npiece):
        rows = pl.ds(tb * p, tb)
        prev = (jnp.where(blk == 0, 0.0, prev_ref[...]) if p == 0
                else x_ref[tb * p - SUBLANES:tb * p, :])
        nxt = (jnp.where(blk == nblk - 1, 0.0, next_ref[...]) if p == npiece - 1
               else x_ref[tb * (p + 1):tb * (p + 1) + SUBLANES, :])
        xc = _rglru_conv(x_ref[rows, :], prev, nxt, cw_ref, cb_ref, slabs.at[p], xbuf.at[p], tb=tb)
        xc_ref[rows, :] = xc
        _rglru_scan(xc, wa_ref, wx_ref, ba_ref, bx_ref, lam_ref, h_ref.at[rows], slabs.at[p],
                    pbuf.at[p], lbuf.at[p], carry, reverse=False, tb=tb)


def _rglru_bwd_kernel(xc_ref, wa_ref, wx_ref, ba_ref, bx_ref, lam_ref, h_ref, slabs, pbuf, lbuf,
                      carry, *, tb):
    @pl.when(pl.program_id(1) == 0)
    def _():
        carry[...] = jnp.zeros_like(carry)

    for p in reversed(range(xc_ref.shape[0] // tb)):
        rows = pl.ds(tb * p, tb)
        _rglru_scan(xc_ref[rows, :], wa_ref, wx_ref, ba_ref, bx_ref, lam_ref, h_ref.at[rows],
                    slabs.at[p], pbuf.at[p], lbuf.at[p], carry, reverse=True, tb=tb)


def _rglru_conv(x, prev, nxt, cw_ref, cb_ref, slabs, xbuf, *, tb):
    nslab = D_RG // LANES
    seg = tb // SUBLANES
    pitch = seg + SUBLANES
    X0 = 2 * SUBLANES

    for k in range(nslab):
        for s in range(SUBLANES):
            slabs[k, pitch * s:pitch * s + seg, :] = x[seg * s:seg * (s + 1), LANES * k:LANES * (k + 1)]
    seam = {}
    for j in range(seg):
        rows = jnp.concatenate(
            [slabs[k, pl.ds(j, SUBLANES, stride=pitch), :] for k in range(nslab)], axis=1)
        xbuf[X0 + SUBLANES * j:X0 + SUBLANES * (j + 1), :] = rows
        if j in (0, seg - 2, seg - 1):
            seam[j] = rows
    row = lax.broadcasted_iota(jnp.int32, (SUBLANES, D_RG), 0)
    tile_row = lambda v, i: jnp.broadcast_to(v[i:i + 1, :], (SUBLANES, D_RG))
    xbuf[0:SUBLANES, :] = jnp.where(row == 0, tile_row(prev, SUBLANES - 2), pltpu.roll(seam[seg - 2], 1, 0))
    xbuf[SUBLANES:X0, :] = jnp.where(row == 0, tile_row(prev, SUBLANES - 1), pltpu.roll(seam[seg - 1], 1, 0))
    xbuf[X0 + tb:, :] = jnp.where(row == SUBLANES - 1, tile_row(nxt, 0),
                                  pltpu.roll(seam[0], SUBLANES - 1, 0))
    xc = cb_ref[...] + xbuf[0:tb, :] * cw_ref[0:1, :]
    xc = xc + xbuf[SUBLANES:SUBLANES + tb, :] * cw_ref[1:2, :]
    xc = xc + xbuf[X0:X0 + tb, :] * cw_ref[2:3, :]
    return xc + xbuf[X0 + SUBLANES:X0 + SUBLANES + tb, :] * cw_ref[3:4, :]


def _rglru_scan(xc, wa_ref, wx_ref, ba_ref, bx_ref, lam_ref, h_ref, slabs, pbuf, lbuf, carry, *,
                reverse, tb):
    nslab = D_RG // LANES
    seg = tb // SUBLANES
    pitch = seg + SUBLANES
    row = lax.broadcasted_iota(jnp.int32, (SUBLANES, D_RG), 0)
    tile_row = lambda v, i: jnp.broadcast_to(v[i:i + 1, :], (SUBLANES, D_RG))

    xcb = xc.astype(BF16)
    r = jax.nn.sigmoid(_dot(xcb, wa_ref[...]) + ba_ref[...])
    i = jax.nn.sigmoid(_dot(xcb, wx_ref[...]) + bx_ref[...])
    decay_rate = RG_C * _softplus(-lam_ref[...])
    a = jnp.exp2((decay_rate * -1.4426950408889634) * r)
    y = jnp.tanh(decay_rate * r) * (a * a + 1.0)
    u = jnp.where(y > 0.0, y * lax.rsqrt(y), 0.0) * (i * xc)

    order = range(seg - 1, -1, -1) if reverse else range(seg)
    P = L = None
    for j in order:
        rs = slice(SUBLANES * j, SUBLANES * (j + 1))
        if P is None:
            P, L = a[rs], u[rs]
        else:
            P, L = a[rs] * P, a[rs] * L + u[rs]
        pbuf[rs, :] = P
        lbuf[rs, :] = L

    A, U = P, L
    for s in (1, 2, 4):
        if reverse:
            keep = row < SUBLANES - s
            shift = SUBLANES - s
        else:
            keep = row >= s
            shift = s
        a_sh = jnp.where(keep, pltpu.roll(A, shift, 0), 1.0)
        u_sh = jnp.where(keep, pltpu.roll(U, shift, 0), 0.0)
        U = A * u_sh + U
        A = A * a_sh
    c_in = carry[...]
    e = U + A * c_in
    if reverse:
        c_seg = jnp.where(row == SUBLANES - 1, c_in, pltpu.roll(e, SUBLANES - 1, 0))
        carry[...] = tile_row(e, 0)
    else:
        c_seg = jnp.where(row == 0, c_in, pltpu.roll(e, 1, 0))
        carry[...] = tile_row(e, SUBLANES - 1)

    for j in range(seg):
        rs = slice(SUBLANES * j, SUBLANES * (j + 1))
        hj = lbuf[rs, :] + pbuf[rs, :] * c_seg
        for k in range(nslab):
            slabs[k, pl.ds(j, SUBLANES, stride=pitch), :] = hj[:, LANES * k:LANES * (k + 1)]
    for k in range(nslab):
        for s in range(SUBLANES):
            h_ref[seg * s:seg * (s + 1), LANES * k:LANES * (k + 1)] = (
                slabs[k, pitch * s:pitch * s + seg, :].astype(h_ref.dtype))


def _rglru(x, cw, cb, gates_fwd, gates_bwd, *, B, S):
    T = B * S
    tile = TB_RG
    tb = TM_SUB
    npiece = tile // tb
    nblk = S // tile
    hb = tile // SUBLANES
    n_halo = T // SUBLANES
    full = lambda shape: pl.BlockSpec(shape, lambda b, j: (0,) * len(shape))
    gate_specs = [full((D_RG, D_RG)), full((D_RG, D_RG)), full((1, D_RG)), full((1, D_RG)), full((1, D_RG))]
    slabs = pltpu.VMEM((npiece, D_RG // LANES, tb + SUBLANES * SUBLANES, LANES), F32)
    scan_scratch = [
        pltpu.VMEM((npiece, tb, D_RG), F32),
        pltpu.VMEM((npiece, tb, D_RG), F32),
        pltpu.VMEM((SUBLANES, D_RG), F32),
    ]
    tile_fwd = pl.BlockSpec((tile, D_RG), lambda b, j: (b * nblk + j, 0))
    tile_bwd = pl.BlockSpec((tile, D_RG), lambda b, j: (b * nblk + nblk - 1 - j, 0))

    h_fwd, xc = pl.pallas_call(
        functools.partial(_rglru_fwd_kernel, nblk=nblk, tb=tb),
        grid=(B, nblk),
        in_specs=[
            tile_fwd,
            pl.BlockSpec((SUBLANES, D_RG), lambda b, j: (jnp.maximum((b * nblk + j) * hb - 1, 0), 0)),
            pl.BlockSpec((SUBLANES, D_RG), lambda b, j: (jnp.minimum((b * nblk + j + 1) * hb, n_halo - 1), 0)),
            full((4, D_RG)), full((1, D_RG)), *gate_specs,
        ],
        out_specs=[tile_fwd, tile_fwd],
        out_shape=[jax.ShapeDtypeStruct((T, D_RG), BF16),
                   jax.ShapeDtypeStruct((T, D_RG), F32)],
        scratch_shapes=[slabs, pltpu.VMEM((npiece, tb + 3 * SUBLANES, D_RG), F32), *scan_scratch],
        compiler_params=_params("parallel", "arbitrary"),
        name="rglru_fwd",
    )(x, x, x, cw, cb, *gates_fwd)
    h_bwd = pl.pallas_call(
        functools.partial(_rglru_bwd_kernel, tb=tb),
        grid=(B, nblk),
        in_specs=[tile_bwd, *gate_specs],
        out_specs=tile_bwd,
        out_shape=jax.ShapeDtypeStruct((T, D_RG), BF16),
        scratch_shapes=[slabs, *scan_scratch],
        compiler_params=_params("parallel", "arbitrary"),
        name="rglru_bwd",
    )(xc, *gates_bwd)
    return h_fwd, h_bwd


def _mlstm_kernel(q_ref, k_ref, vt_ref, gr_ref, uc_ref, h_ref, ct_st, n_st, m_st, *, reverse, bb):
    L = ML_CHUNK
    NH = ML_HEADS
    d = 1 if reverse else 0

    @pl.when(pl.program_id(1) == 0)
    def _():
        ct_st[...] = jnp.zeros_like(ct_st)
        n_st[...] = jnp.zeros_like(n_st)
        m_st[...] = jnp.zeros_like(m_st)

    row8 = lax.broadcasted_iota(jnp.int32, (SUBLANES, L), 0)
    s_id = lax.broadcasted_iota(jnp.int32, (L, L), 0)
    t_id = lax.broadcasted_iota(jnp.int32, (L, L), 1)
    valid = (s_id >= t_id) if reverse else (s_id <= t_id)
    zeros8 = jnp.zeros((SUBLANES, L), F32)
    tile = lambda rows, hd: jnp.broadcast_to(rows[hd:hd + 1, :], (L, L))
    row_of = lambda rows, hd: jnp.broadcast_to(rows[hd:hd + 1, :], (SUBLANES, L))
    pairs = [(b, hp) for b in range(bb) for hp in range(NH // 2)]

    seq = []
    for b in range(bb):
        base = G_ROWS * d
        bcum = gr_ref[b, base + G_B:base + G_B + 8, :]
        u = gr_ref[b, base + G_U:base + G_U + 8, :]
        b_last = gr_ref[b, base + G_BL:base + G_BL + 8, :]
        m_prev = m_st[b]
        inter = bcum + m_prev
        m_t = jnp.maximum(inter, gr_ref[b, base + G_A:base + G_A + 8, :])
        m_new = jnp.maximum(b_last + m_prev, gr_ref[b, base + G_GM:base + G_GM + 8, :])
        n_prev = n_st[b]
        wg = jnp.exp(u + (b_last - m_new))
        seq.append(dict(
            w_int=jnp.exp(inter - m_t), e_neg=jnp.exp(-m_t), v3=_split3(bcum - m_t),
            decay=jnp.exp(b_last + m_prev - m_new), m_new=m_new, n_prev=n_prev, wg=wg,
            n_lhs=jnp.concatenate([n_prev, zeros8], axis=0).astype(BF16),
            wg_lhs=jnp.concatenate([wg, zeros8], axis=0).astype(BF16)))

    cs = lambda hd: slice(hd * ML_HD, (hd + 1) * ML_HD)
    side = lambda i: slice(i * L, (i + 1) * L)
    twice = lambda x: jnp.concatenate([x, x], axis=1)

    def blockdiag(a0, a1):
        z = jnp.zeros_like(a0)
        return jnp.concatenate([jnp.concatenate([a0, z], axis=1), jnp.concatenate([z, a1], axis=1)], axis=0)

    def expo_rhs(b, hd):
        onehot = jnp.where(row8 == hd, 1.0, 0.0)
        v_hi, v_mid, v_lo = (row_of(x, hd) for x in seq[b]["v3"])
        v_rows = jnp.where(row8 == 0, v_hi, jnp.where(row8 == 1, v_mid, jnp.where(row8 == 2, v_lo, 0.0)))
        slabs = [zeros8] * (L // SUBLANES)
        for j in range(3):
            slabs[(UC_DIR * d) // SUBLANES + j] = onehot
        slabs[UC_ONES // SUBLANES] = v_rows
        return jnp.concatenate(slabs, axis=0).astype(BF16)

    st, expo, inter, upd = {}, {}, {}, {}
    for b, hp in pairs:
        sq = seq[b]
        heads = (2 * hp, 2 * hp + 1)
        q_diag = blockdiag(q_ref[b, :, cs(heads[0])], q_ref[b, :, cs(heads[1])])
        sq_all = _dot_nt(
            jnp.concatenate([k_ref[b, :, heads[0] * ML_HD:(heads[1] + 1) * ML_HD], twice(sq["n_lhs"]),
                             jnp.concatenate([ct_st[b, hd].astype(BF16) for hd in heads], axis=1)],
                            axis=0), q_diag)
        st[b, hp] = sq_all[0:L + BF16_ROWS]
        inter[b, hp] = sq_all[L + BF16_ROWS:]
        expo[b, hp] = _dot(uc_ref[b], jnp.concatenate([expo_rhs(b, hd) for hd in heads], axis=1))
        vw = jnp.concatenate([(vt_ref[b, cs(hd), :].astype(F32) * sq["wg"][hd:hd + 1, :]).astype(BF16)
                              for hd in heads], axis=1)
        upd[b, hp] = _dot(jnp.concatenate([vw, twice(sq["wg_lhs"])], axis=0),
                          blockdiag(k_ref[b, :, cs(heads[0])], k_ref[b, :, cs(heads[1])]))

    for b, hp in pairs:
        sq = seq[b]
        for i, hd in enumerate((2 * hp, 2 * hp + 1)):
            ct_st[b, hd] = tile(sq["decay"], hd) * ct_st[b, hd] + upd[b, hp][0:L, side(i)]
            n_st[b, hd:hd + 1, :] = (sq["decay"][hd:hd + 1, :] * sq["n_prev"][hd:hd + 1, :]
                                     + upd[b, hp][L + hd:L + hd + 1, side(i)])
    for b in range(bb):
        m_st[b] = seq[b]["m_new"]

    for b, hp in pairs:
        sq = seq[b]
        heads = (2 * hp, 2 * hp + 1)
        p_t, rw = [], []
        for i, hd in enumerate(heads):
            s_t = st[b, hp][0:L, side(i)] * jnp.exp(jnp.where(valid, expo[b, hp][:, side(i)], -jnp.inf))
            w_h = sq["w_int"][hd:hd + 1, :]
            den = jnp.sum(s_t, axis=0, keepdims=True) + w_h * st[b, hp][L + hd:L + hd + 1, side(i)]
            r = 1.0 / jnp.maximum(jnp.abs(den), sq["e_neg"][hd:hd + 1, :])
            p_t.append((s_t * r).astype(BF16))
            rw.append(r * w_h)
        intra = _dot(jnp.concatenate([vt_ref[b, cs(hd), :] for hd in heads], axis=1), blockdiag(*p_t))
        out = (intra + inter[b, hp] * jnp.concatenate(rw, axis=1)).astype(h_ref.dtype)
        for i, hd in enumerate(heads):
            h_ref[b, cs(hd), :] = out[:, side(i)]


def _mlstm(qk3, vt, gr4, uc3, *, reverse):
    B, S, _ = qk3.shape
    L = ML_CHUNK
    nc = S // L
    bb = ML_SEQS

    def chunk(c):
        return (nc - 1 - c) if reverse else c

    kern = functools.partial(_mlstm_kernel, reverse=reverse, bb=bb)
    return pl.pallas_call(
        kern,
        grid=(B // bb, nc),
        in_specs=[
            pl.BlockSpec((bb, L, D_ML), lambda b, c: (b, chunk(c), 0)),
            pl.BlockSpec((bb, L, D_ML), lambda b, c: (b, chunk(c), 1)),
            pl.BlockSpec((bb, D_ML, L), lambda b, c: (b, 0, chunk(c))),
            pl.BlockSpec((bb, None, 2 * G_ROWS, L), lambda b, c: (b, chunk(c), 0, 0)),
            pl.BlockSpec((bb, L, LANES), lambda b, c: (b, chunk(c), 0)),
        ],
        out_specs=pl.BlockSpec((bb, D_ML, L), lambda b, c: (b, 0, chunk(c))),
        out_shape=jax.ShapeDtypeStruct((B, D_ML, S), BF16),
        scratch_shapes=[
            pltpu.VMEM((bb, ML_HEADS, ML_HD, ML_HD), F32),
            pltpu.VMEM((bb, 2 * ML_HEADS, ML_HD), F32),
            pltpu.VMEM((bb, 2 * ML_HEADS, LANES), F32),
        ],
        compiler_params=_params("parallel", "arbitrary"),
        name="mlstm_bwd" if reverse else "mlstm_fwd",
    )(qk3, qk3, vt, gr4, uc3)


def _outproj_kernel(rf_ref, rb_ref, gate_ref, mf_ref, mb_ref, ot_ref, mg_ref, wr_ref, wm_ref, y_ref):
    for r0 in range(0, y_ref.shape[0], TM_SUB):
        rows = slice(r0, r0 + TM_SUB)
        y_rg = _gelu_gate(gate_ref[rows, :].astype(F32),
                          rf_ref[rows, :].astype(F32) + rb_ref[rows, :].astype(F32))
        acc = _dot(y_rg.astype(BF16), wr_ref[...])
        h_t = mf_ref[:, rows].astype(F32) + mb_ref[:, rows].astype(F32)
        parts = []
        for hd in range(ML_HEADS):
            hh = h_t[hd * ML_HD:(hd + 1) * ML_HD]
            parts.append(hh * lax.rsqrt(jnp.mean(hh * hh, axis=0, keepdims=True) + EPS))
        mg = jnp.tile(mg_ref[...], (1, TM_SUB // LANES))
        y_t = jax.nn.sigmoid(ot_ref[:, rows].astype(F32)) * (jnp.concatenate(parts, axis=0) * mg)
        y_ref[rows, :] = acc + _dot(y_t.T.astype(BF16), wm_ref[...])


def _outproj(rf, rb, pa, mf_t, mb_t, o_t, mg_tile, w_rg, w_ml, *, S):
    T = rf.shape[0]
    tm = TM_PROJ
    nb = S // tm
    tok = lambda width, col: pl.BlockSpec((tm, width), lambda i: (i, col))
    seq_t = pl.BlockSpec((None, D_ML, tm), lambda i: (i // nb, 0, i % nb))
    full = lambda shape: pl.BlockSpec(shape, lambda i: (0,) * len(shape))
    return pl.pallas_call(
        _outproj_kernel,
        grid=(T // tm,),
        in_specs=[
            tok(D_RG, 0), tok(D_RG, 0), tok(D_RG, 0),
            seq_t, seq_t, seq_t,
            full((D_ML, LANES)), full((D_RG, D_MODEL)), full((D_ML, D_MODEL)),
        ],
        out_specs=tok(D_MODEL, 0),
        out_shape=jax.ShapeDtypeStruct((T, D_MODEL), F32),
        compiler_params=_params("parallel"),
        name="outproj",
    )(rf, rb, pa, mf_t, mb_t, o_t, mg_tile, w_rg, w_ml)


def _ffn_kernel(x_ref, prev_ref, next_ref, d_ref, dprev_ref, dnext_ref, g_ref, wu_ref, cw_ref, cb_ref,
                wd_ref, fg_ref, y_ref, slabs, hbuf, uvbuf, acts, acc, *, nblk, tb, final):
    blk = pl.program_id(0) % nblk
    sub = FF_SUB
    nsub = D_FF // sub
    nring = uvbuf.shape[0]
    nslab = D_MODEL // LANES
    seg = tb // SUBLANES
    pitch = seg + SUBLANES

    g = g_ref[...]
    hn = _rmsnorm(x_ref[...] + d_ref[...], g)
    for k in range(nslab):
        for s in range(SUBLANES):
            slabs[k, pitch * s:pitch * s + seg, :] = hn[seg * s:seg * (s + 1), LANES * k:LANES * (k + 1)]

    def perm_rows(j):
        return jnp.concatenate(
            [slabs[k, pl.ds(j, SUBLANES, stride=pitch), :] for k in range(nslab)], axis=1)

    for jj in range(seg // 2):
        hbuf[BF16_ROWS * jj:BF16_ROWS * (jj + 1), :] = jnp.concatenate(
            [perm_rows(2 * jj), perm_rows(2 * jj + 1)], axis=0).astype(BF16)
    row_x = lax.broadcasted_iota(jnp.int32, (SUBLANES, D_MODEL), 0)
    h_prev = jnp.where(blk == 0, 0.0, pltpu.roll(_rmsnorm(prev_ref[...] + dprev_ref[...], g), 1, 0))
    h_next = jnp.where(blk == nblk - 1, 0.0,
                       pltpu.roll(_rmsnorm(next_ref[...] + dnext_ref[...], g), 1, 0))
    halo = jnp.where(row_x == 0, h_prev, jnp.where(row_x == 1, h_next, 0.0))
    hbuf[tb:, :] = jnp.concatenate([halo, jnp.zeros_like(halo)], axis=0).astype(BF16)

    row_u = lax.broadcasted_iota(jnp.int32, (SUBLANES, 2 * sub), 0)

    def pair(ref, sc):
        return jnp.concatenate([ref[:, sub * sc:sub * (sc + 1)],
                                ref[:, D_FF + sub * sc:D_FF + sub * (sc + 1)]], axis=1)

    def up(sc):
        slot = sc % nring
        res = jnp.concatenate([_dot(hbuf[...], wu_ref[:, sub * sc:sub * (sc + 1)]),
                               _dot(hbuf[...], wu_ref[:, D_FF + sub * sc:D_FF + sub * (sc + 1)])], axis=1)
        uvbuf[slot, SUBLANES:SUBLANES + tb, :] = res[0:tb]
        uvbuf[slot, 0:SUBLANES, :] = jnp.where(
            row_u == 0, jnp.broadcast_to(res[tb:tb + 1], row_u.shape),
            pltpu.roll(res[tb - SUBLANES:tb], 1, 0))
        uvbuf[slot, SUBLANES + tb:, :] = jnp.where(
            row_u == SUBLANES - 1, jnp.broadcast_to(res[tb + 1:tb + 2], row_u.shape),
            pltpu.roll(res[0:SUBLANES], SUBLANES - 1, 0))

    def gate(sc):
        slot = sc % nring
        cw = pair(cw_ref, sc)
        cb = pair(cb_ref, sc)
        for r0 in range(0, tb, FF_ROWS):
            c = cb + uvbuf[slot, r0:r0 + FF_ROWS, :] * cw[0:1]
            c = c + uvbuf[slot, SUBLANES + r0:SUBLANES + r0 + FF_ROWS, :] * cw[1:2]
            c = c + uvbuf[slot, 2 * SUBLANES + r0:2 * SUBLANES + r0 + FF_ROWS, :] * cw[2:3]
            acts[r0:r0 + FF_ROWS, sub * sc:sub * (sc + 1)] = _gelu_gate(c[:, :sub], c[:, sub:]).astype(BF16)

    up(0)
    up(1)
    for sc in range(nsub):
        if sc + 2 < nsub:
            up(sc + 2)
        gate(sc)

    acc[...] = _dot(acts[...], wd_ref[...])

    for j in range(seg):
        for k in range(nslab):
            slabs[k, pl.ds(j, SUBLANES, stride=pitch), :] = acc[SUBLANES * j:SUBLANES * (j + 1),
                                                                LANES * k:LANES * (k + 1)]
    ffn = jnp.concatenate(
        [jnp.concatenate([slabs[k, pitch * s:pitch * s + seg, :] for s in range(SUBLANES)], axis=0)
         for k in range(nslab)], axis=1)
    y = (x_ref[...] + d_ref[...]) + ffn
    if final:
        y = _rmsnorm(y, fg_ref[...])
    y_ref[...] = y


def _ffn(x2, d2, g, w_up, cw, cb, w_down, fg, *, S, final):
    T = x2.shape[0]
    tb = TB_FFN
    nblk = S // tb
    hpb = tb // SUBLANES
    n_halo = T // SUBLANES
    full = lambda shape: pl.BlockSpec(shape, lambda i: (0,) * len(shape))
    tile = pl.BlockSpec((tb, D_MODEL), lambda i: (i, 0))
    halo_prev = pl.BlockSpec((SUBLANES, D_MODEL), lambda i: (jnp.maximum(i * hpb - 1, 0), 0))
    halo_next = pl.BlockSpec((SUBLANES, D_MODEL), lambda i: (jnp.minimum((i + 1) * hpb, n_halo - 1), 0))
    kern = functools.partial(_ffn_kernel, nblk=nblk, tb=tb, final=final)
    return pl.pallas_call(
        kern,
        grid=(T // tb,),
        in_specs=[
            tile, halo_prev, halo_next, tile, halo_prev, halo_next,
            full((1, D_MODEL)), full((D_MODEL, 2 * D_FF)), full((3, 2 * D_FF)), full((1, 2 * D_FF)),
            full((D_FF, D_MODEL)), full((1, D_MODEL)),
        ],
        out_specs=pl.BlockSpec((tb, D_MODEL), lambda i: (i, 0)),
        out_shape=jax.ShapeDtypeStruct((T, D_MODEL), F32),
        scratch_shapes=[
            pltpu.VMEM((D_MODEL // LANES, tb + SUBLANES * SUBLANES, LANES), F32),
            pltpu.VMEM((tb + BF16_ROWS, D_MODEL), BF16),
            pltpu.VMEM((FF_RING, tb + 2 * SUBLANES, 2 * FF_SUB), F32),
            pltpu.VMEM((tb, D_FF), BF16),
            pltpu.VMEM((tb, D_MODEL), F32),
        ],
        compiler_params=pltpu.CompilerParams(dimension_semantics=("parallel",),
                                             vmem_limit_bytes=VMEM_LIMIT_FFN),
        name="convffn",
    )(x2, x2, x2, d2, d2, d2, g, w_up, cw, cb, w_down, fg)


def _block_diag(w):
    eye = jnp.eye(RG_BLOCKS, dtype=w.dtype)
    return jnp.einsum('ncd,nm->ncmd', w, eye).reshape(D_RG, D_RG)


def _encoder(x, norm1_g, w_in, b_gates, rg_conv_w, rg_conv_b, rg_wa, rg_ba, rg_wx, rg_bx, rg_lambda,
             ml_norm_g, w_out, norm2_g, w_up, ffn_conv_w, ffn_conv_b, w_down, final_g):
    B, S, _ = x.shape
    T = B * S
    depth = w_in.shape[0]
    x2 = x.reshape(T, D_MODEL)
    row = lambda v: v.reshape(1, -1).astype(F32)
    n_nat = 2 * D_RG + 2 * D_ML
    for l in range(depth):
        w_nat = w_in[l, :, :n_nat].astype(BF16)
        w_tr = w_in[l, :, n_nat:n_nat + 2 * D_ML].T.astype(BF16)
        w_gate = w_in[l, :, n_nat + 2 * D_ML:].T.astype(BF16)
        bias = jnp.broadcast_to(b_gates[l].astype(F32).reshape(N_GATE, 1), (N_GATE, ML_CHUNK))
        rx, rgate, qk, v_t, o_t, gr, uc = _inproj(x2, row(norm1_g[l]), w_nat, w_tr, w_gate, bias,
                                                  B=B, S=S)
        qk3 = qk.reshape(B, S, 2 * D_ML)
        gr4 = gr.reshape(B, S // ML_CHUNK, 2 * G_ROWS, ML_CHUNK)
        uc3 = uc.reshape(B, S, LANES)
        rg_gates = [(_block_diag(rg_wa[l, d]).astype(BF16), _block_diag(rg_wx[l, d]).astype(BF16),
                     row(rg_ba[l, d]), row(rg_bx[l, d]), row(rg_lambda[l, d])) for d in range(2)]
        r_dir = _rglru(rx, rg_conv_w[l].astype(F32), row(rg_conv_b[l]), *rg_gates, B=B, S=S)
        m_dir = [_mlstm(qk3, v_t, gr4, uc3, reverse=reverse) for reverse in (False, True)]
        wo = w_out[l].astype(BF16)
        mg_tile = jnp.broadcast_to(ml_norm_g[l].astype(F32).reshape(D_ML, 1), (D_ML, LANES))
        mixed = _outproj(r_dir[0], r_dir[1], rgate, m_dir[0], m_dir[1], o_t, mg_tile,
                         wo[:D_RG], wo[D_RG:], S=S)
        x2 = _ffn(x2, mixed, row(norm2_g[l]), w_up[l].astype(BF16),
                  ffn_conv_w[l].astype(F32), row(ffn_conv_b[l]),
                  w_down[l].astype(BF16), row(final_g), S=S,
                  final=(l == depth - 1))
    return x2.reshape(B, S, D_MODEL)


def kernel(x_prompt, x_sample, norm1_g, w_in, b_gates, rg_conv_w, rg_conv_b, rg_wa, rg_ba, rg_wx, rg_bx,
           rg_lambda, ml_norm_g, w_out, norm2_g, w_up, ffn_conv_w, ffn_conv_b, w_down, final_g):
    weights = (norm1_g, w_in, b_gates, rg_conv_w, rg_conv_b, rg_wa, rg_ba, rg_wx, rg_bx, rg_lambda,
               ml_norm_g, w_out, norm2_g, w_up, ffn_conv_w, ffn_conv_b, w_down, final_g)
    return (_encoder(x_prompt, *weights), _encoder(x_sample, *weights))
```
